```python
import math
import jax, jax.numpy as jnp
from jax import lax
import numpy as np

D_MODEL = 4096
BATCH = 4
SEQ = 4096
DEPTH = 2
DEC_BATCH = 8
DEC_SEQ = 16
PAST_LEN = 1024

CHUNK = 64
N_EVEN = (DEPTH + 1) // 2
N_ODD = DEPTH // 2
D_FF = 2 * D_MODEL
EPS = 1e-6
MIX_HALF = D_MODEL // 2

RW_HEAD_DIM = 64
RW_HEADS = MIX_HALF // RW_HEAD_DIM
RW_WIDTH = RW_HEADS * RW_HEAD_DIM
RW_W_LORA = 128
RW_A_LORA = 128
RW_G_LORA = 512
RW_IN = 3 * RW_WIDTH + RW_W_LORA + RW_A_LORA + RW_G_LORA
RW_GN_EPS = 64e-5
S5_WIDTH = MIX_HALF
S5_GROUP = 16
S5_GROUPS = S5_WIDTH // S5_GROUP
S5_STATE = 64
EVEN_IN = RW_IN + S5_WIDTH
EVEN_OUT = RW_WIDTH + S5_WIDTH
GDN_DK = 128
GDN_DV = 128
GDN_HEADS = MIX_HALF // GDN_DV
GDN_KW = GDN_HEADS * GDN_DK
GDN_VW = GDN_HEADS * GDN_DV
GDN_CONV = 4
GDN_CONV_CH = 2 * GDN_KW + GDN_VW
GDN_IN = GDN_CONV_CH + GDN_VW + 2 * GDN_HEADS
HG_DK = 128
HG_DV = 128
HG_HEADS = MIX_HALF // HG_DV
HG_KW = HG_HEADS * HG_DK
HG_VW = HG_HEADS * HG_DV
HG_IN = 2 * HG_KW + 2 * HG_VW
ODD_IN = GDN_IN + HG_IN
ODD_OUT = GDN_VW + HG_VW

kernel_name = 'hybrid_streaming_encoder_step'


def rmsnorm(x, g):
    xf = x.astype(jnp.float32)
    y = xf * lax.rsqrt(jnp.mean(xf * xf, axis=-1, keepdims=True) + EPS)
    return (y * g).astype(x.dtype)


def head_rmsnorm(o, w):
    return o * lax.rsqrt(jnp.mean(o * o, axis=-1, keepdims=True) + EPS) * w


def l2norm(t):
    return t * lax.rsqrt(jnp.sum(t * t, axis=-1, keepdims=True) + 1e-12)


def adaln(h, m):
    return h * (1.0 + m[:, 1][:, None]) + m[:, 0][:, None]


def swiglu(h, w_in, w_out):
    gu = h @ w_in
    return (jax.nn.silu(gu[..., :D_FF]) * gu[..., D_FF:]) @ w_out


def causal_conv(z, buf, w):
    K = w.shape[0]
    L = z.shape[1]
    zp = jnp.concatenate([buf.astype(z.dtype), z], axis=1)
    out = zp[:, 0:L] * w[0]
    for j in range(1, K):
        out = out + zp[:, j:j + L] * w[j]
    return out, zp[:, L:]


def _blocks(t, block):
    B, L = t.shape[0], t.shape[1]
    t = t.reshape(B, L // block, block, *t.shape[2:])
    return jnp.swapaxes(t, 2, 3)


def rwkv7_mixer(z, shift, S0, p, e):
    B, L, _ = z.shape
    zf = z.astype(jnp.float32)
    prev = jnp.concatenate([shift[:, None].astype(jnp.float32), zf[:, :-1]], axis=1)
    zs = zf + p['rw_mu'][e] * (prev - zf)
    c1, c2, c3 = RW_WIDTH, 2 * RW_WIDTH, 3 * RW_WIDTH
    c4 = c3 + RW_W_LORA
    c5 = c4 + RW_A_LORA
    r, k, v = zs[..., :c1], zs[..., c1:c2], zs[..., c2:c3]
    xw, xa, xg = zs[..., c3:c4], zs[..., c4:c5], zs[..., c5:]
    w = -jax.nn.softplus(-(p['rw_w0'][e] + jnp.tanh(xw) @ p['rw_w_up'][e])) - 0.5
    decay = jnp.exp(-jnp.exp(w))
    a = jax.nn.sigmoid(p['rw_a0'][e] + xa @ p['rw_a_up'][e])
    g = jax.nn.sigmoid(xg) @ p['rw_g_up'][e]
    heads = lambda t: t.reshape(B, L, RW_HEADS, RW_HEAD_DIM)
    kk = l2norm(heads(k * p['rw_k_k'][e]))
    k = k * (1.0 + (a - 1.0) * p['rw_k_a'][e])
    rh, kh, vh, ah, dh = heads(r), heads(k), heads(v), heads(a), heads(decay)

    def step(S, inp):
        r_t, d_t, k_t, v_t, kk_t, a_t = inp
        sa = jnp.einsum('bhij,bhj->bhi', S, kk_t)
        S = S * d_t[:, :, None, :] - sa[..., None] * (kk_t * a_t)[:, :, None, :] + v_t[..., None] * k_t[:, :, None, :]
        return S, jnp.einsum('bhij,bhj->bhi', S, r_t)

    xs = tuple(jnp.moveaxis(t, 1, 0) for t in (rh, dh, kh, vh, kk, ah))
    S_fin, o = lax.scan(step, S0.astype(jnp.float32), xs)
    o = jnp.moveaxis(o, 0, 1)
    mu = jnp.mean(o, axis=-1, keepdims=True)
    var = jnp.mean(jnp.square(o - mu), axis=-1, keepdims=True)
    o = ((o - mu) * lax.rsqrt(var + RW_GN_EPS)).reshape(B, L, RW_WIDTH) * p['rw_ln_w'][e] + p['rw_ln_b'][e]
    bonus = jnp.sum(rh * kh * p['rw_r_k'][e], axis=-1, keepdims=True) * vh
    o = (o + bonus.reshape(B, L, RW_WIDTH)) * g
    return o, z[:, -1].astype(shift.dtype), S_fin.astype(S0.dtype)


def s5_mixer(u, h_re, h_im, p, e):
    B, L, _ = u.shape
    uf = u.astype(jnp.float32).reshape(B, L, S5_GROUPS, S5_GROUP)
    dt = jnp.exp(p['s5_log_dt'][e])[:, None]
    lr, li = p['s5_lambda_re'][e], p['s5_lambda_im'][e]
    mag = jnp.exp(lr * dt)
    ab_re, ab_im = mag * jnp.cos(li * dt), mag * jnp.sin(li * dt)
    den = lr * lr + li * li
    pr, pi_ = ab_re - 1.0, ab_im
    coef_re = (pr * lr + pi_ * li) / den
    coef_im = (pi_ * lr - pr * li) / den
    b_re, b_im = p['s5_b_re'][e], p['s5_b_im'][e]
    bb_re = coef_re[..., None] * b_re - coef_im[..., None] * b_im
    bb_im = coef_re[..., None] * b_im + coef_im[..., None] * b_re
    bu_re = jnp.einsum('blgm,gnm->blgn', uf, bb_re)
    bu_im = jnp.einsum('blgm,gnm->blgn', uf, bb_im)
    hr0, hi0 = h_re.astype(jnp.float32), h_im.astype(jnp.float32)
    bu_re = bu_re.at[:, 0].add(ab_re * hr0 - ab_im * hi0)
    bu_im = bu_im.at[:, 0].add(ab_re * hi0 + ab_im * hr0)
    a_re = jnp.broadcast_to(ab_re[None, None], (1, L, S5_GROUPS, S5_STATE))
    a_im = jnp.broadcast_to(ab_im[None, None], (1, L, S5_GROUPS, S5_STATE))

    def combine(e1, e2):
        a1r, a1i, b1r, b1i = e1
        a2r, a2i, b2r, b2i = e2
        return (a2r * a1r - a2i * a1i, a2r * a1i + a2i * a1r,
                a2r * b1r - a2i * b1i + b2r, a2r * b1i + a2i * b1r + b2i)

    _, _, hr, hi = lax.associative_scan(combine, (a_re, a_im, bu_re, bu_im), axis=1)
    y = (jnp.einsum('blgn,gmn->blgm', hr, p['s5_c_re'][e]) - jnp.einsum('blgn,gmn->blgm', hi, p['s5_c_im'][e])
         + p['s5_d'][e] * uf)
    yg = jax.nn.gelu(y)
    out = yg * jax.nn.sigmoid(jnp.einsum('blgm,gmk->blgk', yg, p['s5_glu_w'][e]) + p['s5_glu_b'][e])
    return out.reshape(B, L, S5_WIDTH), hr[:, -1].astype(h_re.dtype), hi[:, -1].astype(h_im.dtype)


def gated_delta_chunked(q, k, v, g, beta, S0, block):
    B, L, H, DV = v.shape
    q, k, v, g, beta = (_blocks(t, block) for t in (q, k, v, g, beta))
    gc = jnp.cumsum(g, axis=-1)
    incl = jnp.tril(jnp.ones((block, block), dtype=bool))
    strict = jnp.tril(jnp.ones((block, block), dtype=bool), -1)
    diff = gc[..., :, None] - gc[..., None, :]
    dmask = jnp.where(incl, jnp.exp(jnp.where(incl, diff, 0.0)), 0.0)
    kb = k * beta[..., None]
    m = jnp.where(strict, jnp.einsum('bnhtd,bnhsd->bnhts', kb, k) * dmask, 0.0)
    rhs = jnp.concatenate([v * beta[..., None], kb * jnp.exp(gc)[..., None]], axis=-1)
    sol = lax.linalg.triangular_solve(m + jnp.eye(block, dtype=m.dtype), rhs, left_side=True, lower=True)
    u, w = sol[..., :DV], sol[..., DV:]
    attn = jnp.einsum('bnhtd,bnhsd->bnhts', q, k) * dmask
    qg = q * jnp.exp(gc)[..., None]
    kdec = k * jnp.exp(gc[..., -1:] - gc)[..., None]
    glast = jnp.exp(gc[..., -1])

    def step(S, inp):
        u_c, w_c, a_c, qg_c, kd_c, gl_c = inp
        v_new = u_c - jnp.einsum('bhtd,bhde->bhte', w_c, S)
        o = jnp.einsum('bhtd,bhde->bhte', qg_c, S) + jnp.einsum('bhts,bhse->bhte', a_c, v_new)
        S = S * gl_c[..., None, None] + jnp.einsum('bhtd,bhte->bhde', kd_c, v_new)
        return S, o

    xs = tuple(jnp.moveaxis(t, 1, 0) for t in (u, w, attn, qg, kdec, glast))
    S_fin, o = lax.scan(step, S0, xs)
    o = jnp.transpose(o, (1, 0, 3, 2, 4)).reshape(B, L, H, DV)
    return o, S_fin


def gla_chunked(q, k, v, logf, S0, block):
    B, L, H, DV = v.shape
    q, k, v, logf = (_blocks(t, block) for t in (q, k, v, logf))
    bc = jnp.cumsum(logf, axis=3)
    incl = jnp.tril(jnp.ones((block, block), dtype=bool))[:, :, None]

    def step(S, inp):
        q_c, k_c, v_c, b_c = inp
        diff = b_c[:, :, :, None, :] - b_c[:, :, None, :, :]
        dec = jnp.where(incl, jnp.exp(jnp.where(incl, diff, 0.0)), 0.0)
        scores = jnp.sum(q_c[:, :, :, None, :] * k_c[:, :, None, :, :] * dec, axis=-1)
        o = jnp.einsum('bhtd,bhde->bhte', q_c * jnp.exp(b_c), S) + jnp.einsum('bhts,bhse->bhte', scores, v_c)
        b_last = b_c[:, :, -1]
        S = jnp.exp(b_last)[..., None] * S + jnp.einsum('bhtd,bhte->bhde', k_c * jnp.exp(b_last[:, :, None] - b_c), v_c)
        return S, o

    xs = tuple(jnp.moveaxis(t, 1, 0) for t in (q, k, v, bc))
    S_fin, o = lax.scan(step, S0, xs)
    o = jnp.transpose(o, (1, 0, 3, 2, 4)).reshape(B, L, H, DV)
    return o, S_fin


def gdn_mixer(z, conv_buf, S0, p, o):
    B, L, _ = z.shape
    zf = z.astype(jnp.float32)
    c1 = GDN_CONV_CH
    c2 = c1 + GDN_VW
    c3 = c2 + GDN_HEADS
    qkv, y_gate, a_pre, b_pre = zf[..., :c1], zf[..., c1:c2], zf[..., c2:c3], zf[..., c3:]
    qkv, new_buf = causal_conv(qkv, conv_buf, p['gdn_conv_w'][o])
    qkv = jax.nn.silu(qkv)
    q = l2norm(qkv[..., :GDN_KW].reshape(B, L, GDN_HEADS, GDN_DK)) * GDN_DK ** -0.5
    k = l2norm(qkv[..., GDN_KW:2 * GDN_KW].reshape(B, L, GDN_HEADS, GDN_DK))
    v = qkv[..., 2 * GDN_KW:].reshape(B, L, GDN_HEADS, GDN_DV)
    g = -jnp.exp(p['gdn_a_log'][o]) * jax.nn.softplus(a_pre + p['gdn_dt_bias'][o])
    beta = jax.nn.sigmoid(b_pre)
    out, S = gated_delta_chunked(q, k, v, g, beta, S0.astype(jnp.float32), min(CHUNK, L))
    out = head_rmsnorm(out, p['gdn_norm_w'][o]) * jax.nn.silu(y_gate.reshape(B, L, GDN_HEADS, GDN_DV))
    return out.reshape(B, L, GDN_VW), new_buf.astype(conv_buf.dtype), S.astype(S0.dtype)


def hgrn2_mixer(z, S0, p, o, lb):
    B, L, _ = z.shape
    zf = z.astype(jnp.float32)
    q = zf[..., :HG_KW].reshape(B, L, HG_HEADS, HG_DK)
    f_pre = zf[..., HG_KW:2 * HG_KW].reshape(B, L, HG_HEADS, HG_DK)
    i = zf[..., 2 * HG_KW:2 * HG_KW + HG_VW].reshape(B, L, HG_HEADS, HG_DV)
    og = zf[..., 2 * HG_KW + HG_VW:].reshape(B, L, HG_HEADS, HG_DV)
    lb = lb.reshape(HG_HEADS, HG_DK)
    f = lb + (1.0 - lb) * jax.nn.sigmoid(f_pre)
    out, S = gla_chunked(q, 1.0 - f, i, jnp.log(f), S0.astype(jnp.float32), min(CHUNK, L))
    out = head_rmsnorm(out, p['hg_norm_w'][o]) * jax.nn.sigmoid(og)
    return out.reshape(B, L, HG_VW), S.astype(S0.dtype)


def trunk(x, c, st, p):
    rw_S, rw_sh, s5_re, s5_im, gdn_S, gdn_cv, hg_S = st
    B = x.shape[0]
    gam = jnp.cumsum(jax.nn.softmax(p['hg_lb_logits'].astype(jnp.float32), axis=0), axis=0)
    lower_bounds = gam - gam[0]
    n_rw, n_sh, n_re, n_im, n_gdn, n_cv, n_hg = [], [], [], [], [], [], []
    for l in range(DEPTH):
        mod = (jax.nn.silu(c) @ p['w_mod'][l] + p['b_mod'][l]).reshape(B, 3, 3, D_MODEL)
        h = adaln(rmsnorm(x, p['norm_g'][l, 0]), mod[:, 0])
        x = x + 0.5 * mod[:, 0, 2][:, None] * swiglu(h, p['w_ffn_in'][l, 0], p['w_ffn_out'][l, 0])
        h = adaln(rmsnorm(x, p['norm_g'][l, 1]), mod[:, 1])
        if l % 2 == 0:
            e = l // 2
            zin = h @ p['w_in_even'][e]
            oa, sh, S = rwkv7_mixer(zin[..., :RW_IN], rw_sh[e], rw_S[e], p, e)
            ob, hr, hi = s5_mixer(zin[..., RW_IN:], s5_re[e], s5_im[e], p, e)
            n_rw.append(S)
            n_sh.append(sh)
            n_re.append(hr)
            n_im.append(hi)
            mix = jnp.concatenate([oa, ob], axis=-1).astype(x.dtype) @ p['w_out_even'][e]
        else:
            o = l // 2
            zin = h @ p['w_in_odd'][o]
            oc, cv, S = gdn_mixer(zin[..., :GDN_IN], gdn_cv[o], gdn_S[o], p, o)
            od, Sh = hgrn2_mixer(zin[..., GDN_IN:], hg_S[o], p, o, lower_bounds[l])
            n_gdn.append(S)
            n_cv.append(cv)
            n_hg.append(Sh)
            mix = jnp.concatenate([oc, od], axis=-1).astype(x.dtype) @ p['w_out_odd'][o]
        x = x + mod[:, 1, 2][:, None] * mix
        h = adaln(rmsnorm(x, p['norm_g'][l, 2]), mod[:, 2])
        x = x + 0.5 * mod[:, 2, 2][:, None] * swiglu(h, p['w_ffn_in'][l, 1], p['w_ffn_out'][l, 1])
    y = rmsnorm(x, p['norm_final'])
    return y, (jnp.stack(n_rw), jnp.stack(n_sh), jnp.stack(n_re), jnp.stack(n_im),
               jnp.stack(n_gdn), jnp.stack(n_cv), jnp.stack(n_hg))


def setup_inputs(seed: int = 0) -> dict:
    key = jax.random.key(seed)
    ks = iter(jax.random.split(key, 64))
    f32 = jnp.float32

    def nrm(shape, scale=1.0):
        return scale * jax.random.normal(next(ks), shape, f32)

    def uni(shape, lo, hi):
        return jax.random.uniform(next(ks), shape, f32, lo, hi)

    inp = {}
    inp['x_prompt'] = nrm((BATCH, SEQ, D_MODEL))
    inp['x_sample'] = nrm((DEC_BATCH, DEC_SEQ, D_MODEL))
    inp['c_prompt'] = nrm((BATCH, D_MODEL))
    inp['c_sample'] = nrm((DEC_BATCH, D_MODEL))
    inp['state_rwkv'] = nrm((N_EVEN, DEC_BATCH, RW_HEADS, RW_HEAD_DIM, RW_HEAD_DIM), 0.1)
    inp['state_rwkv_shift'] = nrm((N_EVEN, DEC_BATCH, RW_IN))
    inp['state_s5_re'] = nrm((N_EVEN, DEC_BATCH, S5_GROUPS, S5_STATE), 0.05)
    inp['state_s5_im'] = nrm((N_EVEN, DEC_BATCH, S5_GROUPS, S5_STATE), 0.05)
    inp['state_gdn'] = nrm((N_ODD, DEC_BATCH, GDN_HEADS, GDN_DK, GDN_DV), 0.1)
    inp['cache_gdn_conv'] = nrm((N_ODD, DEC_BATCH, GDN_CONV - 1, GDN_CONV_CH))
    inp['state_hgrn'] = nrm((N_ODD, DEC_BATCH, HG_HEADS, HG_DK, HG_DV), 0.5)
    inp['w_mod'] = nrm((DEPTH, D_MODEL, 9 * D_MODEL), 0.5 * D_MODEL ** -0.5)
    inp['b_mod'] = nrm((DEPTH, 9 * D_MODEL), 0.01)
    inp['norm_g'] = 1.0 + nrm((DEPTH, 3, D_MODEL), 0.01)
    inp['norm_final'] = 1.0 + nrm((D_MODEL,), 0.01)
    inp['w_ffn_in'] = nrm((DEPTH, 2, D_MODEL, 2 * D_FF), D_MODEL ** -0.5)
    inp['w_ffn_out'] = nrm((DEPTH, 2, D_FF, D_MODEL), D_FF ** -0.5)
    inp['w_in_even'] = nrm((N_EVEN, D_MODEL, EVEN_IN), D_MODEL ** -0.5)
    inp['w_out_even'] = nrm((N_EVEN, EVEN_OUT, D_MODEL), EVEN_OUT ** -0.5)
    inp['rw_mu'] = uni((N_EVEN, RW_IN), 0.0, 1.0)
    inp['rw_w0'] = uni((N_EVEN, RW_WIDTH), -4.0, 0.0)
    inp['rw_w_up'] = nrm((N_EVEN, RW_W_LORA, RW_WIDTH), 0.5 * RW_W_LORA ** -0.5)
    inp['rw_a0'] = nrm((N_EVEN, RW_WIDTH), 0.5)
    inp['rw_a_up'] = nrm((N_EVEN, RW_A_LORA, RW_WIDTH), RW_A_LORA ** -0.5)
    inp['rw_g_up'] = nrm((N_EVEN, RW_G_LORA, RW_WIDTH), RW_G_LORA ** -0.5)
    inp['rw_k_k'] = 0.85 + nrm((N_EVEN, RW_WIDTH), 0.1)
    inp['rw_k_a'] = 1.0 + nrm((N_EVEN, RW_WIDTH), 0.05)
    inp['rw_r_k'] = nrm((N_EVEN, RW_HEADS, RW_HEAD_DIM), 0.1)
    inp['rw_ln_w'] = 1.0 + nrm((N_EVEN, RW_WIDTH), 0.01)
    inp['rw_ln_b'] = nrm((N_EVEN, RW_WIDTH), 0.01)
    inp['s5_lambda_re'] = -0.5 + nrm((N_EVEN, S5_GROUPS, S5_STATE), 0.01)
    inp['s5_lambda_im'] = math.pi * jnp.arange(S5_STATE, dtype=f32) + nrm((N_EVEN, S5_GROUPS, S5_STATE), 0.01)
    inp['s5_log_dt'] = uni((N_EVEN, S5_GROUPS), math.log(1e-3), math.log(1e-1))
    inp['s5_b_re'] = nrm((N_EVEN, S5_GROUPS, S5_STATE, S5_GROUP), (2 * S5_GROUP) ** -0.5)
    inp['s5_b_im'] = nrm((N_EVEN, S5_GROUPS, S5_STATE, S5_GROUP), (2 * S5_GROUP) ** -0.5)
    inp['s5_c_re'] = nrm((N_EVEN, S5_GROUPS, S5_GROUP, S5_STATE), (2 * S5_STATE) ** -0.5)
    inp['s5_c_im'] = nrm((N_EVEN, S5_GROUPS, S5_GROUP, S5_STATE), (2 * S5_STATE) ** -0.5)
    inp['s5_d'] = nrm((N_EVEN, S5_GROUPS, S5_GROUP))
    inp['s5_glu_w'] = nrm((N_EVEN, S5_GROUPS, S5_GROUP, S5_GROUP), S5_GROUP ** -0.5)
    inp['s5_glu_b'] = nrm((N_EVEN, S5_GROUPS, S5_GROUP), 0.01)
    inp['w_in_odd'] = nrm((N_ODD, D_MODEL, ODD_IN), D_MODEL ** -0.5)
    inp['w_out_odd'] = nrm((N_ODD, ODD_OUT, D_MODEL), ODD_OUT ** -0.5)
    inp['gdn_conv_w'] = nrm((N_ODD, GDN_CONV, GDN_CONV_CH), 0.5)
    inp['gdn_a_log'] = jnp.log(uni((N_ODD, GDN_HEADS), 1.0, 16.0))
    dt = jnp.exp(uni((N_ODD, GDN_HEADS), math.log(1e-3), math.log(1e-1)))
    inp['gdn_dt_bias'] = dt + jnp.log(-jnp.expm1(-dt))
    inp['gdn_norm_w'] = 1.0 + nrm((N_ODD, GDN_DV), 0.01)
    inp['hg_lb_logits'] = nrm((DEPTH, HG_KW))
    inp['hg_norm_w'] = 1.0 + nrm((N_ODD, HG_DV), 0.01)
    return inp


def reference(x_prompt, x_sample, c_prompt, c_sample,
              state_rwkv, state_rwkv_shift, state_s5_re, state_s5_im, state_gdn, cache_gdn_conv, state_hgrn,
              w_mod, b_mod, norm_g, norm_final, w_ffn_in, w_ffn_out,
              w_in_even, w_out_even,
              rw_mu, rw_w0, rw_w_up, rw_a0, rw_a_up, rw_g_up, rw_k_k, rw_k_a, rw_r_k, rw_ln_w, rw_ln_b,
              s5_lambda_re, s5_lambda_im, s5_log_dt, s5_b_re, s5_b_im, s5_c_re, s5_c_im, s5_d, s5_glu_w, s5_glu_b,
              w_in_odd, w_out_odd, gdn_conv_w, gdn_a_log, gdn_dt_bias, gdn_norm_w, hg_lb_logits, hg_norm_w):
    p = dict(w_mod=w_mod, b_mod=b_mod, norm_g=norm_g, norm_final=norm_final,
             w_ffn_in=w_ffn_in, w_ffn_out=w_ffn_out, w_in_even=w_in_even, w_out_even=w_out_even,
             rw_mu=rw_mu, rw_w0=rw_w0, rw_w_up=rw_w_up, rw_a0=rw_a0, rw_a_up=rw_a_up, rw_g_up=rw_g_up,
             rw_k_k=rw_k_k, rw_k_a=rw_k_a, rw_r_k=rw_r_k, rw_ln_w=rw_ln_w, rw_ln_b=rw_ln_b,
             s5_lambda_re=s5_lambda_re, s5_lambda_im=s5_lambda_im, s5_log_dt=s5_log_dt,
             s5_b_re=s5_b_re, s5_b_im=s5_b_im, s5_c_re=s5_c_re, s5_c_im=s5_c_im, s5_d=s5_d,
             s5_glu_w=s5_glu_w, s5_glu_b=s5_glu_b, w_in_odd=w_in_odd, w_out_odd=w_out_odd,
             gdn_conv_w=gdn_conv_w, gdn_a_log=gdn_a_log, gdn_dt_bias=gdn_dt_bias, gdn_norm_w=gdn_norm_w,
             hg_lb_logits=hg_lb_logits, hg_norm_w=hg_norm_w)
    Bp = x_prompt.shape[0]
    dtp = x_prompt.dtype
    st_prompt = (jnp.zeros((N_EVEN, Bp, RW_HEADS, RW_HEAD_DIM, RW_HEAD_DIM), dtp),
                 jnp.zeros((N_EVEN, Bp, RW_IN), dtp),
                 jnp.zeros((N_EVEN, Bp, S5_GROUPS, S5_STATE), dtp),
                 jnp.zeros((N_EVEN, Bp, S5_GROUPS, S5_STATE), dtp),
                 jnp.zeros((N_ODD, Bp, GDN_HEADS, GDN_DK, GDN_DV), dtp),
                 jnp.zeros((N_ODD, Bp, GDN_CONV - 1, GDN_CONV_CH), dtp),
                 jnp.zeros((N_ODD, Bp, HG_HEADS, HG_DK, HG_DV), dtp))
    st_sample = (state_rwkv, state_rwkv_shift, state_s5_re, state_s5_im, state_gdn, cache_gdn_conv, state_hgrn)
    y_prompt, ns_p = trunk(x_prompt, c_prompt, st_prompt, p)
    y_sample, ns_s = trunk(x_sample, c_sample, st_sample, p)
    p_rw, p_sh, p_re, p_im, p_gdn, p_cv, p_hg = ns_p
    s_rw, s_sh, s_re, s_im, s_gdn, s_cv, s_hg = ns_s
    return (y_prompt, y_sample, p_rw, p_sh, p_re, p_im, p_gdn, p_cv, p_hg,
            s_rw, s_sh, s_re, s_im, s_gdn, s_cv, s_hg)
```

```python
import functools

import jax
import jax.numpy as jnp
from jax import lax
from jax.experimental import pallas as pl
from jax.experimental.pallas import tpu as pltpu

F32 = jnp.float32
BF16 = jnp.bfloat16
HI = lax.Precision.HIGHEST

EPS = 1e-6
RW_GN_EPS = 64e-5
L2_EPS = 1e-12
CHUNK = 64
SUB = 16
LANES = 128
SUBLANES = 8
VMEM_LIMIT = 56 * 1024 * 1024


def _params(sem, big=False):
    return pltpu.CompilerParams(dimension_semantics=sem,
                                vmem_limit_bytes=VMEM_LIMIT if big else None)


def _pick_tile(n, pref, mult=LANES):
    t = (min(pref, n) // mult) * mult
    while t >= mult:
        if n % t == 0:
            return t
        t -= mult
    return n


def _mm(a, b, hi=False):
    if not hi:
        a, b = a.astype(BF16), b.astype(BF16)
    return lax.dot_general(a, b, (((1,), (0,)), ((), ())), precision=HI if hi else None,
                           preferred_element_type=F32)


def _mm_nt(a, b, hi=False):
    if not hi:
        a, b = a.astype(BF16), b.astype(BF16)
    return lax.dot_general(a, b, (((1,), (1,)), ((), ())), precision=HI if hi else None,
                           preferred_element_type=F32)


def _mm_tn(a, b, hi=False):
    if not hi:
        a, b = a.astype(BF16), b.astype(BF16)
    return lax.dot_general(a, b, (((0,), (0,)), ((), ())), precision=HI if hi else None,
                           preferred_element_type=F32)


def _sigmoid(x):
    return 1.0 / (1.0 + jnp.exp(-x))


def _silu(x):
    return x * _sigmoid(x)


def _softplus(x):
    return jnp.maximum(x, 0.0) + jnp.log(1.0 + jnp.exp(-jnp.abs(x)))


def _iota2(shape, axis):
    return lax.broadcasted_iota(jnp.int32, shape, axis)


def _mod_kernel(c_ref, w_ref, b_ref, o_ref):
    s = _silu(c_ref[...]).astype(BF16)
    o_ref[0] = jnp.dot(s, w_ref[0].astype(BF16), preferred_element_type=F32) + b_ref[0]


def _modulation(c, w_mod, b_mod):
    depth, d, n = w_mod.shape
    bc = c.shape[0]
    tn = _pick_tile(n, 512)
    return pl.pallas_call(
        _mod_kernel,
        grid=(depth, n // tn),
        in_specs=[pl.BlockSpec((bc, d), lambda l, j: (0, 0)),
                  pl.BlockSpec((1, d, tn), lambda l, j: (l, 0, j)),
                  pl.BlockSpec((1, 1, tn), lambda l, j: (l, 0, j))],
        out_specs=pl.BlockSpec((1, bc, tn), lambda l, j: (l, 0, j)),
        out_shape=jax.ShapeDtypeStruct((depth, bc, n), F32),
        compiler_params=_params(("arbitrary", "arbitrary"), big=True),
        name="modulation",
    )(c, w_mod, b_mod.reshape(depth, 1, n))


def _norm_kernel(x_ref, g_ref, *rest, idx):
    if idx is None:
        (o_ref,) = rest
    else:
        m_ref, o_ref = rest
    x = x_ref[0]
    y = x * lax.rsqrt(jnp.mean(x * x, axis=-1, keepdims=True) + EPS) * g_ref[...]
    if idx is not None:
        shift = m_ref[0, pl.ds(3 * idx, 1), :]
        scale = m_ref[0, pl.ds(3 * idx + 1, 1), :]
        y = y * (1.0 + scale) + shift
    o_ref[0] = y.astype(o_ref.dtype)


def _norm(x, g, mod9=None, idx=None, out_dtype=BF16):
    b, l, d = x.shape
    tl = _pick_tile(l, 256, SUBLANES)
    in_specs = [pl.BlockSpec((1, tl, d), lambda i, t: (i, t, 0)),
                pl.BlockSpec((1, d), lambda i, t: (0, 0))]
    args = [x, g.reshape(1, d)]
    if idx is not None:
        in_specs.append(pl.BlockSpec((1, mod9.shape[1], d), lambda i, t: (i, 0, 0)))
        args.append(mod9)
    return pl.pallas_call(
        functools.partial(_norm_kernel, idx=idx),
        grid=(b, l // tl),
        in_specs=in_specs,
        out_specs=pl.BlockSpec((1, tl, d), lambda i, t: (i, t, 0)),
        out_shape=jax.ShapeDtypeStruct((b, l, d), out_dtype),
        compiler_params=_params(("arbitrary", "arbitrary")),
        name="norm",
    )(*args)


def _mm_kernel(*refs, n_w, epi, coef):
    x_ref = refs[0]
    w_refs = refs[1:1 + n_w]
    pos = 1 + n_w
    if epi == "res":
        res_ref, gate_ref = refs[pos], refs[pos + 1]
        pos += 2
    o_ref = refs[pos]
    wb_refs = refs[pos + 1:pos + 1 + n_w]

    @pl.when(pl.program_id(1) == 0)
    def _():
        for w_ref, wb_ref in zip(w_refs, wb_refs):
            wb_ref[...] = w_ref[...].astype(BF16)

    x = x_ref[...]
    acc = [jnp.dot(x, wb_ref[...], preferred_element_type=F32) for wb_ref in wb_refs]
    if epi == "swiglu":
        out = _silu(acc[0]) * acc[1]
    elif epi == "res":
        out = res_ref[...] + coef * gate_ref[0] * acc[0]
    else:
        out = acc[0]
    o_ref[...] = out.astype(o_ref.dtype)


def _matmul(x, w, prefix, col_blocks, tn, *, rows_per_gate=None, res=None, gate=None,
            coef=1.0, epi="plain", out_dtype=F32, tm_pref=512, w_buffers=2):
    m, k = x.shape
    offs, n_cols = col_blocks
    n_w = len(offs)
    n_out = n_cols * tn
    if rows_per_gate is not None:
        tm = _pick_tile(rows_per_gate, tm_pref, SUBLANES)
    else:
        tm = _pick_tile(m, tm_pref, SUBLANES)
    npre = len(prefix)
    wmode = {} if w_buffers == 2 else {"pipeline_mode": pl.Buffered(w_buffers)}
    in_specs = [pl.BlockSpec((tm, k), lambda j, i: (i, 0))]
    args = [x]
    for off in offs:
        in_specs.append(pl.BlockSpec((None,) * npre + (k, tn),
                                     lambda j, i, off=off: tuple(prefix) + (0, j + off), **wmode))
        args.append(w)
    if epi == "res":
        in_specs.append(pl.BlockSpec((tm, tn), lambda j, i: (i, j)))
        args.append(res)
        if rows_per_gate is not None:
            rpt = rows_per_gate // tm
            in_specs.append(pl.BlockSpec((1, 1, tn), lambda j, i: (i // rpt, 0, j)))
        else:
            in_specs.append(pl.BlockSpec((1, tm, tn), lambda j, i: (0, i, j)))
        args.append(gate)
    return pl.pallas_call(
        functools.partial(_mm_kernel, n_w=n_w, epi=epi, coef=coef),
        grid=(n_cols, m // tm),
        in_specs=in_specs,
        out_specs=pl.BlockSpec((tm, tn), lambda j, i: (i, j)),
        out_shape=jax.ShapeDtypeStruct((m, n_out), out_dtype),
        scratch_shapes=[pltpu.VMEM((k, tn), BF16) for _ in range(n_w)],
        compiler_params=_params(("arbitrary", "arbitrary"), big=True),
        name="matmul_" + epi,
    )(*args)


def _group_ones(n):
    r = _iota2((LANES, LANES), 0)
    c = _iota2((LANES, LANES), 1)
    sh = n.bit_length() - 1
    return (jnp.right_shift(r, sh) == jnp.right_shift(c, sh)).astype(F32)


def _group_sum(x, n):
    if n % LANES == 0:
        parts = []
        for h in range(x.shape[1] // n):
            s = jnp.sum(x[:, h * n:(h + 1) * n], axis=-1, keepdims=True)
            parts.append(jnp.broadcast_to(s, (x.shape[0], n)))
        return parts[0] if len(parts) == 1 else jnp.concatenate(parts, axis=1)
    ones = _group_ones(n)
    parts = [_mm(x[:, s * LANES:(s + 1) * LANES], ones, hi=True) for s in range(x.shape[1] // LANES)]
    return parts[0] if len(parts) == 1 else jnp.concatenate(parts, axis=1)


def _tril(c, strict=False):
    r = _iota2((c, c), 0)
    s = _iota2((c, c), 1)
    return (r > s) if strict else (r >= s)


def _unit_lower_inv(lm):
    c = lm.shape[0]
    r = _iota2((c, c), 0)
    s = _iota2((c, c), 1)
    eye = (r == s).astype(F32)
    if c <= SUB:
        ld, lo = lm, None
    else:
        sh = SUB.bit_length() - 1
        same = jnp.right_shift(r, sh) == jnp.right_shift(s, sh)
        ld = jnp.where(same, lm, 0.0)
        lo = lm - ld
    neg = -ld
    p = eye + neg
    pw = neg
    k = 1
    while 2 * k < min(SUB, c):
        pw = _mm(pw, pw, hi=True)
        p = p + _mm(p, pw, hi=True)
        k *= 2
    if lo is None:
        return p
    dinv = p
    neg = -_mm(dinv, lo, hi=True)
    x = eye + neg
    pw = neg
    k = 1
    while 2 * k < c // SUB:
        pw = _mm(pw, pw, hi=True)
        x = x + _mm(x, pw, hi=True)
        k *= 2
    return _mm(x, dinv, hi=True)


def _head_rmsnorm(o, w):
    return o * lax.rsqrt(jnp.mean(o * o, axis=-1, keepdims=True) + EPS) * w


def _rw_pre_kernel(z_ref, sh_ref, mu_ref, w0_ref, a0_ref, kk_ref, ka_ref, rk_ref,
                   wup_ref, aup_ref, gup_ref,
                   r_o, d_o, k_o, v_o, kk_o, nkka_o, g_o, bonus_o, prev_scr,
                   *, width, w_lora, a_lora, head_dim):
    @pl.when(pl.program_id(1) == 0)
    def _():
        prev_scr[...] = sh_ref[0]

    z = z_ref[0]
    tl = z.shape[0]
    rolled = pltpu.roll(z, 1, 0)
    prev = jnp.where(_iota2(z.shape, 0) == 0, prev_scr[...], rolled)
    prev_scr[...] = z_ref[0, pl.ds(tl - 1, 1), :]
    zs = z + mu_ref[...] * (prev - z)
    c1, c2, c3 = width, 2 * width, 3 * width
    c4 = c3 + w_lora
    c5 = c4 + a_lora
    r, k, v = zs[:, :c1], zs[:, c1:c2], zs[:, c2:c3]
    xw, xa, xg = zs[:, c3:c4], zs[:, c4:c5], zs[:, c5:]
    w = -_softplus(-(w0_ref[...] + _mm(jnp.tanh(xw), wup_ref[...]))) - 0.5
    decay = jnp.exp(-jnp.exp(w))
    a = _sigmoid(a0_ref[...] + _mm(xa, aup_ref[...]))
    g = _mm(_sigmoid(xg), gup_ref[...])
    kraw = k * kk_ref[...]
    kk = kraw * lax.rsqrt(_group_sum(kraw * kraw, head_dim) + L2_EPS)
    k2 = k * (1.0 + (a - 1.0) * ka_ref[...])
    bonus = _group_sum(r * k2 * rk_ref[...], head_dim) * v
    r_o[0] = r
    d_o[0] = decay
    k_o[0] = k2
    v_o[0] = v
    kk_o[0] = kk
    nkka_o[0] = -(kk * a)
    g_o[0] = g
    bonus_o[0] = bonus


def _rw_pre(zin, rw_in, shift, p, e, head_dim):
    b, l, _ = zin.shape
    width = p["rw_w0"].shape[1]
    w_lora = p["rw_w_up"].shape[1]
    a_lora = p["rw_a_up"].shape[1]
    g_lora = p["rw_g_up"].shape[1]
    tl = _pick_tile(l, 128, SUBLANES)
    row = lambda n: pl.BlockSpec((1, n), lambda i, t: (0, 0))
    out_spec = pl.BlockSpec((1, tl, width), lambda i, t: (i, t, 0))
    out_shape = jax.ShapeDtypeStruct((b, l, width), F32)
    return pl.pallas_call(
        functools.partial(_rw_pre_kernel, width=width, w_lora=w_lora, a_lora=a_lora, head_dim=head_dim),
        grid=(b, l // tl),
        in_specs=[pl.BlockSpec((1, tl, rw_in), lambda i, t: (i, t, 0)),
                  pl.BlockSpec((1, 1, rw_in), lambda i, t: (i, 0, 0)),
                  row(rw_in), row(width), row(width), row(width), row(width), row(width),
                  pl.BlockSpec((None, w_lora, width), lambda i, t: (e, 0, 0)),
                  pl.BlockSpec((None, a_lora, width), lambda i, t: (e, 0, 0)),
                  pl.BlockSpec((None, g_lora, width), lambda i, t: (e, 0, 0))],
        out_specs=[out_spec] * 8,
        out_shape=[out_shape] * 8,
        scratch_shapes=[pltpu.VMEM((1, rw_in), F32)],
        compiler_params=_params(("arbitrary", "arbitrary"), big=True),
        name="rwkv_pre",
    )(zin, shift.reshape(b, 1, rw_in), p["rw_mu"][e:e + 1], p["rw_w0"][e:e + 1], p["rw_a0"][e:e + 1],
      p["rw_k_k"][e:e + 1], p["rw_k_a"][e:e + 1], p["rw_r_k"][e].reshape(1, width),
      p["rw_w_up"], p["rw_a_up"], p["rw_g_up"])


def _rw_scan_kernel(r_ref, d_ref, k_ref, v_ref, kk_ref, nkka_ref, s0_ref, o_ref, s_ref, *, steps, n):
    @pl.when(pl.program_id(1) == 0)
    def _():
        s_ref[...] = s0_ref[...]

    def step(t, carry):
        sa = s_ref[0] * kk_ref[t, pl.ds(0, 1), :]
        for j in range(1, n):
            sa = sa + s_ref[j] * kk_ref[t, pl.ds(j, 1), :]
        v_t = v_ref[t]
        o = None
        for j in range(n):
            sj = (s_ref[j] * d_ref[t, pl.ds(j, 1), :] + sa * nkka_ref[t, pl.ds(j, 1), :]
                  + v_t * k_ref[t, pl.ds(j, 1), :])
            s_ref[j] = sj
            term = sj * r_ref[t, pl.ds(j, 1), :]
            o = term if o is None else o + term
        o_ref[t] = o
        return carry

    lax.fori_loop(0, steps, step, 0)


def _rw_scan(seqs, s0):
    l, n, bh = seqs[0].shape
    lb = LANES if bh % LANES == 0 else bh
    steps = _pick_tile(l, 32, 1)
    seq_spec = pl.BlockSpec((steps, n, lb), lambda c, t: (t, 0, c))
    st_spec = pl.BlockSpec((n, n, lb), lambda c, t: (0, 0, c))
    return pl.pallas_call(
        functools.partial(_rw_scan_kernel, steps=steps, n=n),
        grid=(bh // lb, l // steps),
        in_specs=[seq_spec] * 6 + [st_spec],
        out_specs=[seq_spec, st_spec],
        out_shape=[jax.ShapeDtypeStruct((l, n, bh), F32), jax.ShapeDtypeStruct((n, n, bh), F32)],
        compiler_params=_params(("arbitrary", "arbitrary"), big=True),
        name="rwkv_scan",
    )(*seqs, s0)


def _rw_post_kernel(o_ref, bonus_ref, g_ref, lnw_ref, lnb_ref, out_ref, *, head_dim):
    o = o_ref[0]
    mu = _group_sum(o, head_dim) * (1.0 / head_dim)
    c = o - mu
    var = _group_sum(c * c, head_dim) * (1.0 / head_dim)
    y = c * lax.rsqrt(var + RW_GN_EPS) * lnw_ref[...] + lnb_ref[...]
    out_ref[0] = ((y + bonus_ref[0]) * g_ref[0]).astype(out_ref.dtype)


def _rw_post(o, bonus, g, p, e, head_dim):
    b, l, width = o.shape
    tl = _pick_tile(l, 256, SUBLANES)
    spec = pl.BlockSpec((1, tl, width), lambda i, t: (i, t, 0))
    row = pl.BlockSpec((1, width), lambda i, t: (0, 0))
    return pl.pallas_call(
        functools.partial(_rw_post_kernel, head_dim=head_dim),
        grid=(b, l // tl),
        in_specs=[spec, spec, spec, row, row],
        out_specs=spec,
        out_shape=jax.ShapeDtypeStruct((b, l, width), BF16),
        compiler_params=_params(("arbitrary", "arbitrary")),
        name="rwkv_post",
    )(o, bonus, g, p["rw_ln_w"][e:e + 1], p["rw_ln_b"][e:e + 1])


def _rwkv7(zin, rw_in, shift, s0, p, e):
    b, l, _ = zin.shape
    heads, n = p["rw_r_k"].shape[1], p["rw_r_k"].shape[2]
    r, d, k, v, kk, nkka, g, bonus = _rw_pre(zin, rw_in, shift, p, e, n)

    def to_lanes(t):
        return t.reshape(b, l, heads, n).transpose(1, 3, 0, 2).reshape(l, n, b * heads)

    s0_l = s0.astype(F32).transpose(3, 2, 0, 1).reshape(n, n, b * heads)
    o_l, s_l = _rw_scan([to_lanes(t) for t in (r, d, k, v, kk, nkka)], s0_l)
    o = o_l.reshape(l, n, b, heads).transpose(2, 0, 3, 1).reshape(b, l, heads * n)
    s_fin = s_l.reshape(n, n, b, heads).transpose(2, 3, 1, 0)
    out = _rw_post(o, bonus, g, p, e, n)
    return out, zin[:, -1, :rw_in].astype(shift.dtype), s_fin.astype(s0.dtype)


def _s5_param_kernel(ldt_ref, lr_ref, li_ref, bre_ref, bim_ref, are_o, aim_o, bbre_o, bbim_o):
    dt = jnp.exp(ldt_ref[...])
    lr, li = lr_ref[...], li_ref[...]
    mag = jnp.exp(lr * dt)
    ab_re, ab_im = mag * jnp.cos(li * dt), mag * jnp.sin(li * dt)
    den = lr * lr + li * li
    pr, pi_ = ab_re - 1.0, ab_im
    coef_re = (pr * lr + pi_ * li) / den
    coef_im = (pi_ * lr - pr * li) / den
    are_o[...] = ab_re
    aim_o[...] = ab_im
    for m in range(bre_ref.shape[0]):
        b_re, b_im = bre_ref[m], bim_ref[m]
        bbre_o[m] = coef_re * b_re - coef_im * b_im
        bbim_o[m] = coef_re * b_im + coef_im * b_re


def _s5_params(p, e):
    g, n, m = p["s5_b_re"].shape[1:]
    outs = pl.pallas_call(
        _s5_param_kernel,
        out_shape=[jax.ShapeDtypeStruct((g, n), F32)] * 2 + [jax.ShapeDtypeStruct((m, g, n), F32)] * 2,
        name="s5_params",
    )(p["s5_log_dt"][e].reshape(g, 1), p["s5_lambda_re"][e], p["s5_lambda_im"][e],
      p["s5_b_re"][e].transpose(2, 0, 1), p["s5_b_im"][e].transpose(2, 0, 1))
    return outs


def _block_diag(t, gs):
    g, a, b = t.shape
    t = t.reshape(g // gs, gs, a, b)
    eye = jnp.eye(gs, dtype=t.dtype)
    return jnp.einsum("sgab,gh->sgahb", t, eye).reshape(g // gs, gs * a, gs * b)


def _s5_kernel(u_ref, bre_ref, bim_ref, cre_ref, cim_ref, d_ref, gw_ref, gb_ref, are_ref, aim_ref,
               h0r_ref, h0i_ref, o_ref, hr_ref, hi_ref, xr_scr, xi_scr, *, steps):
    @pl.when(pl.program_id(1) == 0)
    def _():
        hr_ref[...] = h0r_ref[...]
        hi_ref[...] = h0i_ref[...]

    bp, gc = u_ref.shape[1], u_ref.shape[2]
    sn = xr_scr.shape[2]
    u2 = u_ref[...].reshape(steps * bp, gc)
    xr_scr[...] = _mm(u2, bre_ref[0]).reshape(steps, bp, sn)
    xi_scr[...] = _mm(u2, bim_ref[0]).reshape(steps, bp, sn)
    a_re, a_im = are_ref[0], aim_ref[0]

    def step(t, carry):
        h_re, h_im = carry
        n_re = a_re * h_re - a_im * h_im + xr_scr[t]
        n_im = a_re * h_im + a_im * h_re + xi_scr[t]
        xr_scr[t] = n_re
        xi_scr[t] = n_im
        return n_re, n_im

    h_re, h_im = lax.fori_loop(0, steps, step, (hr_ref[...], hi_ref[...]))
    hr_ref[...] = h_re
    hi_ref[...] = h_im
    hs_re = xr_scr[...].reshape(steps * bp, sn)
    hs_im = xi_scr[...].reshape(steps * bp, sn)
    y = _mm(hs_re, cre_ref[0]) - _mm(hs_im, cim_ref[0]) + d_ref[0] * u2
    yg = 0.5 * y * (1.0 + jnp.tanh(0.7978845608028654 * (y + 0.044715 * (y * y * y))))
    out = yg * _sigmoid(_mm(yg, gw_ref[0]) + gb_ref[0])
    o_ref[...] = out.reshape(steps, bp, gc)


def _s5(u, h_re, h_im, p, e):
    b, l, width = u.shape
    g, n, m = p["s5_b_re"].shape[1:]
    gs = LANES // m
    ns = g // gs
    sn = gs * n
    ab_re, ab_im, bb_re, bb_im = _s5_params(p, e)
    bd_bre = _block_diag(bb_re.transpose(1, 0, 2), gs)
    bd_bim = _block_diag(bb_im.transpose(1, 0, 2), gs)
    bd_cre = _block_diag(p["s5_c_re"][e].transpose(0, 2, 1), gs)
    bd_cim = _block_diag(p["s5_c_im"][e].transpose(0, 2, 1), gs)
    bd_gw = _block_diag(p["s5_glu_w"][e], gs)
    bp = -(-b // SUBLANES) * SUBLANES
    pad_b = lambda t: jnp.pad(t, ((0, bp - b),) + ((0, 0),) * (t.ndim - 1))
    u_t = jnp.pad(u.transpose(1, 0, 2), ((0, 0), (0, bp - b), (0, 0)))
    steps = _pick_tile(l, 256, 1)
    slab = lambda r, c: pl.BlockSpec((1, r, c), lambda s, t: (s, 0, 0))
    st_spec = pl.BlockSpec((bp, sn), lambda s, t: (0, s))
    out, hr, hi = pl.pallas_call(
        functools.partial(_s5_kernel, steps=steps),
        grid=(ns, l // steps),
        in_specs=[pl.BlockSpec((steps, bp, LANES), lambda s, t: (t, 0, s)),
                  slab(LANES, sn), slab(LANES, sn), slab(sn, LANES), slab(sn, LANES),
                  slab(1, LANES), slab(LANES, LANES), slab(1, LANES), slab(1, sn), slab(1, sn),
                  st_spec, st_spec],
        out_specs=[pl.BlockSpec((steps, bp, LANES), lambda s, t: (t, 0, s)), st_spec, st_spec],
        out_shape=[jax.ShapeDtypeStruct((l, bp, width), F32),
                   jax.ShapeDtypeStruct((bp, g * n), F32), jax.ShapeDtypeStruct((bp, g * n), F32)],
        scratch_shapes=[pltpu.VMEM((steps, bp, sn), F32), pltpu.VMEM((steps, bp, sn), F32)],
        compiler_params=_params(("arbitrary", "arbitrary"), big=True),
        name="s5",
    )(u_t, bd_bre, bd_bim, bd_cre, bd_cim, p["s5_d"][e].reshape(ns, 1, LANES), bd_gw,
      p["s5_glu_b"][e].reshape(ns, 1, LANES), ab_re.reshape(ns, 1, sn), ab_im.reshape(ns, 1, sn),
      pad_b(h_re.astype(F32).reshape(b, g * n)), pad_b(h_im.astype(F32).reshape(b, g * n)))
    out = out[:, :b].transpose(1, 0, 2)
    return (out, hr[:b].reshape(b, g, n).astype(h_re.dtype), hi[:b].reshape(b, g, n).astype(h_im.dtype))


def _gdn_gate_kernel(z_ref, alog_ref, dtb_ref, gc_o, beta_o):
    z = z_ref[0]
    c = z.shape[0]
    g = -jnp.exp(alog_ref[...]) * _softplus(z + dtb_ref[...])
    gc_o[0] = _mm(_tril(c).astype(F32), g, hi=True)
    beta_o[0] = _sigmoid(z)


def _gdn_gates(zab, a_log, dt_bias, chunk):
    b, l, w = zab.shape
    h = a_log.shape[0]
    pad = lambda t: jnp.pad(t, (0, w - h)).reshape(1, w)
    spec = pl.BlockSpec((1, chunk, w), lambda i, t: (i, t, 0))
    row = pl.BlockSpec((1, w), lambda i, t: (0, 0))
    return pl.pallas_call(
        _gdn_gate_kernel,
        grid=(b, l // chunk),
        in_specs=[spec, row, row],
        out_specs=[spec, spec],
        out_shape=[jax.ShapeDtypeStruct((b, l, w), F32)] * 2,
        compiler_params=_params(("arbitrary", "arbitrary")),
        name="gdn_gates",
    )(zab, pad(a_log), pad(dt_bias))


def _gdn_kernel(q_ref, k_ref, v_ref, y_ref, wq_ref, wk_ref, wv_ref, bq_ref, bk_ref, bv_ref,
                gcol_ref, bcol_ref, grow_ref, nw_ref, s0_ref, o_ref, s_ref, cq_scr, ck_scr, cv_scr,
                *, hb, dk, dv, kconv):
    t = pl.program_id(2)

    @pl.when(t == 0)
    def _():
        s_ref[0] = s0_ref[0]
        cq_scr[...] = bq_ref[0]
        ck_scr[...] = bk_ref[0]
        cv_scr[...] = bv_ref[0]

    c = q_ref.shape[1]

    def conv(x_ref, w_ref, carry_scr):
        x = x_ref[0]
        full = jnp.concatenate([carry_scr[...], x], axis=0)
        carry_scr[...] = x_ref[0, pl.ds(c - SUBLANES, SUBLANES), :]
        acc = None
        for j in range(kconv):
            sh = kconv - 1 - j
            src = full if sh == 0 else pltpu.roll(full, sh, 0)
            term = src[SUBLANES:SUBLANES + c] * w_ref[pl.ds(j, 1), :]
            acc = term if acc is None else acc + term
        return _silu(acc)

    qa = conv(q_ref, wq_ref, cq_scr)
    ka = conv(k_ref, wk_ref, ck_scr)
    va = conv(v_ref, wv_ref, cv_scr)
    incl = _tril(c)
    strict = _tril(c, strict=True)
    outs = []
    for hl in range(hb):
        q = qa[:, hl * dk:(hl + 1) * dk]
        k = ka[:, hl * dk:(hl + 1) * dk]
        v = va[:, hl * dv:(hl + 1) * dv]
        q = q * lax.rsqrt(jnp.sum(q * q, axis=-1, keepdims=True) + L2_EPS) * (dk ** -0.5)
        k = k * lax.rsqrt(jnp.sum(k * k, axis=-1, keepdims=True) + L2_EPS)
        gc_c = gcol_ref[0, 0, :, hl:hl + 1]
        beta_c = bcol_ref[0, 0, :, hl:hl + 1]
        gc_r = grow_ref[0, 0, 0, hl:hl + 1, :]
        kb = k * beta_c
        diff = gc_c - gc_r
        dmask = jnp.where(incl, jnp.exp(jnp.where(incl, diff, 0.0)), 0.0)
        m = jnp.where(strict, _mm_nt(kb, k, hi=True) * dmask, 0.0)
        egc = jnp.exp(gc_c)
        rhs = jnp.concatenate([v * beta_c, kb * egc], axis=1)
        sol = _mm(_unit_lower_inv(m), rhs, hi=True)
        u, w = sol[:, :dv], sol[:, dv:]
        attn = _mm_nt(q, k, hi=True) * dmask
        qg = q * egc
        g_last = gc_r[:, c - 1:c]
        kdec = k * jnp.exp(g_last - gc_c)
        s = s_ref[0, hl]
        v_new = u - _mm(w, s, hi=True)
        o = _mm(qg, s, hi=True) + _mm(attn, v_new, hi=True)
        s_ref[0, hl] = s * jnp.exp(g_last) + _mm_tn(kdec, v_new, hi=True)
        y = y_ref[0, :, hl * dv:(hl + 1) * dv]
        outs.append(_head_rmsnorm(o, nw_ref[...]) * _silu(y))
    o_ref[0] = (outs[0] if hb == 1 else jnp.concatenate(outs, axis=1)).astype(o_ref.dtype)


def _gdn(zmain, zab, conv_buf, s0, p, o, chunk):
    b, l, _ = zmain.shape
    heads = p["gdn_a_log"].shape[1]
    dv = p["gdn_norm_w"].shape[1]
    kconv, conv_ch = p["gdn_conv_w"].shape[1:]
    vw = heads * dv
    kw = (conv_ch - vw) // 2
    dk = kw // heads
    hb = 2 if heads % 2 == 0 else 1
    nhb = heads // hb
    nc = l // chunk
    gc, beta = _gdn_gates(zab, p["gdn_a_log"][o], p["gdn_dt_bias"][o], chunk)
    gcol = gc[..., :heads].reshape(b, l, nhb, hb).transpose(0, 2, 1, 3)
    bcol = beta[..., heads:2 * heads].reshape(b, l, nhb, hb).transpose(0, 2, 1, 3)
    grow = gc[..., :heads].reshape(b, nc, chunk, nhb, hb).transpose(0, 3, 1, 4, 2)
    cbuf = jnp.pad(conv_buf.astype(F32), ((0, 0), (SUBLANES - (kconv - 1), 0), (0, 0)))
    wq, wk = hb * dk, hb * dk
    wv = hb * dv
    qoff, koff, voff, yoff = 0, kw // wk, 2 * kw // wv, (2 * kw + vw) // wv
    col = lambda w_, off: pl.BlockSpec((1, chunk, w_), lambda i, h, t: (i, t, off + h))
    cw = lambda w_, off: pl.BlockSpec((None, kconv, w_), lambda i, h, t: (o, 0, off + h))
    cb = lambda w_, off: pl.BlockSpec((1, SUBLANES, w_), lambda i, h, t: (i, 0, off + h))
    st = pl.BlockSpec((1, hb, dk, dv), lambda i, h, t: (i, h, 0, 0))
    out, s_fin = pl.pallas_call(
        functools.partial(_gdn_kernel, hb=hb, dk=dk, dv=dv, kconv=kconv),
        grid=(b, nhb, nc),
        in_specs=[col(wq, qoff), col(wk, koff), col(wv, voff), col(wv, yoff),
                  cw(wq, qoff), cw(wk, koff), cw(wv, voff),
                  cb(wq, qoff), cb(wk, koff), cb(wv, voff),
                  pl.BlockSpec((1, 1, chunk, hb), lambda i, h, t: (i, h, t, 0)),
                  pl.BlockSpec((1, 1, chunk, hb), lambda i, h, t: (i, h, t, 0)),
                  pl.BlockSpec((1, 1, 1, hb, chunk), lambda i, h, t: (i, h, t, 0, 0)),
                  pl.BlockSpec((1, dv), lambda i, h, t: (0, 0)),
                  st],
        out_specs=[pl.BlockSpec((1, chunk, wv), lambda i, h, t: (i, t, h)), st],
        out_shape=[jax.ShapeDtypeStruct((b, l, vw), BF16), jax.ShapeDtypeStruct((b, heads, dk, dv), F32)],
        scratch_shapes=[pltpu.VMEM((SUBLANES, wq), F32), pltpu.VMEM((SUBLANES, wk), F32),
                        pltpu.VMEM((SUBLANES, wv), F32)],
        compiler_params=_params(("arbitrary", "arbitrary", "arbitrary")),
        name="gdn",
    )(zmain, zmain, zmain, zmain, p["gdn_conv_w"], p["gdn_conv_w"], p["gdn_conv_w"],
      cbuf, cbuf, cbuf, gcol, bcol, grow, p["gdn_norm_w"][o:o + 1], s0.astype(F32))
    tail = jnp.concatenate([conv_buf.astype(F32), zmain[:, -(kconv - 1):, :conv_ch]], axis=1)[:, -(kconv - 1):]
    return out, tail.astype(conv_buf.dtype), s_fin.astype(s0.dtype)


def _hgrn_kernel(q_ref, f_ref, i_ref, og_ref, lbl_ref, nw_ref, s0_ref, o_ref, s_ref, *, hb, dk, dv, layer):
    @pl.when(pl.program_id(2) == 0)
    def _():
        s_ref[0] = s0_ref[0]

    c = q_ref.shape[1]
    logits = lbl_ref[...]
    mx = jnp.max(logits, axis=0, keepdims=True)
    ex = jnp.exp(logits - mx)
    den = jnp.sum(ex, axis=0, keepdims=True)
    lb_all = jnp.zeros_like(den)
    for r in range(1, layer + 1):
        lb_all = lb_all + ex[r:r + 1] / den
    tril_f = _tril(c).astype(F32)
    nb = max(c // SUB, 1)
    sb = min(SUB, c)
    rowi = _iota2((sb, 1), 0)
    eye = _iota2((dk, dk), 0) == _iota2((dk, dk), 1)
    outs = []
    for hl in range(hb):
        lo, hi_ = hl * dk, (hl + 1) * dk
        lb = lb_all[:, lo:hi_]
        q = q_ref[0, :, lo:hi_]
        f = lb + (1.0 - lb) * _sigmoid(f_ref[0, :, lo:hi_])
        k = 1.0 - f
        v = i_ref[0, :, hl * dv:(hl + 1) * dv]
        bcum = _mm(tril_f, jnp.log(f), hi=True)
        s = s_ref[0, hl]
        blocks = []
        for bi in range(nb):
            r0 = bi * sb
            qi, ki, vi, bb = q[r0:r0 + sb], k[r0:r0 + sb], v[r0:r0 + sb], bcum[r0:r0 + sb]
            acc = _mm(qi * jnp.exp(bb), s, hi=True)
            for si in range(sb):
                msk = rowi >= si
                dec = jnp.where(msk, jnp.exp(jnp.where(msk, bb - bb[si:si + 1], 0.0)), 0.0)
                colv = jnp.sum(qi * ki[si:si + 1] * dec, axis=-1, keepdims=True)
                acc = acc + colv * vi[si:si + 1]
            if bi > 0:
                bref = bcum[r0 - 1:r0]
                qs = qi * jnp.exp(bb - bref)
                ks = k[:r0] * jnp.exp(bref - bcum[:r0])
                acc = acc + _mm(_mm_nt(qs, ks, hi=True), v[:r0], hi=True)
            blocks.append(acc)
        o = blocks[0] if nb == 1 else jnp.concatenate(blocks, axis=0)
        b_last = bcum[c - 1:c]
        kd = k * jnp.exp(b_last - bcum)
        e_col = jnp.sum(jnp.where(eye, jnp.exp(b_last), 0.0), axis=1, keepdims=True)
        s_ref[0, hl] = e_col * s + _mm_tn(kd, v, hi=True)
        og = og_ref[0, :, hl * dv:(hl + 1) * dv]
        outs.append(_head_rmsnorm(o, nw_ref[...]) * _sigmoid(og))
    o_ref[0] = (outs[0] if hb == 1 else jnp.concatenate(outs, axis=1)).astype(o_ref.dtype)


def _hgrn2(z, s0, p, o, layer, chunk):
    b, l, zw = z.shape
    dv = p["hg_norm_w"].shape[1]
    depth, kw = p["hg_lb_logits"].shape
    vw = (zw - 2 * kw) // 2
    heads = vw // dv
    dk = kw // heads
    hb = 1
    nhb = heads // hb
    wk, wv = hb * dk, hb * dv
    col = lambda w_, off: pl.BlockSpec((1, chunk, w_), lambda i, h, t: (i, t, off + h))
    st = pl.BlockSpec((1, hb, dk, dv), lambda i, h, t: (i, h, 0, 0))
    out, s_fin = pl.pallas_call(
        functools.partial(_hgrn_kernel, hb=hb, dk=dk, dv=dv, layer=layer),
        grid=(b, nhb, l // chunk),
        in_specs=[col(wk, 0), col(wk, kw // wk), col(wv, 2 * kw // wv), col(wv, (2 * kw + vw) // wv),
                  pl.BlockSpec((depth, wk), lambda i, h, t: (0, h)),
                  pl.BlockSpec((1, dv), lambda i, h, t: (0, 0)),
                  st],
        out_specs=[pl.BlockSpec((1, chunk, wv), lambda i, h, t: (i, t, h)), st],
        out_shape=[jax.ShapeDtypeStruct((b, l, vw), BF16), jax.ShapeDtypeStruct((b, heads, dk, dv), F32)],
        compiler_params=_params(("arbitrary", "arbitrary", "arbitrary")),
        name="hgrn2",
    )(z, z, z, z, p["hg_lb_logits"].astype(F32), p["hg_norm_w"][o:o + 1], s0.astype(F32))
    return out, s_fin.astype(s0.dtype)


def _trunk(x, mod_all, st, p, prep):
    rw_s, rw_sh, s5_re, s5_im, gdn_s, gdn_cv, hg_s = st
    b, l, d = x.shape
    depth = p["w_mod"].shape[0]
    d_ff = p["w_ffn_out"].shape[2]
    m = b * l
    rpg = l if l % SUBLANES == 0 and l >= 256 else None
    n_rw, n_sh, n_re, n_im, n_gdn, n_cv, n_hg = [], [], [], [], [], [], []
    chunk = min(CHUNK, l)

    def gate_of(mod9, idx):
        g = mod9[:, 3 * idx + 2]
        if rpg is not None:
            return g.reshape(b, 1, d)
        return jnp.repeat(g, l, axis=0).reshape(1, m, d)

    def ffn(x, mod9, lyr, idx, slot):
        h = _norm(x, p["norm_g"][lyr, idx], mod9, idx).reshape(m, d)
        tn = _pick_tile(d_ff, 256)
        act = _matmul(h, p["w_ffn_in"], (lyr, slot), ((0, d_ff // tn), d_ff // tn), tn,
                      epi="swiglu", out_dtype=BF16)
        tn2 = _pick_tile(d, 512)
        y = _matmul(act, p["w_ffn_out"], (lyr, slot), ((0,), d // tn2), tn2, epi="res",
                    res=x.reshape(m, d), gate=gate_of(mod9, idx), rows_per_gate=rpg, coef=0.5,
                    w_buffers=1)
        return y.reshape(b, l, d)

    for lyr in range(depth):
        mod9 = mod_all[lyr]
        x = ffn(x, mod9, lyr, 0, 0)
        h = _norm(x, p["norm_g"][lyr, 1], mod9, 1).reshape(m, d)
        if lyr % 2 == 0:
            e = lyr // 2
            even_in = p["w_in_even"].shape[2]
            rw_in = p["rw_mu"].shape[1]
            tn = _pick_tile(even_in, 1280, 256)
            zin = _matmul(h, p["w_in_even"], (e,), ((0,), even_in // tn), tn, w_buffers=1).reshape(b, l, even_in)
            oa, sh, s = _rwkv7(zin, rw_in, rw_sh[e], rw_s[e], p, e)
            ob, hr, hi = _s5(zin[..., rw_in:], s5_re[e], s5_im[e], p, e)
            n_rw.append(s)
            n_sh.append(sh)
            n_re.append(hr)
            n_im.append(hi)
            mix = jnp.concatenate([oa, ob.astype(BF16)], axis=-1).reshape(m, d)
            w_out, widx = p["w_out_even"], (e,)
        else:
            o = lyr // 2
            w_hg, w_ab, gdn_main = prep["odd"][o]
            tn = _pick_tile(gdn_main, 512)
            zmain = _matmul(h, p["w_in_odd"], (o,), ((0,), gdn_main // tn), tn).reshape(b, l, gdn_main)
            zab = _matmul(h, w_ab, (), ((0,), 1), LANES).reshape(b, l, LANES)
            hg_in = w_hg.shape[1]
            tn = _pick_tile(hg_in, 512)
            zhg = _matmul(h, w_hg, (), ((0,), hg_in // tn), tn).reshape(b, l, hg_in)
            oc, cv, s = _gdn(zmain, zab, gdn_cv[o], gdn_s[o], p, o, chunk)
            od, sh_ = _hgrn2(zhg, hg_s[o], p, o, lyr, chunk)
            n_gdn.append(s)
            n_cv.append(cv)
            n_hg.append(sh_)
            mix = jnp.concatenate([oc, od], axis=-1).reshape(m, d)
            w_out, widx = p["w_out_odd"], (o,)
        tn = _pick_tile(d, 512)
        x = _matmul(mix, w_out, widx, ((0,), d // tn), tn, epi="res", res=x.reshape(m, d),
                    gate=gate_of(mod9, 1), rows_per_gate=rpg, coef=1.0).reshape(b, l, d)
        x = ffn(x, mod9, lyr, 2, 1)
    y = _norm(x, p["norm_final"], out_dtype=x.dtype)
    return y, (jnp.stack(n_rw), jnp.stack(n_sh), jnp.stack(n_re), jnp.stack(n_im),
               jnp.stack(n_gdn), jnp.stack(n_cv), jnp.stack(n_hg))


def kernel(x_prompt, x_sample, c_prompt, c_sample, state_rwkv, state_rwkv_shift, state_s5_re, state_s5_im, state_gdn, cache_gdn_conv, state_hgrn, w_mod, b_mod, norm_g, norm_final, w_ffn_in, w_ffn_out, w_in_even, w_out_even, rw_mu, rw_w0, rw_w_up, rw_a0, rw_a_up, rw_g_up, rw_k_k, rw_k_a, rw_r_k, rw_ln_w, rw_ln_b, s5_lambda_re, s5_lambda_im, s5_log_dt, s5_b_re, s5_b_im, s5_c_re, s5_c_im, s5_d, s5_glu_w, s5_glu_b, w_in_odd, w_out_odd, gdn_conv_w, gdn_a_log, gdn_dt_bias, gdn_norm_w, hg_lb_logits, hg_norm_w):
    p = dict(w_mod=w_mod, b_mod=b_mod, norm_g=norm_g, norm_final=norm_final,
             w_ffn_in=w_ffn_in, w_ffn_out=w_ffn_out, w_in_even=w_in_even, w_out_even=w_out_even,
             rw_mu=rw_mu, rw_w0=rw_w0, rw_w_up=rw_w_up, rw_a0=rw_a0, rw_a_up=rw_a_up, rw_g_up=rw_g_up,
             rw_k_k=rw_k_k, rw_k_a=rw_k_a, rw_r_k=rw_r_k, rw_ln_w=rw_ln_w, rw_ln_b=rw_ln_b,
             s5_lambda_re=s5_lambda_re, s5_lambda_im=s5_lambda_im, s5_log_dt=s5_log_dt,
             s5_b_re=s5_b_re, s5_b_im=s5_b_im, s5_c_re=s5_c_re, s5_c_im=s5_c_im, s5_d=s5_d,
             s5_glu_w=s5_glu_w, s5_glu_b=s5_glu_b, w_in_odd=w_in_odd, w_out_odd=w_out_odd,
             gdn_conv_w=gdn_conv_w, gdn_a_log=gdn_a_log, gdn_dt_bias=gdn_dt_bias, gdn_norm_w=gdn_norm_w,
             hg_lb_logits=hg_lb_logits, hg_norm_w=hg_norm_w)
    bp, bs = x_prompt.shape[0], x_sample.shape[0]
    d = x_prompt.shape[-1]
    depth = w_mod.shape[0]
    n_even, n_odd = (depth + 1) // 2, depth // 2
    dtp = x_prompt.dtype

    bc = bp + bs
    bc_pad = -(-bc // 16) * 16
    c_all = jnp.pad(jnp.concatenate([c_prompt, c_sample], axis=0), ((0, bc_pad - bc), (0, 0)))
    mod = _modulation(c_all, w_mod, b_mod).reshape(depth, bc_pad, 9, d)
    mod_p, mod_s = mod[:, :bp], mod[:, bp:bc]

    heads = gdn_a_log.shape[1]
    conv_ch = gdn_conv_w.shape[2]
    vw = heads * gdn_norm_w.shape[1]
    gdn_main = conv_ch + vw
    gdn_in = gdn_main + 2 * heads
    prep = {"odd": []}
    for o in range(n_odd):
        w_hg = w_in_odd[o][:, gdn_in:]
        w_ab = jnp.pad(w_in_odd[o][:, gdn_main:gdn_in], ((0, 0), (0, LANES - 2 * heads)))
        prep["odd"].append((w_hg, w_ab, gdn_main))

    rw_h, rw_n = rw_r_k.shape[1], rw_r_k.shape[2]
    rw_in = rw_mu.shape[1]
    s5_g, s5_n = s5_lambda_re.shape[1], s5_lambda_re.shape[2]
    gdn_dk = (conv_ch - vw) // 2 // heads
    gdn_dv = gdn_norm_w.shape[1]
    kconv = gdn_conv_w.shape[1]
    hg_dv = hg_norm_w.shape[1]
    hg_kw = hg_lb_logits.shape[1]
    hg_vw = (w_in_odd.shape[2] - gdn_in - 2 * hg_kw) // 2
    hg_h = hg_vw // hg_dv
    hg_dk = hg_kw // hg_h
    st_prompt = (jnp.zeros((n_even, bp, rw_h, rw_n, rw_n), dtp),
                 jnp.zeros((n_even, bp, rw_in), dtp),
                 jnp.zeros((n_even, bp, s5_g, s5_n), dtp),
                 jnp.zeros((n_even, bp, s5_g, s5_n), dtp),
                 jnp.zeros((n_odd, bp, heads, gdn_dk, gdn_dv), dtp),
                 jnp.zeros((n_odd, bp, kconv - 1, conv_ch), dtp),
                 jnp.zeros((n_odd, bp, hg_h, hg_dk, hg_dv), dtp))
    st_sample = (state_rwkv, state_rwkv_shift, state_s5_re, state_s5_im, state_gdn, cache_gdn_conv, state_hgrn)
    y_prompt, ns_p = _trunk(x_prompt, mod_p, st_prompt, p, prep)
    y_sample, ns_s = _trunk(x_sample, mod_s, st_sample, p, prep)
    return (y_prompt, y_sample) + tuple(ns_p) + tuple(ns_s)
```

```python
import functools

import jax
import jax.numpy as jnp
from jax import lax
from jax.experimental import pallas as pl
from jax.experimental.pallas import tpu as pltpu

F32 = jnp.float32
BF16 = jnp.bfloat16
HI = lax.Precision.HIGHEST

EPS = 1e-6
RW_GN_EPS = 64e-5
L2_EPS = 1e-12
CHUNK = 64
SUB = 16
GDN_HEADS_PER_STEP = 8
HGRN_HEADS_PER_STEP = 8
LANES = 128
SUBLANES = 8
VMEM_LIMIT = 56 * 1024 * 1024


def _params(sem, big=False):
    return pltpu.CompilerParams(dimension_semantics=sem,
                                vmem_limit_bytes=VMEM_LIMIT if big else None)


def _pick_tile(n, pref, mult=LANES):
    t = (min(pref, n) // mult) * mult
    while t >= mult:
        if n % t == 0:
            return t
        t -= mult
    return n


def _mm(a, b, hi=False):
    if not hi:
        a, b = a.astype(BF16), b.astype(BF16)
    return lax.dot_general(a, b, (((1,), (0,)), ((), ())), precision=HI if hi else None,
                           preferred_element_type=F32)


def _mm_nt(a, b, hi=False):
    if not hi:
        a, b = a.astype(BF16), b.astype(BF16)
    return lax.dot_general(a, b, (((1,), (1,)), ((), ())), precision=HI if hi else None,
                           preferred_element_type=F32)


def _mm_tn(a, b, hi=False):
    if not hi:
        a, b = a.astype(BF16), b.astype(BF16)
    return lax.dot_general(a, b, (((0,), (0,)), ((), ())), precision=HI if hi else None,
                           preferred_element_type=F32)


def _split2(a):
    hi = a.astype(BF16)
    return hi, (a - hi.astype(F32)).astype(BF16)


def _split3(a):
    hi = a.astype(BF16)
    r = a - hi.astype(F32)
    mid = r.astype(BF16)
    return hi, mid, (r - mid.astype(F32)).astype(BF16)


def _bdot(a, b, dims):
    return lax.dot_general(a, b, (dims, ((), ())), preferred_element_type=F32)


def _mm3(a, b, dims=((1,), (0,))):
    ah, al = _split2(a)
    bh, bl = _split2(b)
    return _bdot(ah, bh, dims) + (_bdot(ah, bl, dims) + _bdot(al, bh, dims))


def _mm01(mask, x):
    m = mask.astype(BF16)
    xh, xm, xl = _split3(x)
    dims = ((1,), (0,))
    return _bdot(m, xh, dims) + (_bdot(m, xm, dims) + _bdot(m, xl, dims))


def _x01(x, mask):
    m = mask.astype(BF16)
    xh, xm, xl = _split3(x)
    dims = ((1,), (0,))
    return _bdot(xh, m, dims) + (_bdot(xm, m, dims) + _bdot(xl, m, dims))


def _sigmoid(x):
    return 1.0 / (1.0 + jnp.exp(-x))


def _silu(x):
    return x * _sigmoid(x)


def _softplus(x):
    return jnp.maximum(x, 0.0) + jnp.log(1.0 + jnp.exp(-jnp.abs(x)))


def _iota2(shape, axis):
    return lax.broadcasted_iota(jnp.int32, shape, axis)


def _mod_kernel(c_ref, w_ref, b_ref, o_ref):
    s = _silu(c_ref[...]).astype(BF16)
    o_ref[0] = jnp.dot(s, w_ref[0].astype(BF16), preferred_element_type=F32) + b_ref[0]


def _modulation(c, w_mod, b_mod):
    depth, d, n = w_mod.shape
    bc = c.shape[0]
    tn = _pick_tile(n, 512)
    return pl.pallas_call(
        _mod_kernel,
        grid=(depth, n // tn),
        in_specs=[pl.BlockSpec((bc, d), lambda l, j: (0, 0)),
                  pl.BlockSpec((1, d, tn), lambda l, j: (l, 0, j)),
                  pl.BlockSpec((1, 1, tn), lambda l, j: (l, 0, j))],
        out_specs=pl.BlockSpec((1, bc, tn), lambda l, j: (l, 0, j)),
        out_shape=jax.ShapeDtypeStruct((depth, bc, n), F32),
        compiler_params=_params(("arbitrary", "arbitrary"), big=True),
        name="modulation",
    )(c, w_mod, b_mod.reshape(depth, 1, n))


def _norm_kernel(x_ref, g_ref, *rest, idx):
    if idx is None:
        (o_ref,) = rest
    else:
        m_ref, o_ref = rest
    x = x_ref[0]
    y = x * lax.rsqrt(jnp.mean(x * x, axis=-1, keepdims=True) + EPS) * g_ref[...]
    if idx is not None:
        shift = m_ref[0, pl.ds(3 * idx, 1), :]
        scale = m_ref[0, pl.ds(3 * idx + 1, 1), :]
        y = y * (1.0 + scale) + shift
    o_ref[0] = y.astype(o_ref.dtype)


def _norm(x, g, mod9=None, idx=None, out_dtype=BF16):
    b, l, d = x.shape
    tl = _pick_tile(l, 256, SUBLANES)
    in_specs = [pl.BlockSpec((1, tl, d), lambda i, t: (i, t, 0)),
                pl.BlockSpec((1, d), lambda i, t: (0, 0))]
    args = [x, g.reshape(1, d)]
    if idx is not None:
        in_specs.append(pl.BlockSpec((1, mod9.shape[1], d), lambda i, t: (i, 0, 0)))
        args.append(mod9)
    return pl.pallas_call(
        functools.partial(_norm_kernel, idx=idx),
        grid=(b, l // tl),
        in_specs=in_specs,
        out_specs=pl.BlockSpec((1, tl, d), lambda i, t: (i, t, 0)),
        out_shape=jax.ShapeDtypeStruct((b, l, d), out_dtype),
        compiler_params=_params(("arbitrary", "arbitrary")),
        name="norm",
    )(*args)


def _mm_kernel(*refs, n_w, epi, coef):
    x_ref = refs[0]
    w_refs = refs[1:1 + n_w]
    pos = 1 + n_w
    if epi == "res":
        res_ref, gate_ref = refs[pos], refs[pos + 1]
        pos += 2
    o_ref = refs[pos]
    wb_refs = refs[pos + 1:pos + 1 + n_w]

    @pl.when(pl.program_id(1) == 0)
    def _():
        for w_ref, wb_ref in zip(w_refs, wb_refs):
            wb_ref[...] = w_ref[...].astype(BF16)

    x = x_ref[...]
    acc = [jnp.dot(x, wb_ref[...], preferred_element_type=F32) for wb_ref in wb_refs]
    if epi == "swiglu":
        out = _silu(acc[0]) * acc[1]
    elif epi == "res":
        out = res_ref[...] + coef * gate_ref[0] * acc[0]
    else:
        out = acc[0]
    o_ref[...] = out.astype(o_ref.dtype)


def _matmul(x, w, prefix, col_blocks, tn, *, rows_per_gate=None, res=None, gate=None,
            coef=1.0, epi="plain", out_dtype=F32, tm_pref=512, w_buffers=2):
    m, k = x.shape
    offs, n_cols = col_blocks
    n_w = len(offs)
    n_out = n_cols * tn
    if rows_per_gate is not None:
        tm = _pick_tile(rows_per_gate, tm_pref, SUBLANES)
    else:
        tm = _pick_tile(m, tm_pref, SUBLANES)
    npre = len(prefix)
    wmode = {} if w_buffers == 2 else {"pipeline_mode": pl.Buffered(w_buffers)}
    in_specs = [pl.BlockSpec((tm, k), lambda j, i: (i, 0))]
    args = [x]
    for off in offs:
        in_specs.append(pl.BlockSpec((None,) * npre + (k, tn),
                                     lambda j, i, off=off: tuple(prefix) + (0, j + off), **wmode))
        args.append(w)
    if epi == "res":
        in_specs.append(pl.BlockSpec((tm, tn), lambda j, i: (i, j)))
        args.append(res)
        if rows_per_gate is not None:
            rpt = rows_per_gate // tm
            in_specs.append(pl.BlockSpec((1, 1, tn), lambda j, i: (i // rpt, 0, j)))
        else:
            in_specs.append(pl.BlockSpec((1, tm, tn), lambda j, i: (0, i, j)))
        args.append(gate)
    return pl.pallas_call(
        functools.partial(_mm_kernel, n_w=n_w, epi=epi, coef=coef),
        grid=(n_cols, m // tm),
        in_specs=in_specs,
        out_specs=pl.BlockSpec((tm, tn), lambda j, i: (i, j)),
        out_shape=jax.ShapeDtypeStruct((m, n_out), out_dtype),
        scratch_shapes=[pltpu.VMEM((k, tn), BF16) for _ in range(n_w)],
        compiler_params=_params(("arbitrary", "arbitrary"), big=True),
        name="matmul_" + epi,
    )(*args)


def _group_ones(n):
    r = _iota2((LANES, LANES), 0)
    c = _iota2((LANES, LANES), 1)
    sh = n.bit_length() - 1
    return (jnp.right_shift(r, sh) == jnp.right_shift(c, sh)).astype(F32)


def _group_sum(x, n):
    if n % LANES == 0:
        parts = []
        for h in range(x.shape[1] // n):
            s = jnp.sum(x[:, h * n:(h + 1) * n], axis=-1, keepdims=True)
            parts.append(jnp.broadcast_to(s, (x.shape[0], n)))
        return parts[0] if len(parts) == 1 else jnp.concatenate(parts, axis=1)
    ones = _group_ones(n)
    parts = [_x01(x[:, s * LANES:(s + 1) * LANES], ones) for s in range(x.shape[1] // LANES)]
    return parts[0] if len(parts) == 1 else jnp.concatenate(parts, axis=1)


def _tril(c, strict=False):
    r = _iota2((c, c), 0)
    s = _iota2((c, c), 1)
    return (r > s) if strict else (r >= s)


def _unit_lower_solve(lms, rhss):
    c = lms[0].shape[0]
    r = _iota2((c, c), 0)
    s = _iota2((c, c), 1)
    eye = (r == s).astype(F32)
    if c <= SUB:
        lds, los = lms, None
    else:
        sh = SUB.bit_length() - 1
        same = jnp.right_shift(r, sh) == jnp.right_shift(s, sh)
        lds = [jnp.where(same, lm, 0.0) for lm in lms]
        los = [lm - ld for lm, ld in zip(lms, lds)]
    order = min(SUB, c)
    xs = [-ld for ld in lds]
    ps = [eye + x for x in xs]
    pws = [_mm3(x, x) for x in xs]
    k = 2
    while k < order:
        new_ps = [p + _mm3(pw, p) for p, pw in zip(ps, pws)]
        if 2 * k < order:
            pws = [_mm3(pw, pw) for pw in pws]
        ps = new_ps
        k *= 2
    sols = [_mm3(p, rhs) for p, rhs in zip(ps, rhss)]
    if los is None:
        return sols
    pws = [-_mm3(p, lo) for p, lo in zip(ps, los)]
    k = 1
    while True:
        new_sols = [sol + _mm3(pw, sol) for sol, pw in zip(sols, pws)]
        if 2 * k >= c // SUB:
            return new_sols
        pws = [_mm3(pw, pw) for pw in pws]
        sols = new_sols
        k *= 2


def _head_rmsnorm(o, w):
    return o * lax.rsqrt(jnp.mean(o * o, axis=-1, keepdims=True) + EPS) * w


def _rw_pre_kernel(z_ref, sh_ref, mu_ref, w0_ref, a0_ref, kk_ref, ka_ref, rk_ref,
                   wup_ref, aup_ref, gup_ref,
                   r_o, d_o, k_o, v_o, kk_o, nkka_o, g_o, bonus_o, prev_scr,
                   *, width, w_lora, a_lora, head_dim):
    @pl.when(pl.program_id(1) == 0)
    def _():
        prev_scr[...] = sh_ref[0]

    z = z_ref[0]
    tl = z.shape[0]
    rolled = pltpu.roll(z, 1, 0)
    prev = jnp.where(_iota2(z.shape, 0) == 0, prev_scr[...], rolled)
    prev_scr[...] = z_ref[0, pl.ds(tl - 1, 1), :]
    zs = z + mu_ref[...] * (prev - z)
    c1, c2, c3 = width, 2 * width, 3 * width
    c4 = c3 + w_lora
    c5 = c4 + a_lora
    r, k, v = zs[:, :c1], zs[:, c1:c2], zs[:, c2:c3]
    xw, xa, xg = zs[:, c3:c4], zs[:, c4:c5], zs[:, c5:]
    w = -_softplus(-(w0_ref[...] + _mm(jnp.tanh(xw), wup_ref[...]))) - 0.5
    decay = jnp.exp(-jnp.exp(w))
    a = _sigmoid(a0_ref[...] + _mm(xa, aup_ref[...]))
    g = _mm(_sigmoid(xg), gup_ref[...])
    kraw = k * kk_ref[...]
    kk = kraw * lax.rsqrt(_group_sum(kraw * kraw, head_dim) + L2_EPS)
    k2 = k * (1.0 + (a - 1.0) * ka_ref[...])
    bonus = _group_sum(r * k2 * rk_ref[...], head_dim) * v
    r_o[0] = r
    d_o[0] = decay
    k_o[0] = k2
    v_o[0] = v
    kk_o[0] = kk
    nkka_o[0] = -(kk * a)
    g_o[0] = g
    bonus_o[0] = bonus


def _rw_pre(zin, rw_in, shift, p, e, head_dim):
    b, l, _ = zin.shape
    width = p["rw_w0"].shape[1]
    w_lora = p["rw_w_up"].shape[1]
    a_lora = p["rw_a_up"].shape[1]
    g_lora = p["rw_g_up"].shape[1]
    tl = _pick_tile(l, 128, SUBLANES)
    row = lambda n: pl.BlockSpec((1, n), lambda i, t: (0, 0))
    out_spec = pl.BlockSpec((1, tl, width), lambda i, t: (i, t, 0))
    out_shape = jax.ShapeDtypeStruct((b, l, width), F32)
    return pl.pallas_call(
        functools.partial(_rw_pre_kernel, width=width, w_lora=w_lora, a_lora=a_lora, head_dim=head_dim),
        grid=(b, l // tl),
        in_specs=[pl.BlockSpec((1, tl, rw_in), lambda i, t: (i, t, 0)),
                  pl.BlockSpec((1, 1, rw_in), lambda i, t: (i, 0, 0)),
                  row(rw_in), row(width), row(width), row(width), row(width), row(width),
                  pl.BlockSpec((None, w_lora, width), lambda i, t: (e, 0, 0)),
                  pl.BlockSpec((None, a_lora, width), lambda i, t: (e, 0, 0)),
                  pl.BlockSpec((None, g_lora, width), lambda i, t: (e, 0, 0))],
        out_specs=[out_spec] * 8,
        out_shape=[out_shape] * 8,
        scratch_shapes=[pltpu.VMEM((1, rw_in), F32)],
        compiler_params=_params(("arbitrary", "arbitrary"), big=True),
        name="rwkv_pre",
    )(zin, shift.reshape(b, 1, rw_in), p["rw_mu"][e:e + 1], p["rw_w0"][e:e + 1], p["rw_a0"][e:e + 1],
      p["rw_k_k"][e:e + 1], p["rw_k_a"][e:e + 1], p["rw_r_k"][e].reshape(1, width),
      p["rw_w_up"], p["rw_a_up"], p["rw_g_up"])


def _rw_scan_kernel(r_ref, d_ref, k_ref, v_ref, kk_ref, nkka_ref, s0_ref, o_ref, s_ref, *, steps, n):
    @pl.when(pl.program_id(1) == 0)
    def _():
        s_ref[...] = s0_ref[...]

    sa0 = s_ref[0] * kk_ref[0, pl.ds(0, 1), :]
    for j in range(1, n):
        sa0 = sa0 + s_ref[j] * kk_ref[0, pl.ds(j, 1), :]

    def step(t, sa):
        tn = jnp.minimum(t + 1, steps - 1)
        v_t = v_ref[t]
        o = None
        sa_next = None
        for j in range(n):
            sj = (s_ref[j] * d_ref[t, pl.ds(j, 1), :] + sa * nkka_ref[t, pl.ds(j, 1), :]
                  + v_t * k_ref[t, pl.ds(j, 1), :])
            s_ref[j] = sj
            term = sj * r_ref[t, pl.ds(j, 1), :]
            o = term if o is None else o + term
            nxt = sj * kk_ref[tn, pl.ds(j, 1), :]
            sa_next = nxt if sa_next is None else sa_next + nxt
        o_ref[t] = o
        return sa_next

    lax.fori_loop(0, steps, step, sa0)


def _rw_scan(seqs, s0):
    l, n, bh = seqs[0].shape
    lb = LANES if bh % LANES == 0 else bh
    steps = _pick_tile(l, 32, 1)
    seq_spec = pl.BlockSpec((steps, n, lb), lambda c, t: (t, 0, c))
    st_spec = pl.BlockSpec((n, n, lb), lambda c, t: (0, 0, c))
    return pl.pallas_call(
        functools.partial(_rw_scan_kernel, steps=steps, n=n),
        grid=(bh // lb, l // steps),
        in_specs=[seq_spec] * 6 + [st_spec],
        out_specs=[seq_spec, st_spec],
        out_shape=[jax.ShapeDtypeStruct((l, n, bh), F32), jax.ShapeDtypeStruct((n, n, bh), F32)],
        compiler_params=_params(("arbitrary", "arbitrary"), big=True),
        name="rwkv_scan",
    )(*seqs, s0)


def _rw_post_kernel(o_ref, bonus_ref, g_ref, lnw_ref, lnb_ref, out_ref, *, head_dim):
    o = o_ref[0]
    mu = _group_sum(o, head_dim) * (1.0 / head_dim)
    c = o - mu
    var = _group_sum(c * c, head_dim) * (1.0 / head_dim)
    y = c * lax.rsqrt(var + RW_GN_EPS) * lnw_ref[...] + lnb_ref[...]
    out_ref[0] = ((y + bonus_ref[0]) * g_ref[0]).astype(out_ref.dtype)


def _rw_post(o, bonus, g, p, e, head_dim):
    b, l, width = o.shape
    tl = _pick_tile(l, 256, SUBLANES)
    spec = pl.BlockSpec((1, tl, width), lambda i, t: (i, t, 0))
    row = pl.BlockSpec((1, width), lambda i, t: (0, 0))
    return pl.pallas_call(
        functools.partial(_rw_post_kernel, head_dim=head_dim),
        grid=(b, l // tl),
        in_specs=[spec, spec, spec, row, row],
        out_specs=spec,
        out_shape=jax.ShapeDtypeStruct((b, l, width), BF16),
        compiler_params=_params(("arbitrary", "arbitrary")),
        name="rwkv_post",
    )(o, bonus, g, p["rw_ln_w"][e:e + 1], p["rw_ln_b"][e:e + 1])


def _rwkv7(zin, rw_in, shift, s0, p, e):
    b, l, _ = zin.shape
    heads, n = p["rw_r_k"].shape[1], p["rw_r_k"].shape[2]
    r, d, k, v, kk, nkka, g, bonus = _rw_pre(zin, rw_in, shift, p, e, n)

    def to_lanes(t):
        return t.reshape(b, l, heads, n).transpose(1, 3, 0, 2).reshape(l, n, b * heads)

    s0_l = s0.astype(F32).transpose(3, 2, 0, 1).reshape(n, n, b * heads)
    o_l, s_l = _rw_scan([to_lanes(t) for t in (r, d, k, v, kk, nkka)], s0_l)
    o = o_l.reshape(l, n, b, heads).transpose(2, 0, 3, 1).reshape(b, l, heads * n)
    s_fin = s_l.reshape(n, n, b, heads).transpose(2, 3, 1, 0)
    out = _rw_post(o, bonus, g, p, e, n)
    return out, zin[:, -1, :rw_in].astype(shift.dtype), s_fin.astype(s0.dtype)


def _s5_param_kernel(ldt_ref, lr_ref, li_ref, bre_ref, bim_ref, are_o, aim_o, bbre_o, bbim_o):
    dt = jnp.exp(ldt_ref[...])
    lr, li = lr_ref[...], li_ref[...]
    mag = jnp.exp(lr * dt)
    ab_re, ab_im = mag * jnp.cos(li * dt), mag * jnp.sin(li * dt)
    den = lr * lr + li * li
    pr, pi_ = ab_re - 1.0, ab_im
    coef_re = (pr * lr + pi_ * li) / den
    coef_im = (pi_ * lr - pr * li) / den
    are_o[...] = ab_re
    aim_o[...] = ab_im
    for m in range(bre_ref.shape[0]):
        b_re, b_im = bre_ref[m], bim_ref[m]
        bbre_o[m] = coef_re * b_re - coef_im * b_im
        bbim_o[m] = coef_re * b_im + coef_im * b_re


def _s5_params(p, e):
    g, n, m = p["s5_b_re"].shape[1:]
    outs = pl.pallas_call(
        _s5_param_kernel,
        out_shape=[jax.ShapeDtypeStruct((g, n), F32)] * 2 + [jax.ShapeDtypeStruct((m, g, n), F32)] * 2,
        name="s5_params",
    )(p["s5_log_dt"][e].reshape(g, 1), p["s5_lambda_re"][e], p["s5_lambda_im"][e],
      p["s5_b_re"][e].transpose(2, 0, 1), p["s5_b_im"][e].transpose(2, 0, 1))
    return outs


def _block_diag(t, gs):
    g, a, b = t.shape
    t = t.reshape(g // gs, gs, a, b)
    eye = jnp.eye(gs, dtype=t.dtype)
    return jnp.einsum("sgab,gh->sgahb", t, eye).reshape(g // gs, gs * a, gs * b)


def _s5_kernel(u_ref, bre_ref, bim_ref, cre_ref, cim_ref, d_ref, gw_ref, gb_ref, are_ref, aim_ref,
               h0r_ref, h0i_ref, o_ref, hr_ref, hi_ref, xr_scr, xi_scr, *, steps):
    @pl.when(pl.program_id(1) == 0)
    def _():
        hr_ref[...] = h0r_ref[...]
        hi_ref[...] = h0i_ref[...]

    bp, gc = u_ref.shape[1], u_ref.shape[2]
    sn = xr_scr.shape[2]
    u2 = u_ref[...].reshape(steps * bp, gc)
    xr_scr[...] = _mm(u2, bre_ref[0]).reshape(steps, bp, sn)
    xi_scr[...] = _mm(u2, bim_ref[0]).reshape(steps, bp, sn)
    a_re, a_im = are_ref[0], aim_ref[0]

    def step(t, carry):
        h_re, h_im = carry
        n_re = a_re * h_re - a_im * h_im + xr_scr[t]
        n_im = a_re * h_im + a_im * h_re + xi_scr[t]
        xr_scr[t] = n_re
        xi_scr[t] = n_im
        return n_re, n_im

    h_re, h_im = lax.fori_loop(0, steps, step, (hr_ref[...], hi_ref[...]))
    hr_ref[...] = h_re
    hi_ref[...] = h_im
    hs_re = xr_scr[...].reshape(steps * bp, sn)
    hs_im = xi_scr[...].reshape(steps * bp, sn)
    y = _mm(hs_re, cre_ref[0]) - _mm(hs_im, cim_ref[0]) + d_ref[0] * u2
    yg = 0.5 * y * (1.0 + jnp.tanh(0.7978845608028654 * (y + 0.044715 * (y * y * y))))
    out = yg * _sigmoid(_mm(yg, gw_ref[0]) + gb_ref[0])
    o_ref[...] = out.reshape(steps, bp, gc)


def _s5(u, h_re, h_im, p, e):
    b, l, width = u.shape
    g, n, m = p["s5_b_re"].shape[1:]
    gs = LANES // m
    ns = g // gs
    sn = gs * n
    ab_re, ab_im, bb_re, bb_im = _s5_params(p, e)
    bd_bre = _block_diag(bb_re.transpose(1, 0, 2), gs)
    bd_bim = _block_diag(bb_im.transpose(1, 0, 2), gs)
    bd_cre = _block_diag(p["s5_c_re"][e].transpose(0, 2, 1), gs)
    bd_cim = _block_diag(p["s5_c_im"][e].transpose(0, 2, 1), gs)
    bd_gw = _block_diag(p["s5_glu_w"][e], gs)
    bp = -(-b // SUBLANES) * SUBLANES
    pad_b = lambda t: jnp.pad(t, ((0, bp - b),) + ((0, 0),) * (t.ndim - 1))
    u_t = jnp.pad(u.transpose(1, 0, 2), ((0, 0), (0, bp - b), (0, 0)))
    steps = _pick_tile(l, 256, 1)
    slab = lambda r, c: pl.BlockSpec((1, r, c), lambda s, t: (s, 0, 0))
    st_spec = pl.BlockSpec((bp, sn), lambda s, t: (0, s))
    out, hr, hi = pl.pallas_call(
        functools.partial(_s5_kernel, steps=steps),
        grid=(ns, l // steps),
        in_specs=[pl.BlockSpec((steps, bp, LANES), lambda s, t: (t, 0, s)),
                  slab(LANES, sn), slab(LANES, sn), slab(sn, LANES), slab(sn, LANES),
                  slab(1, LANES), slab(LANES, LANES), slab(1, LANES), slab(1, sn), slab(1, sn),
                  st_spec, st_spec],
        out_specs=[pl.BlockSpec((steps, bp, LANES), lambda s, t: (t, 0, s)), st_spec, st_spec],
        out_shape=[jax.ShapeDtypeStruct((l, bp, width), F32),
                   jax.ShapeDtypeStruct((bp, g * n), F32), jax.ShapeDtypeStruct((bp, g * n), F32)],
        scratch_shapes=[pltpu.VMEM((steps, bp, sn), F32), pltpu.VMEM((steps, bp, sn), F32)],
        compiler_params=_params(("arbitrary", "arbitrary"), big=True),
        name="s5",
    )(u_t, bd_bre, bd_bim, bd_cre, bd_cim, p["s5_d"][e].reshape(ns, 1, LANES), bd_gw,
      p["s5_glu_b"][e].reshape(ns, 1, LANES), ab_re.reshape(ns, 1, sn), ab_im.reshape(ns, 1, sn),
      pad_b(h_re.astype(F32).reshape(b, g * n)), pad_b(h_im.astype(F32).reshape(b, g * n)))
    out = out[:, :b].transpose(1, 0, 2)
    return (out, hr[:b].reshape(b, g, n).astype(h_re.dtype), hi[:b].reshape(b, g, n).astype(h_im.dtype))


def _gdn_gate_kernel(z_ref, alog_ref, dtb_ref, gc_o, beta_o):
    z = z_ref[0]
    c = z.shape[0]
    g = -jnp.exp(alog_ref[...]) * _softplus(z + dtb_ref[...])
    gc_o[0] = _mm01(_tril(c).astype(F32), g)
    beta_o[0] = _sigmoid(z)


def _gdn_gates(zab, a_log, dt_bias, chunk):
    b, l, w = zab.shape
    h = a_log.shape[0]
    pad = lambda t: jnp.pad(t, (0, w - h)).reshape(1, w)
    spec = pl.BlockSpec((1, chunk, w), lambda i, t: (i, t, 0))
    row = pl.BlockSpec((1, w), lambda i, t: (0, 0))
    return pl.pallas_call(
        _gdn_gate_kernel,
        grid=(b, l // chunk),
        in_specs=[spec, row, row],
        out_specs=[spec, spec],
        out_shape=[jax.ShapeDtypeStruct((b, l, w), F32)] * 2,
        compiler_params=_params(("arbitrary", "arbitrary")),
        name="gdn_gates",
    )(zab, pad(a_log), pad(dt_bias))


def _gdn_kernel(q_ref, k_ref, v_ref, y_ref, wq_ref, wk_ref, wv_ref, bq_ref, bk_ref, bv_ref,
                gcol_ref, bcol_ref, grow_ref, nw_ref, s0_ref, o_ref, s_ref, cq_scr, ck_scr, cv_scr,
                *, hb, dk, dv, kconv):
    t = pl.program_id(2)

    @pl.when(t == 0)
    def _():
        s_ref[0] = s0_ref[0]
        cq_scr[...] = bq_ref[0]
        ck_scr[...] = bk_ref[0]
        cv_scr[...] = bv_ref[0]

    c = q_ref.shape[1]

    def conv(x_ref, w_ref, carry_scr):
        x = x_ref[0]
        full = jnp.concatenate([carry_scr[...], x], axis=0)
        carry_scr[...] = x_ref[0, pl.ds(c - SUBLANES, SUBLANES), :]
        acc = None
        for j in range(kconv):
            sh = kconv - 1 - j
            src = full if sh == 0 else pltpu.roll(full, sh, 0)
            term = src[SUBLANES:SUBLANES + c] * w_ref[pl.ds(j, 1), :]
            acc = term if acc is None else acc + term
        return _silu(acc)

    qa = conv(q_ref, wq_ref, cq_scr)
    ka = conv(k_ref, wk_ref, ck_scr)
    va = conv(v_ref, wv_ref, cv_scr)
    incl = _tril(c)
    strict = _tril(c, strict=True)
    hs = range(hb)
    qs, ks, vs = [], [], []
    for hl in hs:
        q = qa[:, hl * dk:(hl + 1) * dk]
        k = ka[:, hl * dk:(hl + 1) * dk]
        qs.append(q * lax.rsqrt(jnp.sum(q * q, axis=-1, keepdims=True) + L2_EPS) * (dk ** -0.5))
        ks.append(k * lax.rsqrt(jnp.sum(k * k, axis=-1, keepdims=True) + L2_EPS))
        vs.append(va[:, hl * dv:(hl + 1) * dv])
    gc_c = [gcol_ref[0, 0, :, hl:hl + 1] for hl in hs]
    beta_c = [bcol_ref[0, 0, :, hl:hl + 1] for hl in hs]
    gc_r = [grow_ref[0, 0, 0, hl:hl + 1, :] for hl in hs]
    g_last = [g[:, c - 1:c] for g in gc_r]
    kbs = [k * b for k, b in zip(ks, beta_c)]
    dmask = [jnp.where(incl, jnp.exp(jnp.where(incl, gc - gr, 0.0)), 0.0) for gc, gr in zip(gc_c, gc_r)]
    kk = [_mm_nt(kb, k) for kb, k in zip(kbs, ks)]
    qk = [_mm_nt(q, k) for q, k in zip(qs, ks)]
    ms = [jnp.where(strict, a * d, 0.0) for a, d in zip(kk, dmask)]
    egc = [jnp.exp(g) for g in gc_c]
    rhss = [jnp.concatenate([v * b, kb * e], axis=1) for v, b, kb, e in zip(vs, beta_c, kbs, egc)]
    sols = _unit_lower_solve(ms, rhss)
    attn = [a * d for a, d in zip(qk, dmask)]
    s_old = [s_ref[0, hl] for hl in hs]
    ws = [_mm(sol[:, dv:], s) for sol, s in zip(sols, s_old)]
    qss = [_mm(q * e, s) for q, e, s in zip(qs, egc, s_old)]
    v_new = [sol[:, :dv] - w for sol, w in zip(sols, ws)]
    av = [_mm(a, vn) for a, vn in zip(attn, v_new)]
    kdec = [k * jnp.exp(gl - gc) for k, gl, gc in zip(ks, g_last, gc_c)]
    kv = [_mm_tn(kd, vn) for kd, vn in zip(kdec, v_new)]
    states = [s * jnp.exp(gl) + x for s, gl, x in zip(s_old, g_last, kv)]
    outs = [_head_rmsnorm(a + b, nw_ref[...]) * _silu(y_ref[0, :, hl * dv:(hl + 1) * dv])
            for hl, a, b in zip(hs, qss, av)]
    o_ref[0] = (outs[0] if hb == 1 else jnp.concatenate(outs, axis=1)).astype(o_ref.dtype)
    s_ref[0] = jnp.stack(states, axis=0)


def _gdn(zmain, zab, conv_buf, s0, p, o, chunk):
    b, l, _ = zmain.shape
    heads = p["gdn_a_log"].shape[1]
    dv = p["gdn_norm_w"].shape[1]
    kconv, conv_ch = p["gdn_conv_w"].shape[1:]
    vw = heads * dv
    kw = (conv_ch - vw) // 2
    dk = kw // heads
    hb = _pick_tile(heads, GDN_HEADS_PER_STEP, 1)
    nhb = heads // hb
    nc = l // chunk
    gc, beta = _gdn_gates(zab, p["gdn_a_log"][o], p["gdn_dt_bias"][o], chunk)
    gcol = gc[..., :heads].reshape(b, l, nhb, hb).transpose(0, 2, 1, 3)
    bcol = beta[..., heads:2 * heads].reshape(b, l, nhb, hb).transpose(0, 2, 1, 3)
    grow = gc[..., :heads].reshape(b, nc, chunk, nhb, hb).transpose(0, 3, 1, 4, 2)
    cbuf = jnp.pad(conv_buf.astype(F32), ((0, 0), (SUBLANES - (kconv - 1), 0), (0, 0)))
    wq, wk = hb * dk, hb * dk
    wv = hb * dv
    qoff, koff, voff, yoff = 0, kw // wk, 2 * kw // wv, (2 * kw + vw) // wv
    col = lambda w_, off: pl.BlockSpec((1, chunk, w_), lambda i, h, t: (i, t, off + h))
    cw = lambda w_, off: pl.BlockSpec((None, kconv, w_), lambda i, h, t: (o, 0, off + h))
    cb = lambda w_, off: pl.BlockSpec((1, SUBLANES, w_), lambda i, h, t: (i, 0, off + h))
    st = pl.BlockSpec((1, hb, dk, dv), lambda i, h, t: (i, h, 0, 0))
    out, s_fin = pl.pallas_call(
        functools.partial(_gdn_kernel, hb=hb, dk=dk, dv=dv, kconv=kconv),
        grid=(b, nhb, nc),
        in_specs=[col(wq, qoff), col(wk, koff), col(wv, voff), col(wv, yoff),
                  cw(wq, qoff), cw(wk, koff), cw(wv, voff),
                  cb(wq, qoff), cb(wk, koff), cb(wv, voff),
                  pl.BlockSpec((1, 1, chunk, hb), lambda i, h, t: (i, h, t, 0)),
                  pl.BlockSpec((1, 1, chunk, hb), lambda i, h, t: (i, h, t, 0)),
                  pl.BlockSpec((1, 1, 1, hb, chunk), lambda i, h, t: (i, h, t, 0, 0)),
                  pl.BlockSpec((1, dv), lambda i, h, t: (0, 0)),
                  st],
        out_specs=[pl.BlockSpec((1, chunk, wv), lambda i, h, t: (i, t, h)), st],
        out_shape=[jax.ShapeDtypeStruct((b, l, vw), BF16), jax.ShapeDtypeStruct((b, heads, dk, dv), F32)],
        scratch_shapes=[pltpu.VMEM((SUBLANES, wq), F32), pltpu.VMEM((SUBLANES, wk), F32),
                        pltpu.VMEM((SUBLANES, wv), F32)],
        compiler_params=_params(("arbitrary", "arbitrary", "arbitrary")),
        name="gdn",
    )(zmain, zmain, zmain, zmain, p["gdn_conv_w"], p["gdn_conv_w"], p["gdn_conv_w"],
      cbuf, cbuf, cbuf, gcol, bcol, grow, p["gdn_norm_w"][o:o + 1], s0.astype(F32))
    tail = jnp.concatenate([conv_buf.astype(F32), zmain[:, -(kconv - 1):, :conv_ch]], axis=1)[:, -(kconv - 1):]
    return out, tail.astype(conv_buf.dtype), s_fin.astype(s0.dtype)


def _hgrn_kernel(q_ref, f_ref, i_ref, og_ref, lbl_ref, nw_ref, s0_ref, o_ref, s_ref, *, hb, dk, dv, layer):
    @pl.when(pl.program_id(2) == 0)
    def _():
        s_ref[0] = s0_ref[0]

    c = q_ref.shape[1]
    logits = lbl_ref[...]
    mx = jnp.max(logits, axis=0, keepdims=True)
    ex = jnp.exp(logits - mx)
    den = jnp.sum(ex, axis=0, keepdims=True)
    lb_all = jnp.zeros_like(den)
    for r in range(1, layer + 1):
        lb_all = lb_all + ex[r:r + 1] / den
    tril_f = _tril(c).astype(F32)
    nb = max(c // SUB, 1)
    sb = min(SUB, c)
    rowi = _iota2((sb, 1), 0)
    eye = _iota2((dk, dk), 0) == _iota2((dk, dk), 1)
    hs = range(hb)
    ksl = [slice(hl * dk, (hl + 1) * dk) for hl in hs]
    vsl = [slice(hl * dv, (hl + 1) * dv) for hl in hs]
    q = q_ref[0]
    f = lb_all + (1.0 - lb_all) * _sigmoid(f_ref[0])
    k = 1.0 - f
    v = i_ref[0]
    bcum = _mm01(tril_f, jnp.log(f))
    qe = q * jnp.exp(bcum)
    s_old = [s_ref[0, hl] for hl in hs]
    inter = [_mm(qe[:, ksl[hl]], s_old[hl]) for hl in hs]
    blocks = [[] for _ in hs]
    for bi in range(nb):
        r0 = bi * sb
        qi, ki, vi, bb = q[r0:r0 + sb], k[r0:r0 + sb], v[r0:r0 + sb], bcum[r0:r0 + sb]
        acc = [inter[hl][r0:r0 + sb] for hl in hs]
        for si in range(sb):
            msk = rowi >= si
            dec = jnp.where(msk, jnp.exp(jnp.where(msk, bb - bb[si:si + 1], 0.0)), 0.0)
            prod = qi * ki[si:si + 1] * dec
            for hl in hs:
                colv = jnp.sum(prod[:, ksl[hl]], axis=-1, keepdims=True)
                acc[hl] = acc[hl] + colv * vi[si:si + 1, vsl[hl]]
        if bi > 0:
            bref = bcum[r0 - 1:r0]
            qsc = qi * jnp.exp(bb - bref)
            ksc = k[:r0] * jnp.exp(bref - bcum[:r0])
            sc = [_mm3(qsc[:, ksl[hl]], ksc[:, ksl[hl]], ((1,), (1,))) for hl in hs]
            acc = [acc[hl] + _mm(sc[hl], v[:r0, vsl[hl]]) for hl in hs]
        for hl in hs:
            blocks[hl].append(acc[hl])
    b_last = bcum[c - 1:c]
    kd = k * jnp.exp(b_last - bcum)
    e_last = jnp.exp(b_last)
    kv = [_mm_tn(kd[:, ksl[hl]], v[:, vsl[hl]]) for hl in hs]
    states = []
    outs = []
    og = og_ref[0]
    for hl in hs:
        e_col = jnp.sum(jnp.where(eye, e_last[:, ksl[hl]], 0.0), axis=1, keepdims=True)
        states.append(e_col * s_old[hl] + kv[hl])
        o = blocks[hl][0] if nb == 1 else jnp.concatenate(blocks[hl], axis=0)
        outs.append(_head_rmsnorm(o, nw_ref[...]) * _sigmoid(og[:, vsl[hl]]))
    o_ref[0] = (outs[0] if hb == 1 else jnp.concatenate(outs, axis=1)).astype(o_ref.dtype)
    s_ref[0] = jnp.stack(states, axis=0)


def _hgrn2(z, s0, p, o, layer, chunk):
    b, l, zw = z.shape
    dv = p["hg_norm_w"].shape[1]
    depth, kw = p["hg_lb_logits"].shape
    vw = (zw - 2 * kw) // 2
    heads = vw // dv
    dk = kw // heads
    hb = _pick_tile(heads, HGRN_HEADS_PER_STEP, 1)
    nhb = heads // hb
    wk, wv = hb * dk, hb * dv
    col = lambda w_, off: pl.BlockSpec((1, chunk, w_), lambda i, h, t: (i, t, off + h))
    st = pl.BlockSpec((1, hb, dk, dv), lambda i, h, t: (i, h, 0, 0))
    out, s_fin = pl.pallas_call(
        functools.partial(_hgrn_kernel, hb=hb, dk=dk, dv=dv, layer=layer),
        grid=(b, nhb, l // chunk),
        in_specs=[col(wk, 0), col(wk, kw // wk), col(wv, 2 * kw // wv), col(wv, (2 * kw + vw) // wv),
                  pl.BlockSpec((depth, wk), lambda i, h, t: (0, h)),
                  pl.BlockSpec((1, dv), lambda i, h, t: (0, 0)),
                  st],
        out_specs=[pl.BlockSpec((1, chunk, wv), lambda i, h, t: (i, t, h)), st],
        out_shape=[jax.ShapeDtypeStruct((b, l, vw), BF16), jax.ShapeDtypeStruct((b, heads, dk, dv), F32)],
        compiler_params=_params(("arbitrary", "arbitrary", "arbitrary")),
        name="hgrn2",
    )(z, z, z, z, p["hg_lb_logits"].astype(F32), p["hg_norm_w"][o:o + 1], s0.astype(F32))
    return out, s_fin.astype(s0.dtype)


def _trunk(x, mod_all, st, p, prep):
    rw_s, rw_sh, s5_re, s5_im, gdn_s, gdn_cv, hg_s = st
    b, l, d = x.shape
    depth = p["w_mod"].shape[0]
    d_ff = p["w_ffn_out"].shape[2]
    m = b * l
    rpg = l if l % SUBLANES == 0 and l >= 256 else None
    n_rw, n_sh, n_re, n_im, n_gdn, n_cv, n_hg = [], [], [], [], [], [], []
    chunk = min(CHUNK, l)

    def gate_of(mod9, idx):
        g = mod9[:, 3 * idx + 2]
        if rpg is not None:
            return g.reshape(b, 1, d)
        return jnp.repeat(g, l, axis=0).reshape(1, m, d)

    def ffn(x, mod9, lyr, idx, slot):
        h = _norm(x, p["norm_g"][lyr, idx], mod9, idx).reshape(m, d)
        tn = _pick_tile(d_ff, 256)
        act = _matmul(h, p["w_ffn_in"], (lyr, slot), ((0, d_ff // tn), d_ff // tn), tn,
                      epi="swiglu", out_dtype=BF16)
        tn2 = _pick_tile(d, 512)
        y = _matmul(act, p["w_ffn_out"], (lyr, slot), ((0,), d // tn2), tn2, epi="res",
                    res=x.reshape(m, d), gate=gate_of(mod9, idx), rows_per_gate=rpg, coef=0.5,
                    w_buffers=1)
        return y.reshape(b, l, d)

    for lyr in range(depth):
        mod9 = mod_all[lyr]
        x = ffn(x, mod9, lyr, 0, 0)
        h = _norm(x, p["norm_g"][lyr, 1], mod9, 1).reshape(m, d)
        if lyr % 2 == 0:
            e = lyr // 2
            even_in = p["w_in_even"].shape[2]
            rw_in = p["rw_mu"].shape[1]
            tn = _pick_tile(even_in, 1280, 256)
            zin = _matmul(h, p["w_in_even"], (e,), ((0,), even_in // tn), tn, w_buffers=1).reshape(b, l, even_in)
            oa, sh, s = _rwkv7(zin, rw_in, rw_sh[e], rw_s[e], p, e)
            ob, hr, hi = _s5(zin[..., rw_in:], s5_re[e], s5_im[e], p, e)
            n_rw.append(s)
            n_sh.append(sh)
            n_re.append(hr)
            n_im.append(hi)
            mix = jnp.concatenate([oa, ob.astype(BF16)], axis=-1).reshape(m, d)
            w_out, widx = p["w_out_even"], (e,)
        else:
            o = lyr // 2
            w_hg, w_ab, gdn_main = prep["odd"][o]
            tn = _pick_tile(gdn_main, 512)
            zmain = _matmul(h, p["w_in_odd"], (o,), ((0,), gdn_main // tn), tn).reshape(b, l, gdn_main)
            zab = _matmul(h, w_ab, (), ((0,), 1), LANES).reshape(b, l, LANES)
            hg_in = w_hg.shape[1]
            tn = _pick_tile(hg_in, 512)
            zhg = _matmul(h, w_hg, (), ((0,), hg_in // tn), tn).reshape(b, l, hg_in)
            oc, cv, s = _gdn(zmain, zab, gdn_cv[o], gdn_s[o], p, o, chunk)
            od, sh_ = _hgrn2(zhg, hg_s[o], p, o, lyr, chunk)
            n_gdn.append(s)
            n_cv.append(cv)
            n_hg.append(sh_)
            mix = jnp.concatenate([oc, od], axis=-1).reshape(m, d)
            w_out, widx = p["w_out_odd"], (o,)
        tn = _pick_tile(d, 512)
        x = _matmul(mix, w_out, widx, ((0,), d // tn), tn, epi="res", res=x.reshape(m, d),
                    gate=gate_of(mod9, 1), rows_per_gate=rpg, coef=1.0).reshape(b, l, d)
        x = ffn(x, mod9, lyr, 2, 1)
    y = _norm(x, p["norm_final"], out_dtype=x.dtype)
    return y, (jnp.stack(n_rw), jnp.stack(n_sh), jnp.stack(n_re), jnp.stack(n_im),
               jnp.stack(n_gdn), jnp.stack(n_cv), jnp.stack(n_hg))


def kernel(x_prompt, x_sample, c_prompt, c_sample, state_rwkv, state_rwkv_shift, state_s5_re, state_s5_im, state_gdn, cache_gdn_conv, state_hgrn, w_mod, b_mod, norm_g, norm_final, w_ffn_in, w_ffn_out, w_in_even, w_out_even, rw_mu, rw_w0, rw_w_up, rw_a0, rw_a_up, rw_g_up, rw_k_k, rw_k_a, rw_r_k, rw_ln_w, rw_ln_b, s5_lambda_re, s5_lambda_im, s5_log_dt, s5_b_re, s5_b_im, s5_c_re, s5_c_im, s5_d, s5_glu_w, s5_glu_b, w_in_odd, w_out_odd, gdn_conv_w, gdn_a_log, gdn_dt_bias, gdn_norm_w, hg_lb_logits, hg_norm_w):
    p = dict(w_mod=w_mod, b_mod=b_mod, norm_g=norm_g, norm_final=norm_final,
             w_ffn_in=w_ffn_in, w_ffn_out=w_ffn_out, w_in_even=w_in_even, w_out_even=w_out_even,
             rw_mu=rw_mu, rw_w0=rw_w0, rw_w_up=rw_w_up, rw_a0=rw_a0, rw_a_up=rw_a_up, rw_g_up=rw_g_up,
             rw_k_k=rw_k_k, rw_k_a=rw_k_a, rw_r_k=rw_r_k, rw_ln_w=rw_ln_w, rw_ln_b=rw_ln_b,
             s5_lambda_re=s5_lambda_re, s5_lambda_im=s5_lambda_im, s5_log_dt=s5_log_dt,
             s5_b_re=s5_b_re, s5_b_im=s5_b_im, s5_c_re=s5_c_re, s5_c_im=s5_c_im, s5_d=s5_d,
             s5_glu_w=s5_glu_w, s5_glu_b=s5_glu_b, w_in_odd=w_in_odd, w_out_odd=w_out_odd,
             gdn_conv_w=gdn_conv_w, gdn_a_log=gdn_a_log, gdn_dt_bias=gdn_dt_bias, gdn_norm_w=gdn_norm_w,
             hg_lb_logits=hg_lb_logits, hg_norm_w=hg_norm_w)
    bp, bs = x_prompt.shape[0], x_sample.shape[0]
    d = x_prompt.shape[-1]
    depth = w_mod.shape[0]
    n_even, n_odd = (depth + 1) // 2, depth // 2
    dtp = x_prompt.dtype

    bc = bp + bs
    bc_pad = -(-bc // 16) * 16
    c_all = jnp.pad(jnp.concatenate([c_prompt, c_sample], axis=0), ((0, bc_pad - bc), (0, 0)))
    mod = _modulation(c_all, w_mod, b_mod).reshape(depth, bc_pad, 9, d)
    mod_p, mod_s = mod[:, :bp], mod[:, bp:bc]

    heads = gdn_a_log.shape[1]
    conv_ch = gdn_conv_w.shape[2]
    vw = heads * gdn_norm_w.shape[1]
    gdn_main = conv_ch + vw
    gdn_in = gdn_main + 2 * heads
    prep = {"odd": []}
    for o in range(n_odd):
        w_hg = w_in_odd[o][:, gdn_in:]
        w_ab = jnp.pad(w_in_odd[o][:, gdn_main:gdn_in], ((0, 0), (0, LANES - 2 * heads)))
        prep["odd"].append((w_hg, w_ab, gdn_main))

    rw_h, rw_n = rw_r_k.shape[1], rw_r_k.shape[2]
    rw_in = rw_mu.shape[1]
    s5_g, s5_n = s5_lambda_re.shape[1], s5_lambda_re.shape[2]
    gdn_dk = (conv_ch - vw) // 2 // heads
    gdn_dv = gdn_norm_w.shape[1]
    kconv = gdn_conv_w.shape[1]
    hg_dv = hg_norm_w.shape[1]
    hg_kw = hg_lb_logits.shape[1]
    hg_vw = (w_in_odd.shape[2] - gdn_in - 2 * hg_kw) // 2
    hg_h = hg_vw // hg_dv
    hg_dk = hg_kw // hg_h
    st_prompt = (jnp.zeros((n_even, bp, rw_h, rw_n, rw_n), dtp),
                 jnp.zeros((n_even, bp, rw_in), dtp),
                 jnp.zeros((n_even, bp, s5_g, s5_n), dtp),
                 jnp.zeros((n_even, bp, s5_g, s5_n), dtp),
                 jnp.zeros((n_odd, bp, heads, gdn_dk, gdn_dv), dtp),
                 jnp.zeros((n_odd, bp, kconv - 1, conv_ch), dtp),
                 jnp.zeros((n_odd, bp, hg_h, hg_dk, hg_dv), dtp))
    st_sample = (state_rwkv, state_rwkv_shift, state_s5_re, state_s5_im, state_gdn, cache_gdn_conv, state_hgrn)
    y_prompt, ns_p = _trunk(x_prompt, mod_p, st_prompt, p, prep)
    y_sample, ns_s = _trunk(x_sample, mod_s, st_sample, p, prep)
    return (y_prompt, y_sample) + tuple(ns_p) + tuple(ns_s)
```

```python
import functools

import jax
import jax.numpy as jnp
from jax import lax
from jax.experimental import pallas as pl
from jax.experimental.pallas import tpu as pltpu

F32 = jnp.float32
BF16 = jnp.bfloat16
HI = lax.Precision.HIGHEST

EPS = 1e-6
RW_GN_EPS = 64e-5
L2_EPS = 1e-12
CHUNK = 64
SUB = 16
GDN_HEADS_PER_STEP = 16
HGRN_HEADS_PER_STEP = 16
LANES = 128
SUBLANES = 8
VMEM_LIMIT = 56 * 1024 * 1024


def _params(sem, big=False):
    return pltpu.CompilerParams(dimension_semantics=sem,
                                vmem_limit_bytes=VMEM_LIMIT if big else None)


def _pick_tile(n, pref, mult=LANES):
    t = (min(pref, n) // mult) * mult
    while t >= mult:
        if n % t == 0:
            return t
        t -= mult
    return n


def _mm(a, b, hi=False):
    if not hi:
        a, b = a.astype(BF16), b.astype(BF16)
    return lax.dot_general(a, b, (((1,), (0,)), ((), ())), precision=HI if hi else None,
                           preferred_element_type=F32)


def _mm_nt(a, b, hi=False):
    if not hi:
        a, b = a.astype(BF16), b.astype(BF16)
    return lax.dot_general(a, b, (((1,), (1,)), ((), ())), precision=HI if hi else None,
                           preferred_element_type=F32)


def _mm_tn(a, b, hi=False):
    if not hi:
        a, b = a.astype(BF16), b.astype(BF16)
    return lax.dot_general(a, b, (((0,), (0,)), ((), ())), precision=HI if hi else None,
                           preferred_element_type=F32)


def _split2(a):
    hi = a.astype(BF16)
    return hi, (a - hi.astype(F32)).astype(BF16)


def _split3(a):
    hi = a.astype(BF16)
    r = a - hi.astype(F32)
    mid = r.astype(BF16)
    return hi, mid, (r - mid.astype(F32)).astype(BF16)


def _bdot(a, b, dims):
    return lax.dot_general(a, b, (dims, ((), ())), preferred_element_type=F32)


def _mm3(a, b, dims=((1,), (0,))):
    ah, al = _split2(a)
    bh, bl = _split2(b)
    return _bdot(ah, bh, dims) + (_bdot(ah, bl, dims) + _bdot(al, bh, dims))


def _mm01(mask, x):
    m = mask.astype(BF16)
    xh, xm, xl = _split3(x)
    dims = ((1,), (0,))
    return _bdot(m, xh, dims) + (_bdot(m, xm, dims) + _bdot(m, xl, dims))


def _x01(x, mask):
    m = mask.astype(BF16)
    xh, xm, xl = _split3(x)
    dims = ((1,), (0,))
    return _bdot(xh, m, dims) + (_bdot(xm, m, dims) + _bdot(xl, m, dims))


def _sigmoid(x):
    return 1.0 / (1.0 + jnp.exp(-x))


def _silu(x):
    return x * _sigmoid(x)


def _softplus(x):
    return jnp.maximum(x, 0.0) + jnp.log(1.0 + jnp.exp(-jnp.abs(x)))


def _iota2(shape, axis):
    return lax.broadcasted_iota(jnp.int32, shape, axis)


def _mod_kernel(c_ref, w_ref, b_ref, o_ref):
    s = _silu(c_ref[...]).astype(BF16)
    o_ref[0] = jnp.dot(s, w_ref[0].astype(BF16), preferred_element_type=F32) + b_ref[0]


def _modulation(c, w_mod, b_mod):
    depth, d, n = w_mod.shape
    bc = c.shape[0]
    tn = _pick_tile(n, 512)
    return pl.pallas_call(
        _mod_kernel,
        grid=(depth, n // tn),
        in_specs=[pl.BlockSpec((bc, d), lambda l, j: (0, 0)),
                  pl.BlockSpec((1, d, tn), lambda l, j: (l, 0, j)),
                  pl.BlockSpec((1, 1, tn), lambda l, j: (l, 0, j))],
        out_specs=pl.BlockSpec((1, bc, tn), lambda l, j: (l, 0, j)),
        out_shape=jax.ShapeDtypeStruct((depth, bc, n), F32),
        compiler_params=_params(("arbitrary", "arbitrary"), big=True),
        name="modulation",
    )(c, w_mod, b_mod.reshape(depth, 1, n))


def _norm_kernel(x_ref, g_ref, *rest, idx):
    if idx is None:
        (o_ref,) = rest
    else:
        m_ref, o_ref = rest
    x = x_ref[0]
    y = x * lax.rsqrt(jnp.mean(x * x, axis=-1, keepdims=True) + EPS) * g_ref[...]
    if idx is not None:
        shift = m_ref[0, pl.ds(3 * idx, 1), :]
        scale = m_ref[0, pl.ds(3 * idx + 1, 1), :]
        y = y * (1.0 + scale) + shift
    o_ref[0] = y.astype(o_ref.dtype)


def _norm(x, g, mod9=None, idx=None, out_dtype=BF16):
    b, l, d = x.shape
    tl = _pick_tile(l, 256, SUBLANES)
    in_specs = [pl.BlockSpec((1, tl, d), lambda i, t: (i, t, 0)),
                pl.BlockSpec((1, d), lambda i, t: (0, 0))]
    args = [x, g.reshape(1, d)]
    if idx is not None:
        in_specs.append(pl.BlockSpec((1, mod9.shape[1], d), lambda i, t: (i, 0, 0)))
        args.append(mod9)
    return pl.pallas_call(
        functools.partial(_norm_kernel, idx=idx),
        grid=(b, l // tl),
        in_specs=in_specs,
        out_specs=pl.BlockSpec((1, tl, d), lambda i, t: (i, t, 0)),
        out_shape=jax.ShapeDtypeStruct((b, l, d), out_dtype),
        compiler_params=_params(("arbitrary", "arbitrary")),
        name="norm",
    )(*args)


def _mm_kernel(*refs, n_x, n_w, epi, coef):
    x_refs = refs[:n_x]
    w_refs = refs[n_x:n_x + n_w]
    pos = n_x + n_w
    if epi == "res":
        res_ref, gate_ref = refs[pos], refs[pos + 1]
        pos += 2
    o_ref = refs[pos]
    wb_refs = refs[pos + 1:pos + 1 + n_w]

    @pl.when(pl.program_id(1) == 0)
    def _():
        for w_ref, wb_ref in zip(w_refs, wb_refs):
            wb_ref[...] = w_ref[...].astype(BF16)

    acc = []
    for wb_ref in wb_refs:
        k0, a = 0, None
        for x_ref in x_refs:
            kx = x_ref.shape[1]
            part = jnp.dot(x_ref[...], wb_ref[pl.ds(k0, kx), :], preferred_element_type=F32)
            a = part if a is None else a + part
            k0 += kx
        acc.append(a)
    if epi == "swiglu":
        out = _silu(acc[0]) * acc[1]
    elif epi == "res":
        out = res_ref[...] + coef * gate_ref[0] * acc[0]
    else:
        out = acc[0]
    o_ref[...] = out.astype(o_ref.dtype)


def _matmul(x, w, prefix, col_blocks, tn, *, rows_per_gate=None, res=None, gate=None,
            coef=1.0, epi="plain", out_dtype=F32, tm_pref=512, w_buffers=2):
    xs = x if isinstance(x, (tuple, list)) else (x,)
    m = xs[0].shape[0]
    k = sum(t.shape[1] for t in xs)
    offs, n_cols = col_blocks
    n_w = len(offs)
    n_out = n_cols * tn
    if rows_per_gate is not None:
        tm = _pick_tile(rows_per_gate, tm_pref, SUBLANES)
    else:
        tm = _pick_tile(m, tm_pref, SUBLANES)
    npre = len(prefix)
    wmode = {} if w_buffers == 2 else {"pipeline_mode": pl.Buffered(w_buffers)}
    in_specs = [pl.BlockSpec((tm, t.shape[1]), lambda j, i: (i, 0)) for t in xs]
    args = list(xs)
    for off in offs:
        in_specs.append(pl.BlockSpec((None,) * npre + (k, tn),
                                     lambda j, i, off=off: tuple(prefix) + (0, j + off), **wmode))
        args.append(w)
    if epi == "res":
        in_specs.append(pl.BlockSpec((tm, tn), lambda j, i: (i, j)))
        args.append(res)
        if rows_per_gate is not None:
            rpt = rows_per_gate // tm
            in_specs.append(pl.BlockSpec((1, 1, tn), lambda j, i: (i // rpt, 0, j)))
        else:
            in_specs.append(pl.BlockSpec((1, tm, tn), lambda j, i: (0, i, j)))
        args.append(gate)
    return pl.pallas_call(
        functools.partial(_mm_kernel, n_x=len(xs), n_w=n_w, epi=epi, coef=coef),
        grid=(n_cols, m // tm),
        in_specs=in_specs,
        out_specs=pl.BlockSpec((tm, tn), lambda j, i: (i, j)),
        out_shape=jax.ShapeDtypeStruct((m, n_out), out_dtype),
        scratch_shapes=[pltpu.VMEM((k, tn), BF16) for _ in range(n_w)],
        compiler_params=_params(("arbitrary", "arbitrary"), big=True),
        name="matmul_" + epi,
    )(*args)


def _group_ones(n):
    r = _iota2((LANES, LANES), 0)
    c = _iota2((LANES, LANES), 1)
    sh = n.bit_length() - 1
    return (jnp.right_shift(r, sh) == jnp.right_shift(c, sh)).astype(F32)


def _group_sum(x, n):
    if n % LANES == 0:
        parts = []
        for h in range(x.shape[1] // n):
            s = jnp.sum(x[:, h * n:(h + 1) * n], axis=-1, keepdims=True)
            parts.append(jnp.broadcast_to(s, (x.shape[0], n)))
        return parts[0] if len(parts) == 1 else jnp.concatenate(parts, axis=1)
    ones = _group_ones(n)
    parts = [_x01(x[:, s * LANES:(s + 1) * LANES], ones) for s in range(x.shape[1] // LANES)]
    return parts[0] if len(parts) == 1 else jnp.concatenate(parts, axis=1)


def _tril(c, strict=False):
    r = _iota2((c, c), 0)
    s = _iota2((c, c), 1)
    return (r > s) if strict else (r >= s)


def _unit_lower_solve(lms, rhss):
    c = lms[0].shape[0]
    r = _iota2((c, c), 0)
    s = _iota2((c, c), 1)
    eye = (r == s).astype(F32)
    if c <= SUB:
        lds, los = lms, None
    else:
        sh = SUB.bit_length() - 1
        same = jnp.right_shift(r, sh) == jnp.right_shift(s, sh)
        lds = [jnp.where(same, lm, 0.0) for lm in lms]
        los = [lm - ld for lm, ld in zip(lms, lds)]
    order = min(SUB, c)
    xs = [-ld for ld in lds]
    ps = [eye + x for x in xs]
    pws = [_mm3(x, x) for x in xs]
    k = 2
    while k < order:
        new_ps = [p + _mm3(pw, p) for p, pw in zip(ps, pws)]
        if 2 * k < order:
            pws = [_mm3(pw, pw) for pw in pws]
        ps = new_ps
        k *= 2
    sols = [_mm3(p, rhs) for p, rhs in zip(ps, rhss)]
    if los is None:
        return sols
    pws = [-_mm3(p, lo) for p, lo in zip(ps, los)]
    k = 1
    while True:
        new_sols = [sol + _mm3(pw, sol) for sol, pw in zip(sols, pws)]
        if 2 * k >= c // SUB:
            return new_sols
        pws = [_mm3(pw, pw) for pw in pws]
        sols = new_sols
        k *= 2


def _head_rmsnorm(o, w):
    return o * lax.rsqrt(jnp.mean(o * o, axis=-1, keepdims=True) + EPS) * w


def _rw_pre_kernel(z_ref, sh_ref, mu_ref, w0_ref, a0_ref, kk_ref, ka_ref, rk_ref,
                   wup_ref, aup_ref, gup_ref,
                   r_o, d_o, k_o, v_o, kk_o, nkka_o, g_o, bonus_o, prev_scr,
                   *, width, w_lora, a_lora, head_dim):
    @pl.when(pl.program_id(1) == 0)
    def _():
        prev_scr[...] = sh_ref[0]

    z = z_ref[0]
    tl = z.shape[0]
    rolled = pltpu.roll(z, 1, 0)
    prev = jnp.where(_iota2(z.shape, 0) == 0, prev_scr[...], rolled)
    prev_scr[...] = z_ref[0, pl.ds(tl - 1, 1), :]
    zs = z + mu_ref[...] * (prev - z)
    c1, c2, c3 = width, 2 * width, 3 * width
    c4 = c3 + w_lora
    c5 = c4 + a_lora
    r, k, v = zs[:, :c1], zs[:, c1:c2], zs[:, c2:c3]
    xw, xa, xg = zs[:, c3:c4], zs[:, c4:c5], zs[:, c5:]
    w = -_softplus(-(w0_ref[...] + _mm(jnp.tanh(xw), wup_ref[...]))) - 0.5
    decay = jnp.exp(-jnp.exp(w))
    a = _sigmoid(a0_ref[...] + _mm(xa, aup_ref[...]))
    g = _mm(_sigmoid(xg), gup_ref[...])
    kraw = k * kk_ref[...]
    kk = kraw * lax.rsqrt(_group_sum(kraw * kraw, head_dim) + L2_EPS)
    k2 = k * (1.0 + (a - 1.0) * ka_ref[...])
    bonus = _group_sum(r * k2 * rk_ref[...], head_dim) * v
    r_o[0] = r
    d_o[0] = decay
    k_o[0] = k2
    v_o[0] = v
    kk_o[0] = kk
    nkka_o[0] = -(kk * a)
    g_o[0] = g
    bonus_o[0] = bonus


def _rw_pre(zin, rw_in, shift, p, e, head_dim):
    b, l, _ = zin.shape
    width = p["rw_w0"].shape[1]
    w_lora = p["rw_w_up"].shape[1]
    a_lora = p["rw_a_up"].shape[1]
    g_lora = p["rw_g_up"].shape[1]
    tl = _pick_tile(l, 128, SUBLANES)
    row = lambda n: pl.BlockSpec((1, n), lambda i, t: (0, 0))
    out_spec = pl.BlockSpec((1, tl, width), lambda i, t: (i, t, 0))
    out_shape = jax.ShapeDtypeStruct((b, l, width), F32)
    return pl.pallas_call(
        functools.partial(_rw_pre_kernel, width=width, w_lora=w_lora, a_lora=a_lora, head_dim=head_dim),
        grid=(b, l // tl),
        in_specs=[pl.BlockSpec((1, tl, rw_in), lambda i, t: (i, t, 0)),
                  pl.BlockSpec((1, 1, rw_in), lambda i, t: (i, 0, 0)),
                  row(rw_in), row(width), row(width), row(width), row(width), row(width),
                  pl.BlockSpec((None, w_lora, width), lambda i, t: (e, 0, 0)),
                  pl.BlockSpec((None, a_lora, width), lambda i, t: (e, 0, 0)),
                  pl.BlockSpec((None, g_lora, width), lambda i, t: (e, 0, 0))],
        out_specs=[out_spec] * 8,
        out_shape=[out_shape] * 8,
        scratch_shapes=[pltpu.VMEM((1, rw_in), F32)],
        compiler_params=_params(("arbitrary", "arbitrary"), big=True),
        name="rwkv_pre",
    )(zin, shift.reshape(b, 1, rw_in), p["rw_mu"][e:e + 1], p["rw_w0"][e:e + 1], p["rw_a0"][e:e + 1],
      p["rw_k_k"][e:e + 1], p["rw_k_a"][e:e + 1], p["rw_r_k"][e].reshape(1, width),
      p["rw_w_up"], p["rw_a_up"], p["rw_g_up"])


def _rw_scan_kernel(r_ref, d_ref, k_ref, v_ref, kk_ref, nkka_ref, s0_ref, o_ref, s_ref, *, steps, n):
    @pl.when(pl.program_id(1) == 0)
    def _():
        s_ref[...] = s0_ref[...]

    sa0 = s_ref[0] * kk_ref[0, pl.ds(0, 1), :]
    for j in range(1, n):
        sa0 = sa0 + s_ref[j] * kk_ref[0, pl.ds(j, 1), :]

    def step(t, sa):
        tn = jnp.minimum(t + 1, steps - 1)
        v_t = v_ref[t]
        o = None
        sa_next = None
        for j in range(n):
            sj = (s_ref[j] * d_ref[t, pl.ds(j, 1), :] + sa * nkka_ref[t, pl.ds(j, 1), :]
                  + v_t * k_ref[t, pl.ds(j, 1), :])
            s_ref[j] = sj
            term = sj * r_ref[t, pl.ds(j, 1), :]
            o = term if o is None else o + term
            nxt = sj * kk_ref[tn, pl.ds(j, 1), :]
            sa_next = nxt if sa_next is None else sa_next + nxt
        o_ref[t] = o
        return sa_next

    lax.fori_loop(0, steps, step, sa0)


def _rw_scan(seqs, s0):
    l, n, bh = seqs[0].shape
    lb = LANES if bh % LANES == 0 else bh
    steps = _pick_tile(l, 32, 1)
    seq_spec = pl.BlockSpec((steps, n, lb), lambda c, t: (t, 0, c))
    st_spec = pl.BlockSpec((n, n, lb), lambda c, t: (0, 0, c))
    return pl.pallas_call(
        functools.partial(_rw_scan_kernel, steps=steps, n=n),
        grid=(bh // lb, l // steps),
        in_specs=[seq_spec] * 6 + [st_spec],
        out_specs=[seq_spec, st_spec],
        out_shape=[jax.ShapeDtypeStruct((l, n, bh), F32), jax.ShapeDtypeStruct((n, n, bh), F32)],
        compiler_params=_params(("arbitrary", "arbitrary"), big=True),
        name="rwkv_scan",
    )(*seqs, s0)


def _rw_post_kernel(o_ref, bonus_ref, g_ref, lnw_ref, lnb_ref, out_ref, *, head_dim):
    o = o_ref[0]
    mu = _group_sum(o, head_dim) * (1.0 / head_dim)
    c = o - mu
    var = _group_sum(c * c, head_dim) * (1.0 / head_dim)
    y = c * lax.rsqrt(var + RW_GN_EPS) * lnw_ref[...] + lnb_ref[...]
    out_ref[0] = ((y + bonus_ref[0]) * g_ref[0]).astype(out_ref.dtype)


def _rw_post(o, bonus, g, p, e, head_dim):
    b, l, width = o.shape
    tl = _pick_tile(l, 256, SUBLANES)
    spec = pl.BlockSpec((1, tl, width), lambda i, t: (i, t, 0))
    row = pl.BlockSpec((1, width), lambda i, t: (0, 0))
    return pl.pallas_call(
        functools.partial(_rw_post_kernel, head_dim=head_dim),
        grid=(b, l // tl),
        in_specs=[spec, spec, spec, row, row],
        out_specs=spec,
        out_shape=jax.ShapeDtypeStruct((b, l, width), BF16),
        compiler_params=_params(("arbitrary", "arbitrary")),
        name="rwkv_post",
    )(o, bonus, g, p["rw_ln_w"][e:e + 1], p["rw_ln_b"][e:e + 1])


def _rwkv7(zin, rw_in, shift, s0, p, e):
    b, l, _ = zin.shape
    heads, n = p["rw_r_k"].shape[1], p["rw_r_k"].shape[2]
    r, d, k, v, kk, nkka, g, bonus = _rw_pre(zin, rw_in, shift, p, e, n)

    def to_lanes(t):
        return t.reshape(b, l, heads, n).transpose(1, 3, 0, 2).reshape(l, n, b * heads)

    s0_l = s0.astype(F32).transpose(3, 2, 0, 1).reshape(n, n, b * heads)
    o_l, s_l = _rw_scan([to_lanes(t) for t in (r, d, k, v, kk, nkka)], s0_l)
    o = o_l.reshape(l, n, b, heads).transpose(2, 0, 3, 1).reshape(b, l, heads * n)
    s_fin = s_l.reshape(n, n, b, heads).transpose(2, 3, 1, 0)
    out = _rw_post(o, bonus, g, p, e, n)
    return out, zin[:, -1, :rw_in].astype(shift.dtype), s_fin.astype(s0.dtype)


def _s5_param_kernel(ldt_ref, lr_ref, li_ref, bre_ref, bim_ref, are_o, aim_o, bbre_o, bbim_o):
    dt = jnp.exp(ldt_ref[...])
    lr, li = lr_ref[...], li_ref[...]
    mag = jnp.exp(lr * dt)
    ab_re, ab_im = mag * jnp.cos(li * dt), mag * jnp.sin(li * dt)
    den = lr * lr + li * li
    pr, pi_ = ab_re - 1.0, ab_im
    coef_re = (pr * lr + pi_ * li) / den
    coef_im = (pi_ * lr - pr * li) / den
    are_o[...] = ab_re
    aim_o[...] = ab_im
    for m in range(bre_ref.shape[0]):
        b_re, b_im = bre_ref[m], bim_ref[m]
        bbre_o[m] = coef_re * b_re - coef_im * b_im
        bbim_o[m] = coef_re * b_im + coef_im * b_re


def _s5_params(p, e):
    g, n, m = p["s5_b_re"].shape[1:]
    outs = pl.pallas_call(
        _s5_param_kernel,
        out_shape=[jax.ShapeDtypeStruct((g, n), F32)] * 2 + [jax.ShapeDtypeStruct((m, g, n), F32)] * 2,
        name="s5_params",
    )(p["s5_log_dt"][e].reshape(g, 1), p["s5_lambda_re"][e], p["s5_lambda_im"][e],
      p["s5_b_re"][e].transpose(2, 0, 1), p["s5_b_im"][e].transpose(2, 0, 1))
    return outs


def _block_diag(t, gs):
    g, a, b = t.shape
    t = t.reshape(g // gs, gs, a, b)
    eye = jnp.eye(gs, dtype=t.dtype)
    return jnp.einsum("sgab,gh->sgahb", t, eye).reshape(g // gs, gs * a, gs * b)


def _s5_kernel(u_ref, bre_ref, bim_ref, cre_ref, cim_ref, d_ref, gw_ref, gb_ref, are_ref, aim_ref,
               h0r_ref, h0i_ref, o_ref, hr_ref, hi_ref, u_scr, xr_scr, xi_scr, *, steps, pitch):
    @pl.when(pl.program_id(1) == 0)
    def _():
        hr_ref[...] = h0r_ref[...]
        hi_ref[...] = h0i_ref[...]

    nb = u_ref.shape[0]
    nk = xr_scr.shape[0]
    lanes = [slice(k * LANES, (k + 1) * LANES) for k in range(nk)]
    for b in range(nb):
        u_scr[pl.ds(b * pitch, steps), :] = u_ref[b]
        u_scr[pl.ds(b * pitch + steps, pitch - steps), :] = jnp.zeros((pitch - steps, LANES), F32)
    u2 = u_scr[...]
    bu_re = _mm(u2, bre_ref[0])
    bu_im = _mm(u2, bim_ref[0])
    for k in range(nk):
        xr_scr[k] = bu_re[:, lanes[k]]
        xi_scr[k] = bu_im[:, lanes[k]]
    a_re = [are_ref[0, :, lanes[k]] for k in range(nk)]
    a_im = [aim_ref[0, :, lanes[k]] for k in range(nk)]

    def step(t, carry):
        rows = pl.ds(t, nb, stride=pitch)
        new = []
        for k in range(nk):
            h_re, h_im = carry[2 * k], carry[2 * k + 1]
            n_re = a_re[k] * h_re - a_im[k] * h_im + xr_scr[k, rows, :]
            n_im = a_re[k] * h_im + a_im[k] * h_re + xi_scr[k, rows, :]
            xr_scr[k, rows, :] = n_re
            xi_scr[k, rows, :] = n_im
            new += [n_re, n_im]
        return tuple(new)

    init = tuple(r[:, lanes[k]] for k in range(nk) for r in (hr_ref, hi_ref))
    fin = lax.fori_loop(0, steps, step, init)
    for k in range(nk):
        hr_ref[:, lanes[k]] = fin[2 * k]
        hi_ref[:, lanes[k]] = fin[2 * k + 1]
    hs_re = jnp.concatenate([xr_scr[k] for k in range(nk)], axis=1) if nk > 1 else xr_scr[0]
    hs_im = jnp.concatenate([xi_scr[k] for k in range(nk)], axis=1) if nk > 1 else xi_scr[0]
    y = _mm(hs_re, cre_ref[0]) - _mm(hs_im, cim_ref[0]) + d_ref[0] * u2
    yg = 0.5 * y * (1.0 + jnp.tanh(0.7978845608028654 * (y + 0.044715 * (y * y * y))))
    out = yg * _sigmoid(_mm(yg, gw_ref[0]) + gb_ref[0])
    for b in range(nb):
        o_ref[b] = out[b * pitch:b * pitch + steps].astype(o_ref.dtype)


def _s5(zin, col0, h_re, h_im, p, e):
    b, l, _ = zin.shape
    g, n, m = p["s5_b_re"].shape[1:]
    width = g * m
    gs = LANES // m
    ns = g // gs
    sn = gs * n
    ab_re, ab_im, bb_re, bb_im = _s5_params(p, e)
    bd_bre = _block_diag(bb_re.transpose(1, 0, 2), gs)
    bd_bim = _block_diag(bb_im.transpose(1, 0, 2), gs)
    bd_cre = _block_diag(p["s5_c_re"][e].transpose(0, 2, 1), gs)
    bd_cim = _block_diag(p["s5_c_im"][e].transpose(0, 2, 1), gs)
    bd_gw = _block_diag(p["s5_glu_w"][e], gs)
    steps = _pick_tile(l, 256, 2 * SUBLANES)
    pitch = steps + SUBLANES
    cb0 = col0 // LANES
    slab = lambda r, c: pl.BlockSpec((1, r, c), lambda s, t: (s, 0, 0))
    st_spec = pl.BlockSpec((b, sn), lambda s, t: (0, s))
    out, hr, hi = pl.pallas_call(
        functools.partial(_s5_kernel, steps=steps, pitch=pitch),
        grid=(ns, l // steps),
        in_specs=[pl.BlockSpec((b, steps, LANES), lambda s, t: (0, t, cb0 + s)),
                  slab(LANES, sn), slab(LANES, sn), slab(sn, LANES), slab(sn, LANES),
                  slab(1, LANES), slab(LANES, LANES), slab(1, LANES), slab(1, sn), slab(1, sn),
                  st_spec, st_spec],
        out_specs=[pl.BlockSpec((b, steps, LANES), lambda s, t: (0, t, s)), st_spec, st_spec],
        out_shape=[jax.ShapeDtypeStruct((b, l, width), BF16),
                   jax.ShapeDtypeStruct((b, g * n), F32), jax.ShapeDtypeStruct((b, g * n), F32)],
        scratch_shapes=[pltpu.VMEM((b * pitch, LANES), F32),
                        pltpu.VMEM((sn // LANES, b * pitch, LANES), F32),
                        pltpu.VMEM((sn // LANES, b * pitch, LANES), F32)],
        compiler_params=_params(("arbitrary", "arbitrary"), big=True),
        name="s5",
    )(zin, bd_bre, bd_bim, bd_cre, bd_cim, p["s5_d"][e].reshape(ns, 1, LANES), bd_gw,
      p["s5_glu_b"][e].reshape(ns, 1, LANES), ab_re.reshape(ns, 1, sn), ab_im.reshape(ns, 1, sn),
      h_re.astype(F32).reshape(b, g * n), h_im.astype(F32).reshape(b, g * n))
    return (out, hr.reshape(b, g, n).astype(h_re.dtype), hi.reshape(b, g, n).astype(h_im.dtype))


def _gdn_gate_kernel(z_ref, alog_ref, dtb_ref, gc_o, beta_o):
    z = z_ref[0]
    c = z.shape[0]
    g = -jnp.exp(alog_ref[...]) * _softplus(z + dtb_ref[...])
    gc_o[0] = _mm01(_tril(c).astype(F32), g)
    beta_o[0] = _sigmoid(z)


def _gdn_gates(zab, a_log, dt_bias, chunk):
    b, l, w = zab.shape
    h = a_log.shape[0]
    pad = lambda t: jnp.pad(t, (0, w - h)).reshape(1, w)
    spec = pl.BlockSpec((1, chunk, w), lambda i, t: (i, t, 0))
    row = pl.BlockSpec((1, w), lambda i, t: (0, 0))
    return pl.pallas_call(
        _gdn_gate_kernel,
        grid=(b, l // chunk),
        in_specs=[spec, row, row],
        out_specs=[spec, spec],
        out_shape=[jax.ShapeDtypeStruct((b, l, w), F32)] * 2,
        compiler_params=_params(("arbitrary", "arbitrary")),
        name="gdn_gates",
    )(zab, pad(a_log), pad(dt_bias))


def _gdn_kernel(q_ref, k_ref, v_ref, y_ref, wq_ref, wk_ref, wv_ref, bq_ref, bk_ref, bv_ref,
                gcol_ref, bcol_ref, grow_ref, nw_ref, s0_ref, o_ref, s_ref, cq_scr, ck_scr, cv_scr,
                *, hb, dk, dv, kconv):
    t = pl.program_id(2)

    @pl.when(t == 0)
    def _():
        s_ref[0] = s0_ref[0]
        cq_scr[...] = bq_ref[0]
        ck_scr[...] = bk_ref[0]
        cv_scr[...] = bv_ref[0]

    c = q_ref.shape[1]

    def conv(x_ref, w_ref, carry_scr):
        x = x_ref[0]
        full = jnp.concatenate([carry_scr[...], x], axis=0)
        carry_scr[...] = x_ref[0, pl.ds(c - SUBLANES, SUBLANES), :]
        acc = None
        for j in range(kconv):
            sh = kconv - 1 - j
            src = full if sh == 0 else pltpu.roll(full, sh, 0)
            term = src[SUBLANES:SUBLANES + c] * w_ref[pl.ds(j, 1), :]
            acc = term if acc is None else acc + term
        return _silu(acc)

    qa = conv(q_ref, wq_ref, cq_scr)
    ka = conv(k_ref, wk_ref, ck_scr)
    va = conv(v_ref, wv_ref, cv_scr)
    incl = _tril(c)
    strict = _tril(c, strict=True)
    hs = range(hb)
    qs, ks, vs = [], [], []
    for hl in hs:
        q = qa[:, hl * dk:(hl + 1) * dk]
        k = ka[:, hl * dk:(hl + 1) * dk]
        qs.append(q * lax.rsqrt(jnp.sum(q * q, axis=-1, keepdims=True) + L2_EPS) * (dk ** -0.5))
        ks.append(k * lax.rsqrt(jnp.sum(k * k, axis=-1, keepdims=True) + L2_EPS))
        vs.append(va[:, hl * dv:(hl + 1) * dv])
    gc_c = [gcol_ref[0, 0, :, hl:hl + 1] for hl in hs]
    beta_c = [bcol_ref[0, 0, :, hl:hl + 1] for hl in hs]
    gc_r = [grow_ref[0, 0, 0, hl:hl + 1, :] for hl in hs]
    g_last = [g[:, c - 1:c] for g in gc_r]
    kbs = [k * b for k, b in zip(ks, beta_c)]
    dmask = [jnp.where(incl, jnp.exp(jnp.where(incl, gc - gr, 0.0)), 0.0) for gc, gr in zip(gc_c, gc_r)]
    kk = [_mm_nt(kb, k) for kb, k in zip(kbs, ks)]
    qk = [_mm_nt(q, k) for q, k in zip(qs, ks)]
    ms = [jnp.where(strict, a * d, 0.0) for a, d in zip(kk, dmask)]
    egc = [jnp.exp(g) for g in gc_c]
    rhss = [jnp.concatenate([v * b, kb * e], axis=1) for v, b, kb, e in zip(vs, beta_c, kbs, egc)]
    sols = _unit_lower_solve(ms, rhss)
    attn = [a * d for a, d in zip(qk, dmask)]
    s_old = [s_ref[0, hl] for hl in hs]
    ws = [_mm(sol[:, dv:], s) for sol, s in zip(sols, s_old)]
    qss = [_mm(q * e, s) for q, e, s in zip(qs, egc, s_old)]
    v_new = [sol[:, :dv] - w for sol, w in zip(sols, ws)]
    av = [_mm(a, vn) for a, vn in zip(attn, v_new)]
    kdec = [k * jnp.exp(gl - gc) for k, gl, gc in zip(ks, g_last, gc_c)]
    kv = [_mm_tn(kd, vn) for kd, vn in zip(kdec, v_new)]
    states = [s * jnp.exp(gl) + x for s, gl, x in zip(s_old, g_last, kv)]
    outs = [_head_rmsnorm(a + b, nw_ref[...]) * _silu(y_ref[0, :, hl * dv:(hl + 1) * dv])
            for hl, a, b in zip(hs, qss, av)]
    o_ref[0] = (outs[0] if hb == 1 else jnp.concatenate(outs, axis=1)).astype(o_ref.dtype)
    s_ref[0] = jnp.stack(states, axis=0)


def _gdn(zmain, zab, conv_buf, s0, p, o, chunk):
    b, l, _ = zmain.shape
    heads = p["gdn_a_log"].shape[1]
    dv = p["gdn_norm_w"].shape[1]
    kconv, conv_ch = p["gdn_conv_w"].shape[1:]
    vw = heads * dv
    kw = (conv_ch - vw) // 2
    dk = kw // heads
    hb = _pick_tile(heads, GDN_HEADS_PER_STEP, 1)
    nhb = heads // hb
    nc = l // chunk
    gc, beta = _gdn_gates(zab, p["gdn_a_log"][o], p["gdn_dt_bias"][o], chunk)
    gcol = gc[..., :heads].reshape(b, l, nhb, hb).transpose(0, 2, 1, 3)
    bcol = beta[..., heads:2 * heads].reshape(b, l, nhb, hb).transpose(0, 2, 1, 3)
    grow = gc[..., :heads].reshape(b, nc, chunk, nhb, hb).transpose(0, 3, 1, 4, 2)
    cbuf = jnp.pad(conv_buf.astype(F32), ((0, 0), (SUBLANES - (kconv - 1), 0), (0, 0)))
    wq, wk = hb * dk, hb * dk
    wv = hb * dv
    qoff, koff, voff, yoff = 0, kw // wk, 2 * kw // wv, (2 * kw + vw) // wv
    col = lambda w_, off: pl.BlockSpec((1, chunk, w_), lambda i, h, t: (i, t, off + h))
    cw = lambda w_, off: pl.BlockSpec((None, kconv, w_), lambda i, h, t: (o, 0, off + h))
    cb = lambda w_, off: pl.BlockSpec((1, SUBLANES, w_), lambda i, h, t: (i, 0, off + h))
    st = pl.BlockSpec((1, hb, dk, dv), lambda i, h, t: (i, h, 0, 0))
    out, s_fin = pl.pallas_call(
        functools.partial(_gdn_kernel, hb=hb, dk=dk, dv=dv, kconv=kconv),
        grid=(b, nhb, nc),
        in_specs=[col(wq, qoff), col(wk, koff), col(wv, voff), col(wv, yoff),
                  cw(wq, qoff), cw(wk, koff), cw(wv, voff),
                  cb(wq, qoff), cb(wk, koff), cb(wv, voff),
                  pl.BlockSpec((1, 1, chunk, hb), lambda i, h, t: (i, h, t, 0)),
                  pl.BlockSpec((1, 1, chunk, hb), lambda i, h, t: (i, h, t, 0)),
                  pl.BlockSpec((1, 1, 1, hb, chunk), lambda i, h, t: (i, h, t, 0, 0)),
                  pl.BlockSpec((1, dv), lambda i, h, t: (0, 0)),
                  st],
        out_specs=[pl.BlockSpec((1, chunk, wv), lambda i, h, t: (i, t, h)), st],
        out_shape=[jax.ShapeDtypeStruct((b, l, vw), BF16), jax.ShapeDtypeStruct((b, heads, dk, dv), F32)],
        scratch_shapes=[pltpu.VMEM((SUBLANES, wq), F32), pltpu.VMEM((SUBLANES, wk), F32),
                        pltpu.VMEM((SUBLANES, wv), F32)],
        compiler_params=_params(("arbitrary", "arbitrary", "arbitrary")),
        name="gdn",
    )(zmain, zmain, zmain, zmain, p["gdn_conv_w"], p["gdn_conv_w"], p["gdn_conv_w"],
      cbuf, cbuf, cbuf, gcol, bcol, grow, p["gdn_norm_w"][o:o + 1], s0.astype(F32))
    tail = jnp.concatenate([conv_buf.astype(F32), zmain[:, -(kconv - 1):, :conv_ch]], axis=1)[:, -(kconv - 1):]
    return out, tail.astype(conv_buf.dtype), s_fin.astype(s0.dtype)


def _hgrn_kernel(q_ref, f_ref, i_ref, og_ref, lbl_ref, nw_ref, s0_ref, o_ref, s_ref, *, hb, dk, dv, layer):
    @pl.when(pl.program_id(2) == 0)
    def _():
        s_ref[0] = s0_ref[0]

    c = q_ref.shape[1]
    logits = lbl_ref[...]
    mx = jnp.max(logits, axis=0, keepdims=True)
    ex = jnp.exp(logits - mx)
    den = jnp.sum(ex, axis=0, keepdims=True)
    lb_all = jnp.zeros_like(den)
    for r in range(1, layer + 1):
        lb_all = lb_all + ex[r:r + 1] / den
    tril_f = _tril(c).astype(F32)
    nb = max(c // SUB, 1)
    sb = min(SUB, c)
    rowi = _iota2((sb, 1), 0)
    eye = _iota2((dk, dk), 0) == _iota2((dk, dk), 1)
    hs = range(hb)
    ksl = [slice(hl * dk, (hl + 1) * dk) for hl in hs]
    vsl = [slice(hl * dv, (hl + 1) * dv) for hl in hs]
    q = q_ref[0]
    f = lb_all + (1.0 - lb_all) * _sigmoid(f_ref[0])
    k = 1.0 - f
    v = i_ref[0]
    bcum = _mm01(tril_f, jnp.log(f))
    qe = q * jnp.exp(bcum)
    s_old = [s_ref[0, hl] for hl in hs]
    inter = [_mm(qe[:, ksl[hl]], s_old[hl]) for hl in hs]
    blocks = [[] for _ in hs]
    for bi in range(nb):
        r0 = bi * sb
        qi, ki, vi, bb = q[r0:r0 + sb], k[r0:r0 + sb], v[r0:r0 + sb], bcum[r0:r0 + sb]
        acc = [inter[hl][r0:r0 + sb] for hl in hs]
        for si in range(sb):
            msk = rowi >= si
            dec = jnp.where(msk, jnp.exp(jnp.where(msk, bb - bb[si:si + 1], 0.0)), 0.0)
            prod = qi * ki[si:si + 1] * dec
            for hl in hs:
                colv = jnp.sum(prod[:, ksl[hl]], axis=-1, keepdims=True)
                acc[hl] = acc[hl] + colv * vi[si:si + 1, vsl[hl]]
        if bi > 0:
            bref = bcum[r0 - 1:r0]
            qsc = qi * jnp.exp(bb - bref)
            ksc = k[:r0] * jnp.exp(bref - bcum[:r0])
            sc = [_mm3(qsc[:, ksl[hl]], ksc[:, ksl[hl]], ((1,), (1,))) for hl in hs]
            acc = [acc[hl] + _mm(sc[hl], v[:r0, vsl[hl]]) for hl in hs]
        for hl in hs:
            blocks[hl].append(acc[hl])
    b_last = bcum[c - 1:c]
    kd = k * jnp.exp(b_last - bcum)
    e_last = jnp.exp(b_last)
    kv = [_mm_tn(kd[:, ksl[hl]], v[:, vsl[hl]]) for hl in hs]
    states = []
    outs = []
    og = og_ref[0]
    for hl in hs:
        e_col = jnp.sum(jnp.where(eye, e_last[:, ksl[hl]], 0.0), axis=1, keepdims=True)
        states.append(e_col * s_old[hl] + kv[hl])
        o = blocks[hl][0] if nb == 1 else jnp.concatenate(blocks[hl], axis=0)
        outs.append(_head_rmsnorm(o, nw_ref[...]) * _sigmoid(og[:, vsl[hl]]))
    o_ref[0] = (outs[0] if hb == 1 else jnp.concatenate(outs, axis=1)).astype(o_ref.dtype)
    s_ref[0] = jnp.stack(states, axis=0)


def _hgrn2(z, s0, p, o, layer, chunk):
    b, l, zw = z.shape
    dv = p["hg_norm_w"].shape[1]
    depth, kw = p["hg_lb_logits"].shape
    vw = (zw - 2 * kw) // 2
    heads = vw // dv
    dk = kw // heads
    hb = _pick_tile(heads, HGRN_HEADS_PER_STEP, 1)
    nhb = heads // hb
    wk, wv = hb * dk, hb * dv
    col = lambda w_, off: pl.BlockSpec((1, chunk, w_), lambda i, h, t: (i, t, off + h))
    st = pl.BlockSpec((1, hb, dk, dv), lambda i, h, t: (i, h, 0, 0))
    out, s_fin = pl.pallas_call(
        functools.partial(_hgrn_kernel, hb=hb, dk=dk, dv=dv, layer=layer),
        grid=(b, nhb, l // chunk),
        in_specs=[col(wk, 0), col(wk, kw // wk), col(wv, 2 * kw // wv), col(wv, (2 * kw + vw) // wv),
                  pl.BlockSpec((depth, wk), lambda i, h, t: (0, h)),
                  pl.BlockSpec((1, dv), lambda i, h, t: (0, 0)),
                  st],
        out_specs=[pl.BlockSpec((1, chunk, wv), lambda i, h, t: (i, t, h)), st],
        out_shape=[jax.ShapeDtypeStruct((b, l, vw), BF16), jax.ShapeDtypeStruct((b, heads, dk, dv), F32)],
        compiler_params=_params(("arbitrary", "arbitrary", "arbitrary")),
        name="hgrn2",
    )(z, z, z, z, p["hg_lb_logits"].astype(F32), p["hg_norm_w"][o:o + 1], s0.astype(F32))
    return out, s_fin.astype(s0.dtype)


def _trunk(x, mod_all, st, p, prep):
    rw_s, rw_sh, s5_re, s5_im, gdn_s, gdn_cv, hg_s = st
    b, l, d = x.shape
    depth = p["w_mod"].shape[0]
    d_ff = p["w_ffn_out"].shape[2]
    m = b * l
    rpg = l if l % SUBLANES == 0 and l >= 256 else None
    n_rw, n_sh, n_re, n_im, n_gdn, n_cv, n_hg = [], [], [], [], [], [], []
    chunk = min(CHUNK, l)

    def gate_of(mod9, idx):
        g = mod9[:, 3 * idx + 2]
        if rpg is not None:
            return g.reshape(b, 1, d)
        return jnp.repeat(g, l, axis=0).reshape(1, m, d)

    def ffn(x, mod9, lyr, idx, slot):
        h = _norm(x, p["norm_g"][lyr, idx], mod9, idx).reshape(m, d)
        tn = _pick_tile(d_ff, 256)
        act = _matmul(h, p["w_ffn_in"], (lyr, slot), ((0, d_ff // tn), d_ff // tn), tn,
                      epi="swiglu", out_dtype=BF16, tm_pref=1024)
        tn2 = _pick_tile(d, 512)
        y = _matmul(act, p["w_ffn_out"], (lyr, slot), ((0,), d // tn2), tn2, epi="res",
                    res=x.reshape(m, d), gate=gate_of(mod9, idx), rows_per_gate=rpg, coef=0.5,
                    w_buffers=1)
        return y.reshape(b, l, d)

    for lyr in range(depth):
        mod9 = mod_all[lyr]
        x = ffn(x, mod9, lyr, 0, 0)
        h = _norm(x, p["norm_g"][lyr, 1], mod9, 1).reshape(m, d)
        if lyr % 2 == 0:
            e = lyr // 2
            even_in = p["w_in_even"].shape[2]
            rw_in = p["rw_mu"].shape[1]
            tn = _pick_tile(even_in, 1280, 256)
            zin = _matmul(h, p["w_in_even"], (e,), ((0,), even_in // tn), tn, w_buffers=1).reshape(b, l, even_in)
            oa, sh, s = _rwkv7(zin, rw_in, rw_sh[e], rw_s[e], p, e)
            ob, hr, hi = _s5(zin, rw_in, s5_re[e], s5_im[e], p, e)
            n_rw.append(s)
            n_sh.append(sh)
            n_re.append(hr)
            n_im.append(hi)
            mix = (oa.reshape(m, -1), ob.reshape(m, -1))
            w_out, widx = p["w_out_even"], (e,)
        else:
            o = lyr // 2
            w_gdn, w_hg, w_ab = prep["odd"][o]
            gdn_main = w_gdn.shape[1]
            tn = _pick_tile(gdn_main, 512)
            zmain = _matmul(h, w_gdn, (), ((0,), gdn_main // tn), tn, tm_pref=1024).reshape(b, l, gdn_main)
            zab = _matmul(h, w_ab, (), ((0,), 1), LANES).reshape(b, l, LANES)
            hg_in = w_hg.shape[1]
            tn = _pick_tile(hg_in, 512)
            zhg = _matmul(h, w_hg, (), ((0,), hg_in // tn), tn, tm_pref=1024).reshape(b, l, hg_in)
            oc, cv, s = _gdn(zmain, zab, gdn_cv[o], gdn_s[o], p, o, chunk)
            od, sh_ = _hgrn2(zhg, hg_s[o], p, o, lyr, chunk)
            n_gdn.append(s)
            n_cv.append(cv)
            n_hg.append(sh_)
            mix = (oc.reshape(m, -1), od.reshape(m, -1))
            w_out, widx = p["w_out_odd"], (o,)
        tn = _pick_tile(d, 512)
        x = _matmul(mix, w_out, widx, ((0,), d // tn), tn, epi="res", res=x.reshape(m, d),
                    gate=gate_of(mod9, 1), rows_per_gate=rpg, coef=1.0, tm_pref=1024).reshape(b, l, d)
        x = ffn(x, mod9, lyr, 2, 1)
    y = _norm(x, p["norm_final"], out_dtype=x.dtype)
    return y, (jnp.stack(n_rw), jnp.stack(n_sh), jnp.stack(n_re), jnp.stack(n_im),
               jnp.stack(n_gdn), jnp.stack(n_cv), jnp.stack(n_hg))


def kernel(x_prompt, x_sample, c_prompt, c_sample, state_rwkv, state_rwkv_shift, state_s5_re, state_s5_im, state_gdn, cache_gdn_conv, state_hgrn, w_mod, b_mod, norm_g, norm_final, w_ffn_in, w_ffn_out, w_in_even, w_out_even, rw_mu, rw_w0, rw_w_up, rw_a0, rw_a_up, rw_g_up, rw_k_k, rw_k_a, rw_r_k, rw_ln_w, rw_ln_b, s5_lambda_re, s5_lambda_im, s5_log_dt, s5_b_re, s5_b_im, s5_c_re, s5_c_im, s5_d, s5_glu_w, s5_glu_b, w_in_odd, w_out_odd, gdn_conv_w, gdn_a_log, gdn_dt_bias, gdn_norm_w, hg_lb_logits, hg_norm_w):
    p = dict(w_mod=w_mod, b_mod=b_mod, norm_g=norm_g, norm_final=norm_final,
             w_ffn_in=w_ffn_in, w_ffn_out=w_ffn_out, w_in_even=w_in_even, w_out_even=w_out_even,
             rw_mu=rw_mu, rw_w0=rw_w0, rw_w_up=rw_w_up, rw_a0=rw_a0, rw_a_up=rw_a_up, rw_g_up=rw_g_up,
             rw_k_k=rw_k_k, rw_k_a=rw_k_a, rw_r_k=rw_r_k, rw_ln_w=rw_ln_w, rw_ln_b=rw_ln_b,
             s5_lambda_re=s5_lambda_re, s5_lambda_im=s5_lambda_im, s5_log_dt=s5_log_dt,
             s5_b_re=s5_b_re, s5_b_im=s5_b_im, s5_c_re=s5_c_re, s5_c_im=s5_c_im, s5_d=s5_d,
             s5_glu_w=s5_glu_w, s5_glu_b=s5_glu_b, w_in_odd=w_in_odd, w_out_odd=w_out_odd,
             gdn_conv_w=gdn_conv_w, gdn_a_log=gdn_a_log, gdn_dt_bias=gdn_dt_bias, gdn_norm_w=gdn_norm_w,
             hg_lb_logits=hg_lb_logits, hg_norm_w=hg_norm_w)
    bp, bs = x_prompt.shape[0], x_sample.shape[0]
    d = x_prompt.shape[-1]
    depth = w_mod.shape[0]
    n_even, n_odd = (depth + 1) // 2, depth // 2
    dtp = x_prompt.dtype

    bc = bp + bs
    bc_pad = -(-bc // 16) * 16
    c_all = jnp.pad(jnp.concatenate([c_prompt, c_sample], axis=0), ((0, bc_pad - bc), (0, 0)))
    mod = _modulation(c_all, w_mod, b_mod).reshape(depth, bc_pad, 9, d)
    mod_p, mod_s = mod[:, :bp], mod[:, bp:bc]

    heads = gdn_a_log.shape[1]
    conv_ch = gdn_conv_w.shape[2]
    vw = heads * gdn_norm_w.shape[1]
    gdn_main = conv_ch + vw
    gdn_in = gdn_main + 2 * heads
    prep = {"odd": []}
    for o in range(n_odd):
        w_gdn = w_in_odd[o][:, :gdn_main]
        w_hg = w_in_odd[o][:, gdn_in:]
        w_ab = jnp.pad(w_in_odd[o][:, gdn_main:gdn_in], ((0, 0), (0, LANES - 2 * heads)))
        prep["odd"].append((w_gdn, w_hg, w_ab))

    rw_h, rw_n = rw_r_k.shape[1], rw_r_k.shape[2]
    rw_in = rw_mu.shape[1]
    s5_g, s5_n = s5_lambda_re.shape[1], s5_lambda_re.shape[2]
    gdn_dk = (conv_ch - vw) // 2 // heads
    gdn_dv = gdn_norm_w.shape[1]
    kconv = gdn_conv_w.shape[1]
    hg_dv = hg_norm_w.shape[1]
    hg_kw = hg_lb_logits.shape[1]
    hg_vw = (w_in_odd.shape[2] - gdn_in - 2 * hg_kw) // 2
    hg_h = hg_vw // hg_dv
    hg_dk = hg_kw // hg_h
    st_prompt = (jnp.zeros((n_even, bp, rw_h, rw_n, rw_n), dtp),
                 jnp.zeros((n_even, bp, rw_in), dtp),
                 jnp.zeros((n_even, bp, s5_g, s5_n), dtp),
                 jnp.zeros((n_even, bp, s5_g, s5_n), dtp),
                 jnp.zeros((n_odd, bp, heads, gdn_dk, gdn_dv), dtp),
                 jnp.zeros((n_odd, bp, kconv - 1, conv_ch), dtp),
                 jnp.zeros((n_odd, bp, hg_h, hg_dk, hg_dv), dtp))
    st_sample = (state_rwkv, state_rwkv_shift, state_s5_re, state_s5_im, state_gdn, cache_gdn_conv, state_hgrn)
    y_prompt, ns_p = _trunk(x_prompt, mod_p, st_prompt, p, prep)
    y_sample, ns_s = _trunk(x_sample, mod_s, st_sample, p, prep)
    return (y_prompt, y_sample) + tuple(ns_p) + tuple(ns_s)
```

```python
import functools

import jax
import jax.numpy as jnp
from jax import lax
from jax.experimental import pallas as pl
from jax.experimental.pallas import tpu as pltpu

F32 = jnp.float32
BF16 = jnp.bfloat16
HI = lax.Precision.HIGHEST

EPS = 1e-6
RW_GN_EPS = 64e-5
L2_EPS = 1e-12
CHUNK = 64
SUB = 16
GDN_HEADS_PER_STEP = 16
HGRN_HEADS_PER_STEP = 16
LANES = 128
SUBLANES = 8
VMEM_LIMIT = 56 * 1024 * 1024


def _params(sem, big=False):
    return pltpu.CompilerParams(dimension_semantics=sem,
                                vmem_limit_bytes=VMEM_LIMIT if big else None)


def _pick_tile(n, pref, mult=LANES):
    t = (min(pref, n) // mult) * mult
    while t >= mult:
        if n % t == 0:
            return t
        t -= mult
    return n


def _mm(a, b, hi=False):
    if not hi:
        a, b = a.astype(BF16), b.astype(BF16)
    return lax.dot_general(a, b, (((1,), (0,)), ((), ())), precision=HI if hi else None,
                           preferred_element_type=F32)


def _mm_nt(a, b, hi=False):
    if not hi:
        a, b = a.astype(BF16), b.astype(BF16)
    return lax.dot_general(a, b, (((1,), (1,)), ((), ())), precision=HI if hi else None,
                           preferred_element_type=F32)


def _mm_tn(a, b, hi=False):
    if not hi:
        a, b = a.astype(BF16), b.astype(BF16)
    return lax.dot_general(a, b, (((0,), (0,)), ((), ())), precision=HI if hi else None,
                           preferred_element_type=F32)


def _split2(a):
    hi = a.astype(BF16)
    return hi, (a - hi.astype(F32)).astype(BF16)


def _split3(a):
    hi = a.astype(BF16)
    r = a - hi.astype(F32)
    mid = r.astype(BF16)
    return hi, mid, (r - mid.astype(F32)).astype(BF16)


def _bdot(a, b, dims):
    return lax.dot_general(a, b, (dims, ((), ())), preferred_element_type=F32)


def _mm3(a, b, dims=((1,), (0,))):
    ah, al = _split2(a)
    bh, bl = _split2(b)
    return _bdot(ah, bh, dims) + (_bdot(ah, bl, dims) + _bdot(al, bh, dims))


def _mm01(mask, x):
    m = mask.astype(BF16)
    xh, xm, xl = _split3(x)
    dims = ((1,), (0,))
    return _bdot(m, xh, dims) + (_bdot(m, xm, dims) + _bdot(m, xl, dims))


def _x01(x, mask):
    m = mask.astype(BF16)
    xh, xm, xl = _split3(x)
    dims = ((1,), (0,))
    return _bdot(xh, m, dims) + (_bdot(xm, m, dims) + _bdot(xl, m, dims))


def _sigmoid(x):
    return 1.0 / (1.0 + jnp.exp(-x))


def _silu(x):
    return x * _sigmoid(x)


def _softplus(x):
    return jnp.maximum(x, 0.0) + jnp.log(1.0 + jnp.exp(-jnp.abs(x)))


def _iota2(shape, axis):
    return lax.broadcasted_iota(jnp.int32, shape, axis)


def _mod_kernel(c_ref, w_ref, b_ref, o_ref):
    s = _silu(c_ref[...]).astype(BF16)
    o_ref[0] = jnp.dot(s, w_ref[0].astype(BF16), preferred_element_type=F32) + b_ref[0]


def _modulation(c, w_mod, b_mod):
    depth, d, n = w_mod.shape
    bc = c.shape[0]
    tn = _pick_tile(n, 512)
    return pl.pallas_call(
        _mod_kernel,
        grid=(depth, n // tn),
        in_specs=[pl.BlockSpec((bc, d), lambda l, j: (0, 0)),
                  pl.BlockSpec((1, d, tn), lambda l, j: (l, 0, j)),
                  pl.BlockSpec((1, 1, tn), lambda l, j: (l, 0, j))],
        out_specs=pl.BlockSpec((1, bc, tn), lambda l, j: (l, 0, j)),
        out_shape=jax.ShapeDtypeStruct((depth, bc, n), F32),
        compiler_params=_params(("arbitrary", "arbitrary"), big=True),
        name="modulation",
    )(c, w_mod, b_mod.reshape(depth, 1, n))


def _norm_kernel(x_ref, g_ref, *rest, idx):
    if idx is None:
        (o_ref,) = rest
    else:
        m_ref, o_ref = rest
    x = x_ref[0]
    y = x * lax.rsqrt(jnp.mean(x * x, axis=-1, keepdims=True) + EPS) * g_ref[...]
    if idx is not None:
        shift = m_ref[0, pl.ds(3 * idx, 1), :]
        scale = m_ref[0, pl.ds(3 * idx + 1, 1), :]
        y = y * (1.0 + scale) + shift
    o_ref[0] = y.astype(o_ref.dtype)


def _norm(x, g, mod9=None, idx=None, out_dtype=BF16):
    b, l, d = x.shape
    tl = _pick_tile(l, 256, SUBLANES)
    in_specs = [pl.BlockSpec((1, tl, d), lambda i, t: (i, t, 0)),
                pl.BlockSpec((1, d), lambda i, t: (0, 0))]
    args = [x, g.reshape(1, d)]
    if idx is not None:
        in_specs.append(pl.BlockSpec((1, mod9.shape[1], d), lambda i, t: (i, 0, 0)))
        args.append(mod9)
    return pl.pallas_call(
        functools.partial(_norm_kernel, idx=idx),
        grid=(b, l // tl),
        in_specs=in_specs,
        out_specs=pl.BlockSpec((1, tl, d), lambda i, t: (i, t, 0)),
        out_shape=jax.ShapeDtypeStruct((b, l, d), out_dtype),
        compiler_params=_params(("arbitrary", "arbitrary")),
        name="norm",
    )(*args)


def _mm_kernel(*refs, n_x, n_w, epi, coef, rider):
    it = iter(refs)
    take = lambda n: [next(it) for _ in range(n)]
    x_refs, w_refs = take(n_x), take(n_w)
    res_ref, gate_ref = take(2) if epi == "res" else (None, None)
    if rider:
        x2_refs = take(n_x)
        res2_ref, gate2_ref = take(2) if epi == "res" else (None, None)
    (o_ref,) = take(1)
    if rider:
        (o2_ref,) = take(1)
    wb_refs = take(n_w)

    def rows(xr, res_r, gate_r, out_r):
        acc = []
        for wb_ref in wb_refs:
            k0, a = 0, None
            for x_ref in xr:
                kx = x_ref.shape[1]
                part = jnp.dot(x_ref[...], wb_ref[pl.ds(k0, kx), :], preferred_element_type=F32)
                a = part if a is None else a + part
                k0 += kx
            acc.append(a)
        if epi == "swiglu":
            out = _silu(acc[0]) * acc[1]
        elif epi == "res":
            out = res_r[...] + coef * gate_r[0] * acc[0]
        else:
            out = acc[0]
        out_r[...] = out.astype(out_r.dtype)

    @pl.when(pl.program_id(1) == 0)
    def _():
        for w_ref, wb_ref in zip(w_refs, wb_refs):
            wb_ref[...] = w_ref[...].astype(BF16)
        if rider:
            rows(x2_refs, res2_ref, gate2_ref, o2_ref)

    rows(x_refs, res_ref, gate_ref, o_ref)


def _matmul(x, w, prefix, col_blocks, tn, *, rows_per_gate, res=None, gate=None, rider=None,
            coef=1.0, epi="plain", out_dtype=F32, tm_pref=512, w_buffers=2):
    xs = x if isinstance(x, (tuple, list)) else (x,)
    m = xs[0].shape[0]
    k = sum(t.shape[1] for t in xs)
    offs, n_cols = col_blocks
    n_w = len(offs)
    n_out = n_cols * tn
    tm = _pick_tile(rows_per_gate, tm_pref, SUBLANES)
    rpt = rows_per_gate // tm
    npre = len(prefix)
    wmode = {} if w_buffers == 2 else {"pipeline_mode": pl.Buffered(w_buffers)}
    in_specs = [pl.BlockSpec((tm, t.shape[1]), lambda j, i: (i, 0)) for t in xs]
    args = list(xs)
    for off in offs:
        in_specs.append(pl.BlockSpec((None,) * npre + (k, tn),
                                     lambda j, i, off=off: tuple(prefix) + (0, j + off), **wmode))
        args.append(w)
    if epi == "res":
        in_specs += [pl.BlockSpec((tm, tn), lambda j, i: (i, j)),
                     pl.BlockSpec((1, 1, tn), lambda j, i: (i // rpt, 0, j))]
        args += [res, gate]
    out_specs = [pl.BlockSpec((tm, tn), lambda j, i: (i, j))]
    out_shape = [jax.ShapeDtypeStruct((m, n_out), out_dtype)]
    if rider is not None:
        x2, res2, gate2 = rider
        x2s = x2 if isinstance(x2, (tuple, list)) else (x2,)
        m2 = x2s[0].shape[0]
        in_specs += [pl.BlockSpec((m2, t.shape[1]), lambda j, i: (0, 0)) for t in x2s]
        args += list(x2s)
        if epi == "res":
            in_specs += [pl.BlockSpec((m2, tn), lambda j, i: (0, j)),
                         pl.BlockSpec((1, m2, tn), lambda j, i: (0, 0, j))]
            args += [res2, gate2]
        out_specs.append(pl.BlockSpec((m2, tn), lambda j, i: (0, j)))
        out_shape.append(jax.ShapeDtypeStruct((m2, n_out), out_dtype))
    outs = pl.pallas_call(
        functools.partial(_mm_kernel, n_x=len(xs), n_w=n_w, epi=epi, coef=coef, rider=rider is not None),
        grid=(n_cols, m // tm),
        in_specs=in_specs,
        out_specs=out_specs,
        out_shape=out_shape,
        scratch_shapes=[pltpu.VMEM((k, tn), BF16) for _ in range(n_w)],
        compiler_params=_params(("arbitrary", "arbitrary"), big=True),
        name="matmul_" + epi,
    )(*args)
    return tuple(outs) if rider is not None else (outs[0], None)


def _group_ones(n):
    r = _iota2((LANES, LANES), 0)
    c = _iota2((LANES, LANES), 1)
    sh = n.bit_length() - 1
    return (jnp.right_shift(r, sh) == jnp.right_shift(c, sh)).astype(F32)


def _group_sum(x, n):
    if n % LANES == 0:
        parts = []
        for h in range(x.shape[1] // n):
            s = jnp.sum(x[:, h * n:(h + 1) * n], axis=-1, keepdims=True)
            parts.append(jnp.broadcast_to(s, (x.shape[0], n)))
        return parts[0] if len(parts) == 1 else jnp.concatenate(parts, axis=1)
    ones = _group_ones(n)
    parts = [_x01(x[:, s * LANES:(s + 1) * LANES], ones) for s in range(x.shape[1] // LANES)]
    return parts[0] if len(parts) == 1 else jnp.concatenate(parts, axis=1)


def _tril(c, strict=False):
    r = _iota2((c, c), 0)
    s = _iota2((c, c), 1)
    return (r > s) if strict else (r >= s)


def _unit_lower_solve(lms, rhss):
    c = lms[0].shape[0]
    r = _iota2((c, c), 0)
    s = _iota2((c, c), 1)
    eye = (r == s).astype(F32)
    if c <= SUB:
        lds, los = lms, None
    else:
        sh = SUB.bit_length() - 1
        same = jnp.right_shift(r, sh) == jnp.right_shift(s, sh)
        lds = [jnp.where(same, lm, 0.0) for lm in lms]
        los = [lm - ld for lm, ld in zip(lms, lds)]
    order = min(SUB, c)
    xs = [-ld for ld in lds]
    ps = [eye + x for x in xs]
    pws = [_mm3(x, x) for x in xs]
    k = 2
    while k < order:
        new_ps = [p + _mm3(pw, p) for p, pw in zip(ps, pws)]
        if 2 * k < order:
            pws = [_mm3(pw, pw) for pw in pws]
        ps = new_ps
        k *= 2
    sols = [_mm3(p, rhs) for p, rhs in zip(ps, rhss)]
    if los is None:
        return sols
    pws = [-_mm3(p, lo) for p, lo in zip(ps, los)]
    k = 1
    while True:
        new_sols = [sol + _mm3(pw, sol) for sol, pw in zip(sols, pws)]
        if 2 * k >= c // SUB:
            return new_sols
        pws = [_mm3(pw, pw) for pw in pws]
        sols = new_sols
        k *= 2


def _head_rmsnorm(o, w):
    return o * lax.rsqrt(jnp.mean(o * o, axis=-1, keepdims=True) + EPS) * w


def _rw_pre_kernel(z_ref, sh_ref, mu_ref, w0_ref, a0_ref, wup_ref, aup_ref, gup_ref,
                   r_o, d_o, k_o, v_o, a_o, g_o, prev_scr, *, width, w_lora, a_lora):
    @pl.when(pl.program_id(1) == 0)
    def _():
        prev_scr[...] = sh_ref[0]

    z = z_ref[0]
    tl = z.shape[0]
    rolled = pltpu.roll(z, 1, 0)
    prev = jnp.where(_iota2(z.shape, 0) == 0, prev_scr[...], rolled)
    prev_scr[...] = z_ref[0, pl.ds(tl - 1, 1), :]
    zs = z + mu_ref[...] * (prev - z)
    c1, c2, c3 = width, 2 * width, 3 * width
    c4 = c3 + w_lora
    c5 = c4 + a_lora
    r, k, v = zs[:, :c1], zs[:, c1:c2], zs[:, c2:c3]
    xw, xa, xg = zs[:, c3:c4], zs[:, c4:c5], zs[:, c5:]
    w = -_softplus(-(w0_ref[...] + _mm(jnp.tanh(xw), wup_ref[...]))) - 0.5
    decay = jnp.exp(-jnp.exp(w))
    a = _sigmoid(a0_ref[...] + _mm(xa, aup_ref[...]))
    g = _mm(_sigmoid(xg), gup_ref[...])
    r_o[0] = r
    d_o[0] = decay
    k_o[0] = k
    v_o[0] = v
    a_o[0] = a
    g_o[0] = g


def _rw_pre(zin, rw_in, shift, q):
    b, l, _ = zin.shape
    width = q["w0"].shape[1]
    w_lora, a_lora, g_lora = q["w_up"].shape[0], q["a_up"].shape[0], q["g_up"].shape[0]
    tl = _pick_tile(l, 128, SUBLANES)
    row = lambda n: pl.BlockSpec((1, n), lambda i, t: (0, 0))
    full = lambda r: pl.BlockSpec((r, width), lambda i, t: (0, 0))
    out_spec = pl.BlockSpec((1, tl, width), lambda i, t: (i, t, 0))
    out_shape = jax.ShapeDtypeStruct((b, l, width), F32)
    return pl.pallas_call(
        functools.partial(_rw_pre_kernel, width=width, w_lora=w_lora, a_lora=a_lora),
        grid=(b, l // tl),
        in_specs=[pl.BlockSpec((1, tl, rw_in), lambda i, t: (i, t, 0)),
                  pl.BlockSpec((1, 1, rw_in), lambda i, t: (i, 0, 0)),
                  row(rw_in), row(width), row(width), full(w_lora), full(a_lora), full(g_lora)],
        out_specs=[out_spec] * 6,
        out_shape=[out_shape] * 6,
        scratch_shapes=[pltpu.VMEM((1, rw_in), F32)],
        compiler_params=_params(("arbitrary", "arbitrary"), big=True),
        name="rwkv_pre",
    )(zin, shift.reshape(b, 1, rw_in), q["mu"], q["w0"], q["a0"], q["w_up"], q["a_up"], q["g_up"])


_RW_R, _RW_D, _RW_K, _RW_V, _RW_KK, _RW_NKKA = range(6)


def _rw_scan_kernel(r_ref, d_ref, k_ref, v_ref, a_ref, g_ref, par_ref, s0_ref, out_ref, s_ref,
                    seq_scr, o_scr, bs_scr, *, steps, pitch, n, heads, bpl):
    @pl.when(pl.program_id(1) == 0)
    def _():
        s_ref[...] = s0_ref[...]

    used = bpl * heads

    def to_lanes(ref, j):
        pieces = [ref[b, :, j * heads:(j + 1) * heads] for b in range(bpl)]
        if used < LANES:
            pieces.append(jnp.zeros((steps, LANES - used), F32))
        return pieces[0] if len(pieces) == 1 else jnp.concatenate(pieces, axis=1)

    kkp, kap, rkp, lnw, lnb = (par_ref[i] for i in range(5))
    sumsq = bsum = None
    for j in range(n):
        rows = pl.ds(j * pitch, steps)
        r, k, a = to_lanes(r_ref, j), to_lanes(k_ref, j), to_lanes(a_ref, j)
        kraw = k * kkp[j:j + 1]
        k2 = k * (1.0 + (a - 1.0) * kap[j:j + 1])
        seq_scr[_RW_R, rows, :] = r
        seq_scr[_RW_D, rows, :] = to_lanes(d_ref, j)
        seq_scr[_RW_V, rows, :] = to_lanes(v_ref, j)
        seq_scr[_RW_K, rows, :] = k2
        seq_scr[_RW_KK, rows, :] = kraw
        seq_scr[_RW_NKKA, rows, :] = a
        sq = kraw * kraw
        bt = r * k2 * rkp[j:j + 1]
        sumsq = sq if sumsq is None else sumsq + sq
        bsum = bt if bsum is None else bsum + bt
    inv = lax.rsqrt(sumsq + L2_EPS)
    bs_scr[...] = bsum
    for j in range(n):
        rows = pl.ds(j * pitch, steps)
        kk = seq_scr[_RW_KK, rows, :] * inv
        seq_scr[_RW_KK, rows, :] = kk
        seq_scr[_RW_NKKA, rows, :] = -(kk * seq_scr[_RW_NKKA, rows, :])

    def row(q, j, t):
        return seq_scr[q, pl.ds(j * pitch + t, 1), :]

    sa0 = s_ref[0] * row(_RW_KK, 0, 0)
    for j in range(1, n):
        sa0 = sa0 + s_ref[j] * row(_RW_KK, j, 0)

    def step(t, sa):
        tn = jnp.minimum(t + 1, steps - 1)
        tile = pl.ds(t, n, stride=pitch)
        v_t = seq_scr[_RW_V, tile, :]
        o = None
        sa_next = None
        for j in range(n):
            sj = s_ref[j] * row(_RW_D, j, t) + sa * row(_RW_NKKA, j, t) + v_t * row(_RW_K, j, t)
            s_ref[j] = sj
            term = sj * row(_RW_R, j, t)
            o = term if o is None else o + term
            nxt = sj * row(_RW_KK, j, tn)
            sa_next = nxt if sa_next is None else sa_next + nxt
        mu = jnp.mean(o, axis=0, keepdims=True)
        c = o - mu
        var = jnp.mean(c * c, axis=0, keepdims=True)
        y = c * lax.rsqrt(var + RW_GN_EPS) * lnw + lnb
        o_scr[tile, :] = y + bs_scr[pl.ds(t, 1), :] * v_t
        return sa_next

    lax.fori_loop(0, steps, step, sa0)

    per = LANES // heads
    for b in range(bpl):
        for qd in range(n // per):
            pieces = [o_scr[pl.ds(i * pitch, steps), :][:, b * heads:(b + 1) * heads]
                      for i in range(qd * per, (qd + 1) * per)]
            slab = pieces[0] if per == 1 else jnp.concatenate(pieces, axis=1)
            cols = slice(qd * LANES, (qd + 1) * LANES)
            out_ref[b, :, cols] = (slab * g_ref[b, :, cols]).astype(out_ref.dtype)


def _rw_scan(seqs, g, par, s0, heads, bpl):
    b, l, width = g.shape
    n = width // heads
    steps = _pick_tile(l, 32, 2 * SUBLANES)
    pitch = steps + SUBLANES
    seq_spec = pl.BlockSpec((bpl, steps, width), lambda c, t: (c, t, 0))
    st_spec = pl.BlockSpec((n, n, LANES), lambda c, t: (0, 0, c))
    return pl.pallas_call(
        functools.partial(_rw_scan_kernel, steps=steps, pitch=pitch, n=n, heads=heads, bpl=bpl),
        grid=(b // bpl, l // steps),
        in_specs=[seq_spec] * 6 + [pl.BlockSpec((5, n, LANES), lambda c, t: (0, 0, 0)), st_spec],
        out_specs=[seq_spec, st_spec],
        out_shape=[jax.ShapeDtypeStruct((b, l, width), BF16), jax.ShapeDtypeStruct(s0.shape, F32)],
        scratch_shapes=[pltpu.VMEM((6, n * pitch, LANES), F32), pltpu.VMEM((n * pitch, LANES), F32),
                        pltpu.VMEM((steps, LANES), F32)],
        compiler_params=_params(("arbitrary", "arbitrary"), big=True),
        name="rwkv_scan",
    )(*seqs, g, par, s0)


def _index_major(t, heads, n, blocks=1, inverse=False):
    width = heads * n
    a, c = (n, heads) if inverse else (heads, n)
    parts = []
    for i in range(blocks):
        blk = t[..., i * width:(i + 1) * width]
        parts.append(blk.reshape(blk.shape[:-1] + (a, c)).swapaxes(-1, -2).reshape(blk.shape))
    if t.shape[-1] > blocks * width:
        parts.append(t[..., blocks * width:])
    return parts[0] if len(parts) == 1 else jnp.concatenate(parts, axis=-1)


def _rw_prep(p, e):
    heads, n = p["rw_r_k"].shape[1], p["rw_r_k"].shape[2]
    im = functools.partial(_index_major, heads=heads, n=n)
    w_out = p["w_out_even"][e]
    return dict(heads=heads, n=n,
                w_in=im(p["w_in_even"][e], blocks=3), w_out=jnp.swapaxes(im(jnp.swapaxes(w_out, 0, 1)), 0, 1),
                mu=im(p["rw_mu"][e:e + 1], blocks=3), w0=im(p["rw_w0"][e:e + 1]), a0=im(p["rw_a0"][e:e + 1]),
                w_up=im(p["rw_w_up"][e]), a_up=im(p["rw_a_up"][e]), g_up=im(p["rw_g_up"][e]),
                lane=[t.reshape(heads, n).T for t in (p["rw_k_k"][e], p["rw_k_a"][e], p["rw_r_k"][e],
                                                      p["rw_ln_w"][e], p["rw_ln_b"][e])])


def _rwkv7(zin, rw_in, shift, s0, q):
    b, l, _ = zin.shape
    heads, n = q["heads"], q["n"]
    bpl = min(b, LANES // heads)
    nlb, used = b // bpl, bpl * heads
    r, d, k, v, a, g = _rw_pre(zin, rw_in, _index_major(shift, heads, n, blocks=3), q)
    par = jnp.stack([jnp.pad(jnp.tile(t, (1, bpl)), ((0, 0), (0, LANES - used))) for t in q["lane"]])
    s0_l = s0.astype(F32).transpose(3, 2, 0, 1).reshape(n, n, nlb, used)
    s0_l = jnp.pad(s0_l, ((0, 0), (0, 0), (0, 0), (0, LANES - used))).reshape(n, n, nlb * LANES)
    out, s_l = _rw_scan((r, d, k, v, a), g, par, s0_l, heads, bpl)
    s_fin = s_l.reshape(n, n, nlb, LANES)[..., :used].reshape(n, n, b, heads).transpose(2, 3, 1, 0)
    new_shift = _index_major(zin[:, -1, :rw_in], heads, n, blocks=3, inverse=True)
    return out, new_shift.astype(shift.dtype), s_fin.astype(s0.dtype)


def _s5_param_kernel(ldt_ref, lr_ref, li_ref, bre_ref, bim_ref, are_o, aim_o, bbre_o, bbim_o):
    dt = jnp.exp(ldt_ref[...])
    lr, li = lr_ref[...], li_ref[...]
    mag = jnp.exp(lr * dt)
    ab_re, ab_im = mag * jnp.cos(li * dt), mag * jnp.sin(li * dt)
    den = lr * lr + li * li
    pr, pi_ = ab_re - 1.0, ab_im
    coef_re = (pr * lr + pi_ * li) / den
    coef_im = (pi_ * lr - pr * li) / den
    are_o[...] = ab_re
    aim_o[...] = ab_im
    for m in range(bre_ref.shape[0]):
        b_re, b_im = bre_ref[m], bim_ref[m]
        bbre_o[m] = coef_re * b_re - coef_im * b_im
        bbim_o[m] = coef_re * b_im + coef_im * b_re


def _s5_params(p, e):
    g, n, m = p["s5_b_re"].shape[1:]
    outs = pl.pallas_call(
        _s5_param_kernel,
        out_shape=[jax.ShapeDtypeStruct((g, n), F32)] * 2 + [jax.ShapeDtypeStruct((m, g, n), F32)] * 2,
        name="s5_params",
    )(p["s5_log_dt"][e].reshape(g, 1), p["s5_lambda_re"][e], p["s5_lambda_im"][e],
      p["s5_b_re"][e].transpose(2, 0, 1), p["s5_b_im"][e].transpose(2, 0, 1))
    return outs


def _block_diag(t, gs):
    g, a, b = t.shape
    t = t.reshape(g // gs, gs, a, b)
    eye = jnp.eye(gs, dtype=t.dtype)
    return jnp.einsum("sgab,gh->sgahb", t, eye).reshape(g // gs, gs * a, gs * b)


def _s5_kernel(u_ref, bre_ref, bim_ref, cre_ref, cim_ref, d_ref, gw_ref, gb_ref, are_ref, aim_ref,
               h0r_ref, h0i_ref, o_ref, hr_ref, hi_ref, u_scr, xr_scr, xi_scr, *, steps, pitch):
    @pl.when(pl.program_id(1) == 0)
    def _():
        hr_ref[...] = h0r_ref[...]
        hi_ref[...] = h0i_ref[...]

    nb = u_ref.shape[0]
    nk = xr_scr.shape[0]
    lanes = [slice(k * LANES, (k + 1) * LANES) for k in range(nk)]
    for b in range(nb):
        u_scr[pl.ds(b * pitch, steps), :] = u_ref[b]
        u_scr[pl.ds(b * pitch + steps, pitch - steps), :] = jnp.zeros((pitch - steps, LANES), F32)
    u2 = u_scr[...]
    bu_re = _mm(u2, bre_ref[0])
    bu_im = _mm(u2, bim_ref[0])
    for k in range(nk):
        xr_scr[k] = bu_re[:, lanes[k]]
        xi_scr[k] = bu_im[:, lanes[k]]
    a_re = [are_ref[0, :, lanes[k]] for k in range(nk)]
    a_im = [aim_ref[0, :, lanes[k]] for k in range(nk)]

    def step(t, carry):
        rows = pl.ds(t, nb, stride=pitch)
        new = []
        for k in range(nk):
            h_re, h_im = carry[2 * k], carry[2 * k + 1]
            n_re = a_re[k] * h_re - a_im[k] * h_im + xr_scr[k, rows, :]
            n_im = a_re[k] * h_im + a_im[k] * h_re + xi_scr[k, rows, :]
            xr_scr[k, rows, :] = n_re
            xi_scr[k, rows, :] = n_im
            new += [n_re, n_im]
        return tuple(new)

    init = tuple(r[:, lanes[k]] for k in range(nk) for r in (hr_ref, hi_ref))
    fin = lax.fori_loop(0, steps, step, init)
    for k in range(nk):
        hr_ref[:, lanes[k]] = fin[2 * k]
        hi_ref[:, lanes[k]] = fin[2 * k + 1]
    hs_re = jnp.concatenate([xr_scr[k] for k in range(nk)], axis=1) if nk > 1 else xr_scr[0]
    hs_im = jnp.concatenate([xi_scr[k] for k in range(nk)], axis=1) if nk > 1 else xi_scr[0]
    y = _mm(hs_re, cre_ref[0]) - _mm(hs_im, cim_ref[0]) + d_ref[0] * u2
    yg = 0.5 * y * (1.0 + jnp.tanh(0.7978845608028654 * (y + 0.044715 * (y * y * y))))
    out = yg * _sigmoid(_mm(yg, gw_ref[0]) + gb_ref[0])
    for b in range(nb):
        o_ref[b] = out[b * pitch:b * pitch + steps].astype(o_ref.dtype)


def _s5(zin, col0, h_re, h_im, p, e):
    b, l, _ = zin.shape
    g, n, m = p["s5_b_re"].shape[1:]
    width = g * m
    gs = LANES // m
    ns = g // gs
    sn = gs * n
    ab_re, ab_im, bb_re, bb_im = _s5_params(p, e)
    bd_bre = _block_diag(bb_re.transpose(1, 0, 2), gs)
    bd_bim = _block_diag(bb_im.transpose(1, 0, 2), gs)
    bd_cre = _block_diag(p["s5_c_re"][e].transpose(0, 2, 1), gs)
    bd_cim = _block_diag(p["s5_c_im"][e].transpose(0, 2, 1), gs)
    bd_gw = _block_diag(p["s5_glu_w"][e], gs)
    steps = _pick_tile(l, 256, 2 * SUBLANES)
    pitch = steps + SUBLANES
    cb0 = col0 // LANES
    slab = lambda r, c: pl.BlockSpec((1, r, c), lambda s, t: (s, 0, 0))
    st_spec = pl.BlockSpec((b, sn), lambda s, t: (0, s))
    out, hr, hi = pl.pallas_call(
        functools.partial(_s5_kernel, steps=steps, pitch=pitch),
        grid=(ns, l // steps),
        in_specs=[pl.BlockSpec((b, steps, LANES), lambda s, t: (0, t, cb0 + s)),
                  slab(LANES, sn), slab(LANES, sn), slab(sn, LANES), slab(sn, LANES),
                  slab(1, LANES), slab(LANES, LANES), slab(1, LANES), slab(1, sn), slab(1, sn),
                  st_spec, st_spec],
        out_specs=[pl.BlockSpec((b, steps, LANES), lambda s, t: (0, t, s)), st_spec, st_spec],
        out_shape=[jax.ShapeDtypeStruct((b, l, width), BF16),
                   jax.ShapeDtypeStruct((b, g * n), F32), jax.ShapeDtypeStruct((b, g * n), F32)],
        scratch_shapes=[pltpu.VMEM((b * pitch, LANES), F32),
                        pltpu.VMEM((sn // LANES, b * pitch, LANES), F32),
                        pltpu.VMEM((sn // LANES, b * pitch, LANES), F32)],
        compiler_params=_params(("arbitrary", "arbitrary"), big=True),
        name="s5",
    )(zin, bd_bre, bd_bim, bd_cre, bd_cim, p["s5_d"][e].reshape(ns, 1, LANES), bd_gw,
      p["s5_glu_b"][e].reshape(ns, 1, LANES), ab_re.reshape(ns, 1, sn), ab_im.reshape(ns, 1, sn),
      h_re.astype(F32).reshape(b, g * n), h_im.astype(F32).reshape(b, g * n))
    return (out, hr.reshape(b, g, n).astype(h_re.dtype), hi.reshape(b, g, n).astype(h_im.dtype))


def _gdn_gate_kernel(z_ref, alog_ref, dtb_ref, gc_o, beta_o):
    z = z_ref[0]
    c = z.shape[0]
    g = -jnp.exp(alog_ref[...]) * _softplus(z + dtb_ref[...])
    gc_o[0] = _mm01(_tril(c).astype(F32), g)
    beta_o[0] = _sigmoid(z)


def _gdn_gates(zab, a_log, dt_bias, chunk):
    b, l, w = zab.shape
    h = a_log.shape[0]
    pad = lambda t: jnp.pad(t, (0, w - h)).reshape(1, w)
    spec = pl.BlockSpec((1, chunk, w), lambda i, t: (i, t, 0))
    row = pl.BlockSpec((1, w), lambda i, t: (0, 0))
    return pl.pallas_call(
        _gdn_gate_kernel,
        grid=(b, l // chunk),
        in_specs=[spec, row, row],
        out_specs=[spec, spec],
        out_shape=[jax.ShapeDtypeStruct((b, l, w), F32)] * 2,
        compiler_params=_params(("arbitrary", "arbitrary")),
        name="gdn_gates",
    )(zab, pad(a_log), pad(dt_bias))


def _gdn_kernel(q_ref, k_ref, v_ref, y_ref, wq_ref, wk_ref, wv_ref, bq_ref, bk_ref, bv_ref,
                gcol_ref, bcol_ref, grow_ref, nw_ref, s0_ref, o_ref, s_ref, cq_scr, ck_scr, cv_scr,
                *, hb, dk, dv, kconv):
    t = pl.program_id(2)

    @pl.when(t == 0)
    def _():
        s_ref[0] = s0_ref[0]
        cq_scr[...] = bq_ref[0]
        ck_scr[...] = bk_ref[0]
        cv_scr[...] = bv_ref[0]

    c = q_ref.shape[1]

    def conv(x_ref, w_ref, carry_scr):
        x = x_ref[0]
        full = jnp.concatenate([carry_scr[...], x], axis=0)
        carry_scr[...] = x_ref[0, pl.ds(c - SUBLANES, SUBLANES), :]
        acc = None
        for j in range(kconv):
            sh = kconv - 1 - j
            src = full if sh == 0 else pltpu.roll(full, sh, 0)
            term = src[SUBLANES:SUBLANES + c] * w_ref[pl.ds(j, 1), :]
            acc = term if acc is None else acc + term
        return _silu(acc)

    qa = conv(q_ref, wq_ref, cq_scr)
    ka = conv(k_ref, wk_ref, ck_scr)
    va = conv(v_ref, wv_ref, cv_scr)
    incl = _tril(c)
    strict = _tril(c, strict=True)
    hs = range(hb)
    qs, ks, vs = [], [], []
    for hl in hs:
        q = qa[:, hl * dk:(hl + 1) * dk]
        k = ka[:, hl * dk:(hl + 1) * dk]
        qs.append(q * lax.rsqrt(jnp.sum(q * q, axis=-1, keepdims=True) + L2_EPS) * (dk ** -0.5))
        ks.append(k * lax.rsqrt(jnp.sum(k * k, axis=-1, keepdims=True) + L2_EPS))
        vs.append(va[:, hl * dv:(hl + 1) * dv])
    gc_c = [gcol_ref[0, 0, :, hl:hl + 1] for hl in hs]
    beta_c = [bcol_ref[0, 0, :, hl:hl + 1] for hl in hs]
    gc_r = [grow_ref[0, 0, 0, hl:hl + 1, :] for hl in hs]
    g_last = [g[:, c - 1:c] for g in gc_r]
    kbs = [k * b for k, b in zip(ks, beta_c)]
    dmask = [jnp.where(incl, jnp.exp(jnp.where(incl, gc - gr, 0.0)), 0.0) for gc, gr in zip(gc_c, gc_r)]
    kk = [_mm_nt(kb, k) for kb, k in zip(kbs, ks)]
    qk = [_mm_nt(q, k) for q, k in zip(qs, ks)]
    ms = [jnp.where(strict, a * d, 0.0) for a, d in zip(kk, dmask)]
    egc = [jnp.exp(g) for g in gc_c]
    rhss = [jnp.concatenate([v * b, kb * e], axis=1) for v, b, kb, e in zip(vs, beta_c, kbs, egc)]
    sols = _unit_lower_solve(ms, rhss)
    attn = [a * d for a, d in zip(qk, dmask)]
    s_old = [s_ref[0, hl] for hl in hs]
    ws = [_mm(sol[:, dv:], s) for sol, s in zip(sols, s_old)]
    qss = [_mm(q * e, s) for q, e, s in zip(qs, egc, s_old)]
    v_new = [sol[:, :dv] - w for sol, w in zip(sols, ws)]
    av = [_mm(a, vn) for a, vn in zip(attn, v_new)]
    kdec = [k * jnp.exp(gl - gc) for k, gl, gc in zip(ks, g_last, gc_c)]
    kv = [_mm_tn(kd, vn) for kd, vn in zip(kdec, v_new)]
    states = [s * jnp.exp(gl) + x for s, gl, x in zip(s_old, g_last, kv)]
    outs = [_head_rmsnorm(a + b, nw_ref[...]) * _silu(y_ref[0, :, hl * dv:(hl + 1) * dv])
            for hl, a, b in zip(hs, qss, av)]
    o_ref[0] = (outs[0] if hb == 1 else jnp.concatenate(outs, axis=1)).astype(o_ref.dtype)
    s_ref[0] = jnp.stack(states, axis=0)


def _gdn(zmain, zab, conv_buf, s0, p, o, chunk):
    b, l, _ = zmain.shape
    heads = p["gdn_a_log"].shape[1]
    dv = p["gdn_norm_w"].shape[1]
    kconv, conv_ch = p["gdn_conv_w"].shape[1:]
    vw = heads * dv
    kw = (conv_ch - vw) // 2
    dk = kw // heads
    hb = _pick_tile(heads, GDN_HEADS_PER_STEP, 1)
    nhb = heads // hb
    nc = l // chunk
    gc, beta = _gdn_gates(zab, p["gdn_a_log"][o], p["gdn_dt_bias"][o], chunk)
    gcol = gc[..., :heads].reshape(b, l, nhb, hb).transpose(0, 2, 1, 3)
    bcol = beta[..., heads:2 * heads].reshape(b, l, nhb, hb).transpose(0, 2, 1, 3)
    grow = gc[..., :heads].reshape(b, nc, chunk, nhb, hb).transpose(0, 3, 1, 4, 2)
    cbuf = jnp.pad(conv_buf.astype(F32), ((0, 0), (SUBLANES - (kconv - 1), 0), (0, 0)))
    wq, wk = hb * dk, hb * dk
    wv = hb * dv
    qoff, koff, voff, yoff = 0, kw // wk, 2 * kw // wv, (2 * kw + vw) // wv
    col = lambda w_, off: pl.BlockSpec((1, chunk, w_), lambda i, h, t: (i, t, off + h))
    cw = lambda w_, off: pl.BlockSpec((None, kconv, w_), lambda i, h, t: (o, 0, off + h))
    cb = lambda w_, off: pl.BlockSpec((1, SUBLANES, w_), lambda i, h, t: (i, 0, off + h))
    st = pl.BlockSpec((1, hb, dk, dv), lambda i, h, t: (i, h, 0, 0))
    out, s_fin = pl.pallas_call(
        functools.partial(_gdn_kernel, hb=hb, dk=dk, dv=dv, kconv=kconv),
        grid=(b, nhb, nc),
        in_specs=[col(wq, qoff), col(wk, koff), col(wv, voff), col(wv, yoff),
                  cw(wq, qoff), cw(wk, koff), cw(wv, voff),
                  cb(wq, qoff), cb(wk, koff), cb(wv, voff),
                  pl.BlockSpec((1, 1, chunk, hb), lambda i, h, t: (i, h, t, 0)),
                  pl.BlockSpec((1, 1, chunk, hb), lambda i, h, t: (i, h, t, 0)),
                  pl.BlockSpec((1, 1, 1, hb, chunk), lambda i, h, t: (i, h, t, 0, 0)),
                  pl.BlockSpec((1, dv), lambda i, h, t: (0, 0)),
                  st],
        out_specs=[pl.BlockSpec((1, chunk, wv), lambda i, h, t: (i, t, h)), st],
        out_shape=[jax.ShapeDtypeStruct((b, l, vw), BF16), jax.ShapeDtypeStruct((b, heads, dk, dv), F32)],
        scratch_shapes=[pltpu.VMEM((SUBLANES, wq), F32), pltpu.VMEM((SUBLANES, wk), F32),
                        pltpu.VMEM((SUBLANES, wv), F32)],
        compiler_params=_params(("arbitrary", "arbitrary", "arbitrary")),
        name="gdn",
    )(zmain, zmain, zmain, zmain, p["gdn_conv_w"], p["gdn_conv_w"], p["gdn_conv_w"],
      cbuf, cbuf, cbuf, gcol, bcol, grow, p["gdn_norm_w"][o:o + 1], s0.astype(F32))
    tail = jnp.concatenate([conv_buf.astype(F32), zmain[:, -(kconv - 1):, :conv_ch]], axis=1)[:, -(kconv - 1):]
    return out, tail.astype(conv_buf.dtype), s_fin.astype(s0.dtype)


def _hgrn_kernel(q_ref, f_ref, i_ref, og_ref, lbl_ref, nw_ref, s0_ref, o_ref, s_ref, *, hb, dk, dv, layer):
    @pl.when(pl.program_id(2) == 0)
    def _():
        s_ref[0] = s0_ref[0]

    c = q_ref.shape[1]
    logits = lbl_ref[...]
    mx = jnp.max(logits, axis=0, keepdims=True)
    ex = jnp.exp(logits - mx)
    den = jnp.sum(ex, axis=0, keepdims=True)
    lb_all = jnp.zeros_like(den)
    for r in range(1, layer + 1):
        lb_all = lb_all + ex[r:r + 1] / den
    tril_f = _tril(c).astype(F32)
    nb = max(c // SUB, 1)
    sb = min(SUB, c)
    rowi = _iota2((sb, 1), 0)
    eye = _iota2((dk, dk), 0) == _iota2((dk, dk), 1)
    hs = range(hb)
    ksl = [slice(hl * dk, (hl + 1) * dk) for hl in hs]
    vsl = [slice(hl * dv, (hl + 1) * dv) for hl in hs]
    q = q_ref[0]
    f = lb_all + (1.0 - lb_all) * _sigmoid(f_ref[0])
    k = 1.0 - f
    v = i_ref[0]
    bcum = _mm01(tril_f, jnp.log(f))
    qe = q * jnp.exp(bcum)
    s_old = [s_ref[0, hl] for hl in hs]
    inter = [_mm(qe[:, ksl[hl]], s_old[hl]) for hl in hs]
    blocks = [[] for _ in hs]
    for bi in range(nb):
        r0 = bi * sb
        qi, ki, vi, bb = q[r0:r0 + sb], k[r0:r0 + sb], v[r0:r0 + sb], bcum[r0:r0 + sb]
        pieces = [[] for _ in hs]
        for p0 in range(0, sb, SUBLANES):
            p1 = min(p0 + SUBLANES, sb)
            qp, bp_ = qi[p0:p1], bb[p0:p1]
            acc = [inter[hl][r0 + p0:r0 + p1] for hl in hs]
            for si in range(p1):
                diff = bp_ - bb[si:si + 1]
                if si > p0:
                    msk = rowi[p0:p1] >= si
                    dec = jnp.where(msk, jnp.exp(jnp.where(msk, diff, 0.0)), 0.0)
                else:
                    dec = jnp.exp(diff)
                prod = qp * ki[si:si + 1] * dec
                for hl in hs:
                    colv = jnp.sum(prod[:, ksl[hl]], axis=-1, keepdims=True)
                    acc[hl] = acc[hl] + colv * vi[si:si + 1, vsl[hl]]
            for hl in hs:
                pieces[hl].append(acc[hl])
        acc = [pc[0] if len(pc) == 1 else jnp.concatenate(pc, axis=0) for pc in pieces]
        if bi > 0:
            bref = bcum[r0 - 1:r0]
            qsc = qi * jnp.exp(bb - bref)
            ksc = k[:r0] * jnp.exp(bref - bcum[:r0])
            sc = [_mm3(qsc[:, ksl[hl]], ksc[:, ksl[hl]], ((1,), (1,))) for hl in hs]
            acc = [acc[hl] + _mm(sc[hl], v[:r0, vsl[hl]]) for hl in hs]
        for hl in hs:
            blocks[hl].append(acc[hl])
    b_last = bcum[c - 1:c]
    kd = k * jnp.exp(b_last - bcum)
    e_last = jnp.exp(b_last)
    kv = [_mm_tn(kd[:, ksl[hl]], v[:, vsl[hl]]) for hl in hs]
    states = []
    outs = []
    og = og_ref[0]
    for hl in hs:
        e_col = jnp.sum(jnp.where(eye, e_last[:, ksl[hl]], 0.0), axis=1, keepdims=True)
        states.append(e_col * s_old[hl] + kv[hl])
        o = blocks[hl][0] if nb == 1 else jnp.concatenate(blocks[hl], axis=0)
        outs.append(_head_rmsnorm(o, nw_ref[...]) * _sigmoid(og[:, vsl[hl]]))
    o_ref[0] = (outs[0] if hb == 1 else jnp.concatenate(outs, axis=1)).astype(o_ref.dtype)
    s_ref[0] = jnp.stack(states, axis=0)


def _hgrn2(z, s0, p, o, layer, chunk):
    b, l, zw = z.shape
    dv = p["hg_norm_w"].shape[1]
    depth, kw = p["hg_lb_logits"].shape
    vw = (zw - 2 * kw) // 2
    heads = vw // dv
    dk = kw // heads
    hb = _pick_tile(heads, HGRN_HEADS_PER_STEP, 1)
    nhb = heads // hb
    wk, wv = hb * dk, hb * dv
    col = lambda w_, off: pl.BlockSpec((1, chunk, w_), lambda i, h, t: (i, t, off + h))
    st = pl.BlockSpec((1, hb, dk, dv), lambda i, h, t: (i, h, 0, 0))
    out, s_fin = pl.pallas_call(
        functools.partial(_hgrn_kernel, hb=hb, dk=dk, dv=dv, layer=layer),
        grid=(b, nhb, l // chunk),
        in_specs=[col(wk, 0), col(wk, kw // wk), col(wv, 2 * kw // wv), col(wv, (2 * kw + vw) // wv),
                  pl.BlockSpec((depth, wk), lambda i, h, t: (0, h)),
                  pl.BlockSpec((1, dv), lambda i, h, t: (0, 0)),
                  st],
        out_specs=[pl.BlockSpec((1, chunk, wv), lambda i, h, t: (i, t, h)), st],
        out_shape=[jax.ShapeDtypeStruct((b, l, vw), BF16), jax.ShapeDtypeStruct((b, heads, dk, dv), F32)],
        compiler_params=_params(("arbitrary", "arbitrary", "arbitrary")),
        name="hgrn2",
    )(z, z, z, z, p["hg_lb_logits"].astype(F32), p["hg_norm_w"][o:o + 1], s0.astype(F32))
    return out, s_fin.astype(s0.dtype)


def _trunk(xs, mods, sts, p, prep):
    groups = (0, 1)
    dims = [x.shape[:2] for x in xs]
    d = xs[0].shape[2]
    rows = [b * l for b, l in dims]
    depth = p["w_mod"].shape[0]
    d_ff = p["w_ffn_out"].shape[2]
    chunks = [min(CHUNK, l) for _, l in dims]
    new_st = [[[] for _ in range(7)] for _ in groups]
    x2 = [x.reshape(r, d) for x, r in zip(xs, rows)]

    def mm(x_pair, w, prefix, cols, tn, *, epi="plain", res_pair=None, mod9s=None, idx=None, **kw):
        res2 = gate = gate2 = None
        if epi == "res":
            gate = mod9s[0][:, 3 * idx + 2].reshape(dims[0][0], 1, d)
            gate2 = jnp.repeat(mod9s[1][:, 3 * idx + 2], dims[1][1], axis=0).reshape(1, rows[1], d)
            res2 = res_pair[1]
        return _matmul(x_pair[0], w, prefix, cols, tn, epi=epi, rows_per_gate=dims[0][1],
                       res=None if res_pair is None else res_pair[0], gate=gate,
                       rider=(x_pair[1], res2, gate2), **kw)

    def norm(x2, g, mod9s, idx):
        return [_norm(x2[i].reshape(*dims[i], d), g, mod9s[i], idx).reshape(rows[i], d) for i in groups]

    def ffn(x2, mod9s, lyr, idx, slot):
        h = norm(x2, p["norm_g"][lyr, idx], mod9s, idx)
        tn = _pick_tile(d_ff, 256)
        act = mm(h, p["w_ffn_in"], (lyr, slot), ((0, d_ff // tn), d_ff // tn), tn,
                 epi="swiglu", out_dtype=BF16, tm_pref=1024)
        tn2 = _pick_tile(d, 512)
        return mm(act, p["w_ffn_out"], (lyr, slot), ((0,), d // tn2), tn2, epi="res",
                  res_pair=x2, mod9s=mod9s, idx=idx, coef=0.5, w_buffers=1)

    for lyr in range(depth):
        mod9s = [m[lyr] for m in mods]
        x2 = ffn(x2, mod9s, lyr, 0, 0)
        h = norm(x2, p["norm_g"][lyr, 1], mod9s, 1)
        mix = [None, None]
        if lyr % 2 == 0:
            e = lyr // 2
            even_in = p["w_in_even"].shape[2]
            rw_in = p["rw_mu"].shape[1]
            tn = _pick_tile(even_in, 1280, 256)
            q = prep["even"][e]
            zin = mm(h, q["w_in"], (), ((0,), even_in // tn), tn, w_buffers=1)
            for i in groups:
                rw_s, rw_sh, s5_re, s5_im = sts[i][:4]
                z = zin[i].reshape(*dims[i], even_in)
                oa, sh, s = _rwkv7(z, rw_in, rw_sh[e], rw_s[e], q)
                ob, hr, hi = _s5(z, rw_in, s5_re[e], s5_im[e], p, e)
                for slot, val in zip((0, 1, 2, 3), (s, sh, hr, hi)):
                    new_st[i][slot].append(val)
                mix[i] = (oa.reshape(rows[i], -1), ob.reshape(rows[i], -1))
            w_out, widx = q["w_out"], ()
        else:
            o = lyr // 2
            w_gdn, w_hg, w_ab = prep["odd"][o]
            gdn_main = w_gdn.shape[1]
            tn = _pick_tile(gdn_main, 512)
            zmain = mm(h, w_gdn, (), ((0,), gdn_main // tn), tn, tm_pref=1024)
            zab = mm(h, w_ab, (), ((0,), 1), LANES)
            hg_in = w_hg.shape[1]
            tn = _pick_tile(hg_in, 512)
            zhg = mm(h, w_hg, (), ((0,), hg_in // tn), tn, tm_pref=1024)
            for i in groups:
                gdn_s, gdn_cv, hg_s = sts[i][4:]
                oc, cv, s = _gdn(zmain[i].reshape(*dims[i], gdn_main), zab[i].reshape(*dims[i], LANES),
                                 gdn_cv[o], gdn_s[o], p, o, chunks[i])
                od, sh_ = _hgrn2(zhg[i].reshape(*dims[i], hg_in), hg_s[o], p, o, lyr, chunks[i])
                for slot, val in zip((4, 5, 6), (s, cv, sh_)):
                    new_st[i][slot].append(val)
                mix[i] = (oc.reshape(rows[i], -1), od.reshape(rows[i], -1))
            w_out, widx = p["w_out_odd"], (o,)
        tn = _pick_tile(d, 512)
        x2 = mm(mix, w_out, widx, ((0,), d // tn), tn, epi="res", res_pair=x2, mod9s=mod9s, idx=1,
                coef=1.0, tm_pref=1024)
        x2 = ffn(x2, mod9s, lyr, 2, 1)
    ys = [_norm(x2[i].reshape(*dims[i], d), p["norm_final"], out_dtype=xs[i].dtype) for i in groups]
    return ys, [tuple(jnp.stack(v) for v in new_st[i]) for i in groups]


def kernel(x_prompt, x_sample, c_prompt, c_sample, state_rwkv, state_rwkv_shift, state_s5_re, state_s5_im, state_gdn, cache_gdn_conv, state_hgrn, w_mod, b_mod, norm_g, norm_final, w_ffn_in, w_ffn_out, w_in_even, w_out_even, rw_mu, rw_w0, rw_w_up, rw_a0, rw_a_up, rw_g_up, rw_k_k, rw_k_a, rw_r_k, rw_ln_w, rw_ln_b, s5_lambda_re, s5_lambda_im, s5_log_dt, s5_b_re, s5_b_im, s5_c_re, s5_c_im, s5_d, s5_glu_w, s5_glu_b, w_in_odd, w_out_odd, gdn_conv_w, gdn_a_log, gdn_dt_bias, gdn_norm_w, hg_lb_logits, hg_norm_w):
    p = dict(w_mod=w_mod, b_mod=b_mod, norm_g=norm_g, norm_final=norm_final,
             w_ffn_in=w_ffn_in, w_ffn_out=w_ffn_out, w_in_even=w_in_even, w_out_even=w_out_even,
             rw_mu=rw_mu, rw_w0=rw_w0, rw_w_up=rw_w_up, rw_a0=rw_a0, rw_a_up=rw_a_up, rw_g_up=rw_g_up,
             rw_k_k=rw_k_k, rw_k_a=rw_k_a, rw_r_k=rw_r_k, rw_ln_w=rw_ln_w, rw_ln_b=rw_ln_b,
             s5_lambda_re=s5_lambda_re, s5_lambda_im=s5_lambda_im, s5_log_dt=s5_log_dt,
             s5_b_re=s5_b_re, s5_b_im=s5_b_im, s5_c_re=s5_c_re, s5_c_im=s5_c_im, s5_d=s5_d,
             s5_glu_w=s5_glu_w, s5_glu_b=s5_glu_b, w_in_odd=w_in_odd, w_out_odd=w_out_odd,
             gdn_conv_w=gdn_conv_w, gdn_a_log=gdn_a_log, gdn_dt_bias=gdn_dt_bias, gdn_norm_w=gdn_norm_w,
             hg_lb_logits=hg_lb_logits, hg_norm_w=hg_norm_w)
    bp, bs = x_prompt.shape[0], x_sample.shape[0]
    d = x_prompt.shape[-1]
    depth = w_mod.shape[0]
    n_even, n_odd = (depth + 1) // 2, depth // 2
    dtp = x_prompt.dtype

    bc = bp + bs
    bc_pad = -(-bc // 16) * 16
    c_all = jnp.pad(jnp.concatenate([c_prompt, c_sample], axis=0), ((0, bc_pad - bc), (0, 0)))
    mod = _modulation(c_all, w_mod, b_mod).reshape(depth, bc_pad, 9, d)
    mod_p, mod_s = mod[:, :bp], mod[:, bp:bc]

    heads = gdn_a_log.shape[1]
    conv_ch = gdn_conv_w.shape[2]
    vw = heads * gdn_norm_w.shape[1]
    gdn_main = conv_ch + vw
    gdn_in = gdn_main + 2 * heads
    prep = {"odd": [], "even": [_rw_prep(p, e) for e in range(n_even)]}
    for o in range(n_odd):
        w_gdn = w_in_odd[o][:, :gdn_main]
        w_hg = w_in_odd[o][:, gdn_in:]
        w_ab = jnp.pad(w_in_odd[o][:, gdn_main:gdn_in], ((0, 0), (0, LANES - 2 * heads)))
        prep["odd"].append((w_gdn, w_hg, w_ab))

    rw_h, rw_n = rw_r_k.shape[1], rw_r_k.shape[2]
    rw_in = rw_mu.shape[1]
    s5_g, s5_n = s5_lambda_re.shape[1], s5_lambda_re.shape[2]
    gdn_dk = (conv_ch - vw) // 2 // heads
    gdn_dv = gdn_norm_w.shape[1]
    kconv = gdn_conv_w.shape[1]
    hg_dv = hg_norm_w.shape[1]
    hg_kw = hg_lb_logits.shape[1]
    hg_vw = (w_in_odd.shape[2] - gdn_in - 2 * hg_kw) // 2
    hg_h = hg_vw // hg_dv
    hg_dk = hg_kw // hg_h
    st_prompt = (jnp.zeros((n_even, bp, rw_h, rw_n, rw_n), dtp),
                 jnp.zeros((n_even, bp, rw_in), dtp),
                 jnp.zeros((n_even, bp, s5_g, s5_n), dtp),
                 jnp.zeros((n_even, bp, s5_g, s5_n), dtp),
                 jnp.zeros((n_odd, bp, heads, gdn_dk, gdn_dv), dtp),
                 jnp.zeros((n_odd, bp, kconv - 1, conv_ch), dtp),
                 jnp.zeros((n_odd, bp, hg_h, hg_dk, hg_dv), dtp))
    st_sample = (state_rwkv, state_rwkv_shift, state_s5_re, state_s5_im, state_gdn, cache_gdn_conv, state_hgrn)
    ys, ns = _trunk((x_prompt, x_sample), (mod_p, mod_s), (st_prompt, st_sample), p, prep)
    return tuple(ys) + tuple(ns[0]) + tuple(ns[1])
```

```python
import functools

import jax
import jax.numpy as jnp
from jax import lax
from jax.experimental import pallas as pl
from jax.experimental.pallas import tpu as pltpu

F32 = jnp.float32
BF16 = jnp.bfloat16
HI = lax.Precision.HIGHEST

EPS = 1e-6
RW_GN_EPS = 64e-5
L2_EPS = 1e-12
CHUNK = 64
SUB = 16
HGRN_HEADS_PER_STEP = 16
LANES = 128
SUBLANES = 8
VMEM_LIMIT = 56 * 1024 * 1024


def _params(sem, big=False):
    return pltpu.CompilerParams(dimension_semantics=sem,
                                vmem_limit_bytes=VMEM_LIMIT if big else None)


def _pick_tile(n, pref, mult=LANES):
    t = (min(pref, n) // mult) * mult
    while t >= mult:
        if n % t == 0:
            return t
        t -= mult
    return n


def _mm(a, b, hi=False):
    if not hi:
        a, b = a.astype(BF16), b.astype(BF16)
    return lax.dot_general(a, b, (((1,), (0,)), ((), ())), precision=HI if hi else None,
                           preferred_element_type=F32)


def _mm_nt(a, b, hi=False):
    if not hi:
        a, b = a.astype(BF16), b.astype(BF16)
    return lax.dot_general(a, b, (((1,), (1,)), ((), ())), precision=HI if hi else None,
                           preferred_element_type=F32)


def _mm_tn(a, b, hi=False):
    if not hi:
        a, b = a.astype(BF16), b.astype(BF16)
    return lax.dot_general(a, b, (((0,), (0,)), ((), ())), precision=HI if hi else None,
                           preferred_element_type=F32)


def _split2(a):
    hi = a.astype(BF16)
    return hi, (a - hi.astype(F32)).astype(BF16)


def _split3(a):
    hi = a.astype(BF16)
    r = a - hi.astype(F32)
    mid = r.astype(BF16)
    return hi, mid, (r - mid.astype(F32)).astype(BF16)


def _bdot(a, b, dims):
    return lax.dot_general(a, b, (dims, ((), ())), preferred_element_type=F32)


def _mm3(a, b, dims=((1,), (0,))):
    ah, al = _split2(a)
    bh, bl = _split2(b)
    return _bdot(ah, bh, dims) + (_bdot(ah, bl, dims) + _bdot(al, bh, dims))


def _mm01(mask, x):
    m = mask.astype(BF16)
    xh, xm, xl = _split3(x)
    dims = ((1,), (0,))
    return _bdot(m, xh, dims) + (_bdot(m, xm, dims) + _bdot(m, xl, dims))


def _x01(x, mask):
    m = mask.astype(BF16)
    xh, xm, xl = _split3(x)
    dims = ((1,), (0,))
    return _bdot(xh, m, dims) + (_bdot(xm, m, dims) + _bdot(xl, m, dims))


def _sigmoid(x):
    return 1.0 / (1.0 + jnp.exp(-x))


def _silu(x):
    return x * _sigmoid(x)


def _softplus(x):
    return jnp.maximum(x, 0.0) + jnp.log(1.0 + jnp.exp(-jnp.abs(x)))


def _iota2(shape, axis):
    return lax.broadcasted_iota(jnp.int32, shape, axis)


def _mod_kernel(c_ref, w_ref, b_ref, o_ref):
    s = _silu(c_ref[...]).astype(BF16)
    o_ref[0] = jnp.dot(s, w_ref[0].astype(BF16), preferred_element_type=F32) + b_ref[0]


def _modulation(c, w_mod, b_mod):
    depth, d, n = w_mod.shape
    bc = c.shape[0]
    tn = _pick_tile(n, 512)
    return pl.pallas_call(
        _mod_kernel,
        grid=(depth, n // tn),
        in_specs=[pl.BlockSpec((bc, d), lambda l, j: (0, 0)),
                  pl.BlockSpec((1, d, tn), lambda l, j: (l, 0, j)),
                  pl.BlockSpec((1, 1, tn), lambda l, j: (l, 0, j))],
        out_specs=pl.BlockSpec((1, bc, tn), lambda l, j: (l, 0, j)),
        out_shape=jax.ShapeDtypeStruct((depth, bc, n), F32),
        compiler_params=_params(("arbitrary", "arbitrary"), big=True),
        name="modulation",
    )(c, w_mod, b_mod.reshape(depth, 1, n))


def _norm_kernel(x_ref, g_ref, *rest, idx):
    if idx is None:
        (o_ref,) = rest
    else:
        m_ref, o_ref = rest
    x = x_ref[0]
    y = x * lax.rsqrt(jnp.mean(x * x, axis=-1, keepdims=True) + EPS) * g_ref[...]
    if idx is not None:
        shift = m_ref[0, pl.ds(3 * idx, 1), :]
        scale = m_ref[0, pl.ds(3 * idx + 1, 1), :]
        y = y * (1.0 + scale) + shift
    o_ref[0] = y.astype(o_ref.dtype)


def _norm(x, g, mod9=None, idx=None, out_dtype=BF16):
    b, l, d = x.shape
    tl = _pick_tile(l, 256, SUBLANES)
    in_specs = [pl.BlockSpec((1, tl, d), lambda i, t: (i, t, 0)),
                pl.BlockSpec((1, d), lambda i, t: (0, 0))]
    args = [x, g.reshape(1, d)]
    if idx is not None:
        in_specs.append(pl.BlockSpec((1, mod9.shape[1], d), lambda i, t: (i, 0, 0)))
        args.append(mod9)
    return pl.pallas_call(
        functools.partial(_norm_kernel, idx=idx),
        grid=(b, l // tl),
        in_specs=in_specs,
        out_specs=pl.BlockSpec((1, tl, d), lambda i, t: (i, t, 0)),
        out_shape=jax.ShapeDtypeStruct((b, l, d), out_dtype),
        compiler_params=_params(("arbitrary", "arbitrary")),
        name="norm",
    )(*args)


def _mm_kernel(*refs, n_x, n_w, epi, coef, rider):
    it = iter(refs)
    take = lambda n: [next(it) for _ in range(n)]
    x_refs, w_refs = take(n_x), take(n_w)
    res_ref, gate_ref = take(2) if epi == "res" else (None, None)
    if rider:
        x2_refs = take(n_x)
        res2_ref, gate2_ref = take(2) if epi == "res" else (None, None)
    (o_ref,) = take(1)
    if rider:
        (o2_ref,) = take(1)
    wb_refs = take(n_w)

    def rows(xr, res_r, gate_r, out_r):
        acc = []
        for wb_ref in wb_refs:
            k0, a = 0, None
            for x_ref in xr:
                kx = x_ref.shape[1]
                part = jnp.dot(x_ref[...], wb_ref[pl.ds(k0, kx), :], preferred_element_type=F32)
                a = part if a is None else a + part
                k0 += kx
            acc.append(a)
        if epi == "swiglu":
            out = _silu(acc[0]) * acc[1]
        elif epi == "res":
            out = res_r[...] + coef * gate_r[0] * acc[0]
        else:
            out = acc[0]
        out_r[...] = out.astype(out_r.dtype)

    @pl.when(pl.program_id(1) == 0)
    def _():
        for w_ref, wb_ref in zip(w_refs, wb_refs):
            wb_ref[...] = w_ref[...].astype(BF16)
        if rider:
            rows(x2_refs, res2_ref, gate2_ref, o2_ref)

    rows(x_refs, res_ref, gate_ref, o_ref)


def _matmul(x, w, prefix, col_blocks, tn, *, rows_per_gate, res=None, gate=None, rider=None,
            coef=1.0, epi="plain", out_dtype=F32, tm_pref=512, w_buffers=2):
    xs = x if isinstance(x, (tuple, list)) else (x,)
    m = xs[0].shape[0]
    k = sum(t.shape[1] for t in xs)
    offs, n_cols = col_blocks
    n_w = len(offs)
    n_out = n_cols * tn
    tm = _pick_tile(rows_per_gate, tm_pref, SUBLANES)
    rpt = rows_per_gate // tm
    npre = len(prefix)
    wmode = {} if w_buffers == 2 else {"pipeline_mode": pl.Buffered(w_buffers)}
    in_specs = [pl.BlockSpec((tm, t.shape[1]), lambda j, i: (i, 0)) for t in xs]
    args = list(xs)
    for off in offs:
        in_specs.append(pl.BlockSpec((None,) * npre + (k, tn),
                                     lambda j, i, off=off: tuple(prefix) + (0, j + off), **wmode))
        args.append(w)
    if epi == "res":
        in_specs += [pl.BlockSpec((tm, tn), lambda j, i: (i, j)),
                     pl.BlockSpec((1, 1, tn), lambda j, i: (i // rpt, 0, j))]
        args += [res, gate]
    out_specs = [pl.BlockSpec((tm, tn), lambda j, i: (i, j))]
    out_shape = [jax.ShapeDtypeStruct((m, n_out), out_dtype)]
    if rider is not None:
        x2, res2, gate2 = rider
        x2s = x2 if isinstance(x2, (tuple, list)) else (x2,)
        m2 = x2s[0].shape[0]
        in_specs += [pl.BlockSpec((m2, t.shape[1]), lambda j, i: (0, 0)) for t in x2s]
        args += list(x2s)
        if epi == "res":
            in_specs += [pl.BlockSpec((m2, tn), lambda j, i: (0, j)),
                         pl.BlockSpec((1, m2, tn), lambda j, i: (0, 0, j))]
            args += [res2, gate2]
        out_specs.append(pl.BlockSpec((m2, tn), lambda j, i: (0, j)))
        out_shape.append(jax.ShapeDtypeStruct((m2, n_out), out_dtype))
    outs = pl.pallas_call(
        functools.partial(_mm_kernel, n_x=len(xs), n_w=n_w, epi=epi, coef=coef, rider=rider is not None),
        grid=(n_cols, m // tm),
        in_specs=in_specs,
        out_specs=out_specs,
        out_shape=out_shape,
        scratch_shapes=[pltpu.VMEM((k, tn), BF16) for _ in range(n_w)],
        compiler_params=_params(("arbitrary", "arbitrary"), big=True),
        name="matmul_" + epi,
    )(*args)
    return tuple(outs) if rider is not None else (outs[0], None)


def _group_ones(n):
    r = _iota2((LANES, LANES), 0)
    c = _iota2((LANES, LANES), 1)
    sh = n.bit_length() - 1
    return (jnp.right_shift(r, sh) == jnp.right_shift(c, sh)).astype(F32)


def _group_sum(x, n):
    if n % LANES == 0:
        parts = []
        for h in range(x.shape[1] // n):
            s = jnp.sum(x[:, h * n:(h + 1) * n], axis=-1, keepdims=True)
            parts.append(jnp.broadcast_to(s, (x.shape[0], n)))
        return parts[0] if len(parts) == 1 else jnp.concatenate(parts, axis=1)
    ones = _group_ones(n)
    parts = [_x01(x[:, s * LANES:(s + 1) * LANES], ones) for s in range(x.shape[1] // LANES)]
    return parts[0] if len(parts) == 1 else jnp.concatenate(parts, axis=1)


def _tril(c, strict=False):
    r = _iota2((c, c), 0)
    s = _iota2((c, c), 1)
    return (r > s) if strict else (r >= s)


def _unit_lower_solve(lms, rhss):
    c = lms[0].shape[0]
    r = _iota2((c, c), 0)
    s = _iota2((c, c), 1)
    eye = (r == s).astype(F32)
    if c <= SUB:
        lds, los = lms, None
    else:
        sh = SUB.bit_length() - 1
        same = jnp.right_shift(r, sh) == jnp.right_shift(s, sh)
        lds = [jnp.where(same, lm, 0.0) for lm in lms]
        los = [lm - ld for lm, ld in zip(lms, lds)]
    order = min(SUB, c)
    xs = [-ld for ld in lds]
    ps = [eye + x for x in xs]
    pws = [_mm(x, x) for x in xs]
    k = 2
    while k < order:
        new_ps = [p + _mm(pw, p) for p, pw in zip(ps, pws)]
        if 2 * k < order:
            pws = [_mm(pw, pw) for pw in pws]
        ps = new_ps
        k *= 2
    levels = []
    if los is not None:
        pws = [-_mm(p, lo) for p, lo in zip(ps, los)]
        k = 1
        while True:
            levels.append(pws)
            if 2 * k >= c // SUB:
                break
            pws = [_mm(pw, pw) for pw in pws]
            k *= 2

    def apply(vs):
        sols = [_mm(p, v) for p, v in zip(ps, vs)]
        for lvl in levels:
            sols = [sol + _mm(pw, sol) for sol, pw in zip(sols, lvl)]
        return sols

    sols = apply(rhss)
    resid = [rhs - (sol + _mm3(lm, sol)) for rhs, sol, lm in zip(rhss, sols, lms)]
    return [sol + cor for sol, cor in zip(sols, apply(resid))]


def _head_rmsnorm(o, w):
    return o * lax.rsqrt(jnp.mean(o * o, axis=-1, keepdims=True) + EPS) * w


def _token_shift(z, prev_row, mu):
    prev = jnp.where(_iota2(z.shape, 0) == 0, prev_row, pltpu.roll(z, 1, 0))
    return z + mu * (prev - z)


def _rw_pre_kernel(z_ref, sh_ref, mu_ref, w0_ref, a0_ref, wup_ref, aup_ref, gup_ref,
                   d_o, a_o, g_o, prev_scr, *, w_lora, a_lora):
    @pl.when(pl.program_id(1) == 0)
    def _():
        prev_scr[...] = sh_ref[0]

    tl = z_ref.shape[1]
    zs = _token_shift(z_ref[0], prev_scr[...], mu_ref[...])
    prev_scr[...] = z_ref[0, pl.ds(tl - 1, 1), :]
    xw, xa, xg = zs[:, :w_lora], zs[:, w_lora:w_lora + a_lora], zs[:, w_lora + a_lora:]
    w = -_softplus(-(w0_ref[...] + _mm(jnp.tanh(xw), wup_ref[...]))) - 0.5
    d_o[0] = jnp.exp(-jnp.exp(w))
    a_o[0] = _sigmoid(a0_ref[...] + _mm(xa, aup_ref[...]))
    g_o[0] = _mm(_sigmoid(xg), gup_ref[...])


def _rw_pre(zin, rw_in, shift, q):
    b, l, _ = zin.shape
    width = q["w0"].shape[1]
    w_lora, a_lora, g_lora = q["w_up"].shape[0], q["a_up"].shape[0], q["g_up"].shape[0]
    lw = rw_in - 3 * width
    assert (3 * width) % lw == 0 and lw % LANES == 0
    cb = 3 * width // lw
    tl = _pick_tile(l, 256, SUBLANES)
    row = lambda n: pl.BlockSpec((1, n), lambda i, t: (0, 0))
    full = lambda r: pl.BlockSpec((r, width), lambda i, t: (0, 0))
    out_spec = pl.BlockSpec((1, tl, width), lambda i, t: (i, t, 0))
    out_shape = jax.ShapeDtypeStruct((b, l, width), F32)
    return pl.pallas_call(
        functools.partial(_rw_pre_kernel, w_lora=w_lora, a_lora=a_lora),
        grid=(b, l // tl),
        in_specs=[pl.BlockSpec((1, tl, lw), lambda i, t: (i, t, cb)),
                  pl.BlockSpec((1, 1, lw), lambda i, t: (i, 0, 0)),
                  row(lw), row(width), row(width), full(w_lora), full(a_lora), full(g_lora)],
        out_specs=[out_spec] * 3,
        out_shape=[out_shape] * 3,
        scratch_shapes=[pltpu.VMEM((1, lw), F32)],
        compiler_params=_params(("arbitrary", "arbitrary"), big=True),
        name="rwkv_pre",
    )(zin, shift[:, 3 * width:].reshape(b, 1, lw), q["mu"][:, 3 * width:], q["w0"], q["a0"],
      q["w_up"], q["a_up"], q["g_up"])


_RW_R, _RW_D, _RW_K, _RW_V, _RW_KK, _RW_NKKA = range(6)


def _rw_scan_kernel(zr_ref, zk_ref, zv_ref, d_ref, a_ref, g_ref, mu_ref, sh_ref, par_ref, s0_ref,
                    out_ref, s_ref, zs_scr, prev_scr, seq_scr, o_scr, bs_scr,
                    *, steps, pitch, n, heads, bpl):
    @pl.when(pl.program_id(1) == 0)
    def _():
        s_ref[...] = s0_ref[...]
        prev_scr[...] = sh_ref[...]

    used = bpl * heads
    width = n * heads
    for qi, z_ref in enumerate((zr_ref, zk_ref, zv_ref)):
        cols = slice(qi * width, (qi + 1) * width)
        for b in range(bpl):
            zs_scr[qi, b] = _token_shift(z_ref[b], prev_scr[b, :, cols], mu_ref[:, cols])
            prev_scr[b, :, cols] = z_ref[b, pl.ds(steps - 1, 1), :]

    def to_lanes(ref, j):
        pieces = [ref[b, :, j * heads:(j + 1) * heads] for b in range(bpl)]
        if used < LANES:
            pieces.append(jnp.zeros((steps, LANES - used), F32))
        return pieces[0] if len(pieces) == 1 else jnp.concatenate(pieces, axis=1)

    r_ref, k_ref, v_ref = zs_scr.at[0], zs_scr.at[1], zs_scr.at[2]
    kkp, kap, rkp, lnw, lnb = (par_ref[i] for i in range(5))
    sumsq = bsum = None
    for j in range(n):
        rows = pl.ds(j * pitch, steps)
        r, k, a = to_lanes(r_ref, j), to_lanes(k_ref, j), to_lanes(a_ref, j)
        kraw = k * kkp[j:j + 1]
        k2 = k * (1.0 + (a - 1.0) * kap[j:j + 1])
        seq_scr[_RW_R, rows, :] = r
        seq_scr[_RW_D, rows, :] = to_lanes(d_ref, j)
        seq_scr[_RW_V, rows, :] = to_lanes(v_ref, j)
        seq_scr[_RW_K, rows, :] = k2
        seq_scr[_RW_KK, rows, :] = kraw
        seq_scr[_RW_NKKA, rows, :] = a
        sq = kraw * kraw
        bt = r * k2 * rkp[j:j + 1]
        sumsq = sq if sumsq is None else sumsq + sq
        bsum = bt if bsum is None else bsum + bt
    inv = lax.rsqrt(sumsq + L2_EPS)
    bs_scr[...] = bsum
    for j in range(n):
        rows = pl.ds(j * pitch, steps)
        kk = seq_scr[_RW_KK, rows, :] * inv
        seq_scr[_RW_KK, rows, :] = kk
        seq_scr[_RW_NKKA, rows, :] = -(kk * seq_scr[_RW_NKKA, rows, :])

    def row(q, j, t):
        return seq_scr[q, pl.ds(j * pitch + t, 1), :]

    sa0 = s_ref[0] * row(_RW_KK, 0, 0)
    for j in range(1, n):
        sa0 = sa0 + s_ref[j] * row(_RW_KK, j, 0)

    def step(t, sa):
        tn = jnp.minimum(t + 1, steps - 1)
        tile = pl.ds(t, n, stride=pitch)
        v_t = seq_scr[_RW_V, tile, :]
        o = None
        sa_next = None
        for j in range(n):
            sj = s_ref[j] * row(_RW_D, j, t) + sa * row(_RW_NKKA, j, t) + v_t * row(_RW_K, j, t)
            s_ref[j] = sj
            term = sj * row(_RW_R, j, t)
            o = term if o is None else o + term
            nxt = sj * row(_RW_KK, j, tn)
            sa_next = nxt if sa_next is None else sa_next + nxt
        mu = jnp.mean(o, axis=0, keepdims=True)
        c = o - mu
        var = jnp.mean(c * c, axis=0, keepdims=True)
        y = c * lax.rsqrt(var + RW_GN_EPS) * lnw + lnb
        o_scr[tile, :] = y + bs_scr[pl.ds(t, 1), :] * v_t
        return sa_next

    lax.fori_loop(0, steps, step, sa0)

    per = LANES // heads
    for b in range(bpl):
        for qd in range(n // per):
            pieces = [o_scr[pl.ds(i * pitch, steps), :][:, b * heads:(b + 1) * heads]
                      for i in range(qd * per, (qd + 1) * per)]
            slab = pieces[0] if per == 1 else jnp.concatenate(pieces, axis=1)
            cols = slice(qd * LANES, (qd + 1) * LANES)
            out_ref[b, :, cols] = (slab * g_ref[b, :, cols]).astype(out_ref.dtype)


def _rw_scan(zin, d, a, g, mu3, sh3, par, s0, heads, bpl):
    b, l, width = g.shape
    n = width // heads
    steps = _pick_tile(l, 32, 2 * SUBLANES)
    pitch = steps + SUBLANES
    seq_spec = pl.BlockSpec((bpl, steps, width), lambda c, t: (c, t, 0))
    z_spec = lambda qi: pl.BlockSpec((bpl, steps, width), lambda c, t: (c, t, qi))
    st_spec = pl.BlockSpec((n, n, LANES), lambda c, t: (0, 0, c))
    return pl.pallas_call(
        functools.partial(_rw_scan_kernel, steps=steps, pitch=pitch, n=n, heads=heads, bpl=bpl),
        grid=(b // bpl, l // steps),
        in_specs=[z_spec(0), z_spec(1), z_spec(2), seq_spec, seq_spec, seq_spec,
                  pl.BlockSpec((1, 3 * width), lambda c, t: (0, 0)),
                  pl.BlockSpec((bpl, 1, 3 * width), lambda c, t: (c, 0, 0)),
                  pl.BlockSpec((5, n, LANES), lambda c, t: (0, 0, 0)), st_spec],
        out_specs=[seq_spec, st_spec],
        out_shape=[jax.ShapeDtypeStruct((b, l, width), BF16), jax.ShapeDtypeStruct(s0.shape, F32)],
        scratch_shapes=[pltpu.VMEM((3, bpl, steps, width), F32), pltpu.VMEM((bpl, 1, 3 * width), F32),
                        pltpu.VMEM((6, n * pitch, LANES), F32), pltpu.VMEM((n * pitch, LANES), F32),
                        pltpu.VMEM((steps, LANES), F32)],
        compiler_params=_params(("arbitrary", "arbitrary"), big=True),
        name="rwkv_scan",
    )(zin, zin, zin, d, a, g, mu3, sh3, par, s0)


def _index_major(t, heads, n, blocks=1, inverse=False):
    width = heads * n
    a, c = (n, heads) if inverse else (heads, n)
    parts = []
    for i in range(blocks):
        blk = t[..., i * width:(i + 1) * width]
        parts.append(blk.reshape(blk.shape[:-1] + (a, c)).swapaxes(-1, -2).reshape(blk.shape))
    if t.shape[-1] > blocks * width:
        parts.append(t[..., blocks * width:])
    return parts[0] if len(parts) == 1 else jnp.concatenate(parts, axis=-1)


def _rw_prep(p, e):
    heads, n = p["rw_r_k"].shape[1], p["rw_r_k"].shape[2]
    im = functools.partial(_index_major, heads=heads, n=n)
    w_out = p["w_out_even"][e]
    return dict(heads=heads, n=n,
                w_in=im(p["w_in_even"][e], blocks=3), w_out=jnp.swapaxes(im(jnp.swapaxes(w_out, 0, 1)), 0, 1),
                mu=im(p["rw_mu"][e:e + 1], blocks=3), w0=im(p["rw_w0"][e:e + 1]), a0=im(p["rw_a0"][e:e + 1]),
                w_up=im(p["rw_w_up"][e]), a_up=im(p["rw_a_up"][e]), g_up=im(p["rw_g_up"][e]),
                lane=[t.reshape(heads, n).T for t in (p["rw_k_k"][e], p["rw_k_a"][e], p["rw_r_k"][e],
                                                      p["rw_ln_w"][e], p["rw_ln_b"][e])])


def _rwkv7(zin, rw_in, shift, s0, q):
    b, l, _ = zin.shape
    heads, n = q["heads"], q["n"]
    bpl = min(b, LANES // heads)
    nlb, used = b // bpl, bpl * heads
    width = heads * n
    shift_im = _index_major(shift.astype(F32), heads, n, blocks=3)
    d, a, g = _rw_pre(zin, rw_in, shift_im, q)
    par = jnp.stack([jnp.pad(jnp.tile(t, (1, bpl)), ((0, 0), (0, LANES - used))) for t in q["lane"]])
    s0_l = s0.astype(F32).transpose(3, 2, 0, 1).reshape(n, n, nlb, used)
    s0_l = jnp.pad(s0_l, ((0, 0), (0, 0), (0, 0), (0, LANES - used))).reshape(n, n, nlb * LANES)
    out, s_l = _rw_scan(zin, d, a, g, q["mu"][:, :3 * width], shift_im[:, :3 * width].reshape(b, 1, 3 * width),
                        par, s0_l, heads, bpl)
    s_fin = s_l.reshape(n, n, nlb, LANES)[..., :used].reshape(n, n, b, heads).transpose(2, 3, 1, 0)
    new_shift = _index_major(zin[:, -1, :rw_in], heads, n, blocks=3, inverse=True)
    return out, new_shift.astype(shift.dtype), s_fin.astype(s0.dtype)


def _s5_param_kernel(ldt_ref, lr_ref, li_ref, bre_ref, bim_ref, are_o, aim_o, bbre_o, bbim_o):
    dt = jnp.exp(ldt_ref[...])
    lr, li = lr_ref[...], li_ref[...]
    mag = jnp.exp(lr * dt)
    ab_re, ab_im = mag * jnp.cos(li * dt), mag * jnp.sin(li * dt)
    den = lr * lr + li * li
    pr, pi_ = ab_re - 1.0, ab_im
    coef_re = (pr * lr + pi_ * li) / den
    coef_im = (pi_ * lr - pr * li) / den
    are_o[...] = ab_re
    aim_o[...] = ab_im
    for m in range(bre_ref.shape[0]):
        b_re, b_im = bre_ref[m], bim_ref[m]
        bbre_o[m] = coef_re * b_re - coef_im * b_im
        bbim_o[m] = coef_re * b_im + coef_im * b_re


def _s5_params(p, e):
    g, n, m = p["s5_b_re"].shape[1:]
    outs = pl.pallas_call(
        _s5_param_kernel,
        out_shape=[jax.ShapeDtypeStruct((g, n), F32)] * 2 + [jax.ShapeDtypeStruct((m, g, n), F32)] * 2,
        name="s5_params",
    )(p["s5_log_dt"][e].reshape(g, 1), p["s5_lambda_re"][e], p["s5_lambda_im"][e],
      p["s5_b_re"][e].transpose(2, 0, 1), p["s5_b_im"][e].transpose(2, 0, 1))
    return outs


def _block_diag(t, gs):
    g, a, b = t.shape
    t = t.reshape(g // gs, gs, a, b)
    eye = jnp.eye(gs, dtype=t.dtype)
    return jnp.einsum("sgab,gh->sgahb", t, eye).reshape(g // gs, gs * a, gs * b)


def _s5_kernel(u_ref, bre_ref, bim_ref, cre_ref, cim_ref, d_ref, gw_ref, gb_ref, are_ref, aim_ref,
               h0r_ref, h0i_ref, o_ref, hr_ref, hi_ref, u_scr, xr_scr, xi_scr, *, steps, pitch):
    @pl.when(pl.program_id(1) == 0)
    def _():
        hr_ref[...] = h0r_ref[...]
        hi_ref[...] = h0i_ref[...]

    nb = u_ref.shape[0]
    nk = xr_scr.shape[0]
    lanes = [slice(k * LANES, (k + 1) * LANES) for k in range(nk)]
    for b in range(nb):
        u_scr[pl.ds(b * pitch, steps), :] = u_ref[b]
        u_scr[pl.ds(b * pitch + steps, pitch - steps), :] = jnp.zeros((pitch - steps, LANES), F32)
    u2 = u_scr[...]
    bu_re = _mm(u2, bre_ref[0])
    bu_im = _mm(u2, bim_ref[0])
    for k in range(nk):
        xr_scr[k] = bu_re[:, lanes[k]]
        xi_scr[k] = bu_im[:, lanes[k]]
    a_re = [are_ref[0, :, lanes[k]] for k in range(nk)]
    a_im = [aim_ref[0, :, lanes[k]] for k in range(nk)]

    def step(t, carry):
        rows = pl.ds(t, nb, stride=pitch)
        new = []
        for k in range(nk):
            h_re, h_im = carry[2 * k], carry[2 * k + 1]
            n_re = a_re[k] * h_re - a_im[k] * h_im + xr_scr[k, rows, :]
            n_im = a_re[k] * h_im + a_im[k] * h_re + xi_scr[k, rows, :]
            xr_scr[k, rows, :] = n_re
            xi_scr[k, rows, :] = n_im
            new += [n_re, n_im]
        return tuple(new)

    init = tuple(r[:, lanes[k]] for k in range(nk) for r in (hr_ref, hi_ref))
    fin = lax.fori_loop(0, steps, step, init)
    for k in range(nk):
        hr_ref[:, lanes[k]] = fin[2 * k]
        hi_ref[:, lanes[k]] = fin[2 * k + 1]
    hs_re = jnp.concatenate([xr_scr[k] for k in range(nk)], axis=1) if nk > 1 else xr_scr[0]
    hs_im = jnp.concatenate([xi_scr[k] for k in range(nk)], axis=1) if nk > 1 else xi_scr[0]
    y = _mm(hs_re, cre_ref[0]) - _mm(hs_im, cim_ref[0]) + d_ref[0] * u2
    yg = 0.5 * y * (1.0 + jnp.tanh(0.7978845608028654 * (y + 0.044715 * (y * y * y))))
    out = yg * _sigmoid(_mm(yg, gw_ref[0]) + gb_ref[0])
    for b in range(nb):
        o_ref[b] = out[b * pitch:b * pitch + steps].astype(o_ref.dtype)


def _s5(zin, col0, h_re, h_im, p, e):
    b, l, _ = zin.shape
    g, n, m = p["s5_b_re"].shape[1:]
    width = g * m
    gs = LANES // m
    ns = g // gs
    sn = gs * n
    ab_re, ab_im, bb_re, bb_im = _s5_params(p, e)
    bd_bre = _block_diag(bb_re.transpose(1, 0, 2), gs)
    bd_bim = _block_diag(bb_im.transpose(1, 0, 2), gs)
    bd_cre = _block_diag(p["s5_c_re"][e].transpose(0, 2, 1), gs)
    bd_cim = _block_diag(p["s5_c_im"][e].transpose(0, 2, 1), gs)
    bd_gw = _block_diag(p["s5_glu_w"][e], gs)
    steps = _pick_tile(l, 256, 2 * SUBLANES)
    pitch = steps + SUBLANES
    cb0 = col0 // LANES
    slab = lambda r, c: pl.BlockSpec((1, r, c), lambda s, t: (s, 0, 0))
    st_spec = pl.BlockSpec((b, sn), lambda s, t: (0, s))
    out, hr, hi = pl.pallas_call(
        functools.partial(_s5_kernel, steps=steps, pitch=pitch),
        grid=(ns, l // steps),
        in_specs=[pl.BlockSpec((b, steps, LANES), lambda s, t: (0, t, cb0 + s)),
                  slab(LANES, sn), slab(LANES, sn), slab(sn, LANES), slab(sn, LANES),
                  slab(1, LANES), slab(LANES, LANES), slab(1, LANES), slab(1, sn), slab(1, sn),
                  st_spec, st_spec],
        out_specs=[pl.BlockSpec((b, steps, LANES), lambda s, t: (0, t, s)), st_spec, st_spec],
        out_shape=[jax.ShapeDtypeStruct((b, l, width), BF16),
                   jax.ShapeDtypeStruct((b, g * n), F32), jax.ShapeDtypeStruct((b, g * n), F32)],
        scratch_shapes=[pltpu.VMEM((b * pitch, LANES), F32),
                        pltpu.VMEM((sn // LANES, b * pitch, LANES), F32),
                        pltpu.VMEM((sn // LANES, b * pitch, LANES), F32)],
        compiler_params=_params(("arbitrary", "arbitrary"), big=True),
        name="s5",
    )(zin, bd_bre, bd_bim, bd_cre, bd_cim, p["s5_d"][e].reshape(ns, 1, LANES), bd_gw,
      p["s5_glu_b"][e].reshape(ns, 1, LANES), ab_re.reshape(ns, 1, sn), ab_im.reshape(ns, 1, sn),
      h_re.astype(F32).reshape(b, g * n), h_im.astype(F32).reshape(b, g * n))
    return (out, hr.reshape(b, g, n).astype(h_re.dtype), hi.reshape(b, g, n).astype(h_im.dtype))


def _gdn_kernel(q_ref, k_ref, v_ref, y_ref, wq_ref, wk_ref, wv_ref, bq_ref, bk_ref, bv_ref,
                zab_ref, alog_ref, dtb_ref, nw_ref, s0_ref, o_ref, s_ref, cq_scr, ck_scr, cv_scr,
                *, hb, dk, dv, kconv):
    t = pl.program_id(2)

    @pl.when(t == 0)
    def _():
        s_ref[0] = s0_ref[0]
        cq_scr[...] = bq_ref[0]
        ck_scr[...] = bk_ref[0]
        cv_scr[...] = bv_ref[0]

    c = q_ref.shape[1]

    def conv(x_ref, w_ref, carry_scr):
        x = x_ref[0]
        full = jnp.concatenate([carry_scr[...], x], axis=0)
        carry_scr[...] = x_ref[0, pl.ds(c - SUBLANES, SUBLANES), :]
        acc = None
        for j in range(kconv):
            sh = kconv - 1 - j
            src = full if sh == 0 else pltpu.roll(full, sh, 0)
            term = src[SUBLANES:SUBLANES + c] * w_ref[pl.ds(j, 1), :]
            acc = term if acc is None else acc + term
        return _silu(acc)

    qa = conv(q_ref, wq_ref, cq_scr)
    ka = conv(k_ref, wk_ref, ck_scr)
    va = conv(v_ref, wv_ref, cv_scr)
    incl = _tril(c)
    strict = _tril(c, strict=True)
    hs = range(hb)
    qs, ks, vs = [], [], []
    for hl in hs:
        q = qa[:, hl * dk:(hl + 1) * dk]
        k = ka[:, hl * dk:(hl + 1) * dk]
        qs.append(q * lax.rsqrt(jnp.sum(q * q, axis=-1, keepdims=True) + L2_EPS) * (dk ** -0.5))
        ks.append(k * lax.rsqrt(jnp.sum(k * k, axis=-1, keepdims=True) + L2_EPS))
        vs.append(va[:, hl * dv:(hl + 1) * dv])
    z = zab_ref[0]
    gcum = _mm01(incl.astype(F32), -jnp.exp(alog_ref[...]) * _softplus(z + dtb_ref[...]))
    beta = _sigmoid(z)
    gcum_t = gcum.T
    gc_c = [gcum[:, hl:hl + 1] for hl in hs]
    beta_c = [beta[:, hb + hl:hb + hl + 1] for hl in hs]
    gc_r = [gcum_t[hl:hl + 1, :] for hl in hs]
    g_last = [g[:, c - 1:c] for g in gc_r]
    kbs = [k * b for k, b in zip(ks, beta_c)]
    dmask = [jnp.where(incl, jnp.exp(jnp.where(incl, gc - gr, 0.0)), 0.0) for gc, gr in zip(gc_c, gc_r)]
    kk = [_mm_nt(kb, k) for kb, k in zip(kbs, ks)]
    qk = [_mm_nt(q, k) for q, k in zip(qs, ks)]
    ms = [jnp.where(strict, a * d, 0.0) for a, d in zip(kk, dmask)]
    egc = [jnp.exp(g) for g in gc_c]
    rhss = [jnp.concatenate([v * b, kb * e], axis=1) for v, b, kb, e in zip(vs, beta_c, kbs, egc)]
    sols = _unit_lower_solve(ms, rhss)
    attn = [a * d for a, d in zip(qk, dmask)]
    s_old = [s_ref[0, hl] for hl in hs]
    ws = [_mm(sol[:, dv:], s) for sol, s in zip(sols, s_old)]
    qss = [_mm(q * e, s) for q, e, s in zip(qs, egc, s_old)]
    v_new = [sol[:, :dv] - w for sol, w in zip(sols, ws)]
    av = [_mm(a, vn) for a, vn in zip(attn, v_new)]
    kdec = [k * jnp.exp(gl - gc) for k, gl, gc in zip(ks, g_last, gc_c)]
    kv = [_mm_tn(kd, vn) for kd, vn in zip(kdec, v_new)]
    states = [s * jnp.exp(gl) + x for s, gl, x in zip(s_old, g_last, kv)]
    outs = [_head_rmsnorm(a + b, nw_ref[...]) * _silu(y_ref[0, :, hl * dv:(hl + 1) * dv])
            for hl, a, b in zip(hs, qss, av)]
    o_ref[0] = (outs[0] if hb == 1 else jnp.concatenate(outs, axis=1)).astype(o_ref.dtype)
    s_ref[0] = jnp.stack(states, axis=0)


def _gdn(zmain, zab, conv_buf, s0, p, o, chunk):
    b, l, _ = zmain.shape
    heads = p["gdn_a_log"].shape[1]
    dv = p["gdn_norm_w"].shape[1]
    kconv, conv_ch = p["gdn_conv_w"].shape[1:]
    vw = heads * dv
    kw = (conv_ch - vw) // 2
    dk = kw // heads
    hb = heads
    nhb = heads // hb
    nc = l // chunk
    lane_row = lambda t: jnp.pad(t, (0, LANES - heads)).reshape(1, LANES)
    cbuf = jnp.pad(conv_buf.astype(F32), ((0, 0), (SUBLANES - (kconv - 1), 0), (0, 0)))
    wq, wk = hb * dk, hb * dk
    wv = hb * dv
    qoff, koff, voff, yoff = 0, kw // wk, 2 * kw // wv, (2 * kw + vw) // wv
    col = lambda w_, off: pl.BlockSpec((1, chunk, w_), lambda i, h, t: (i, t, off + h))
    cw = lambda w_, off: pl.BlockSpec((None, kconv, w_), lambda i, h, t: (o, 0, off + h))
    cb = lambda w_, off: pl.BlockSpec((1, SUBLANES, w_), lambda i, h, t: (i, 0, off + h))
    st = pl.BlockSpec((1, hb, dk, dv), lambda i, h, t: (i, h, 0, 0))
    out, s_fin = pl.pallas_call(
        functools.partial(_gdn_kernel, hb=hb, dk=dk, dv=dv, kconv=kconv),
        grid=(b, nhb, nc),
        in_specs=[col(wq, qoff), col(wk, koff), col(wv, voff), col(wv, yoff),
                  cw(wq, qoff), cw(wk, koff), cw(wv, voff),
                  cb(wq, qoff), cb(wk, koff), cb(wv, voff),
                  pl.BlockSpec((1, chunk, LANES), lambda i, h, t: (i, t, 0)),
                  pl.BlockSpec((1, LANES), lambda i, h, t: (0, 0)),
                  pl.BlockSpec((1, LANES), lambda i, h, t: (0, 0)),
                  pl.BlockSpec((1, dv), lambda i, h, t: (0, 0)),
                  st],
        out_specs=[pl.BlockSpec((1, chunk, wv), lambda i, h, t: (i, t, h)), st],
        out_shape=[jax.ShapeDtypeStruct((b, l, vw), BF16), jax.ShapeDtypeStruct((b, heads, dk, dv), F32)],
        scratch_shapes=[pltpu.VMEM((SUBLANES, wq), F32), pltpu.VMEM((SUBLANES, wk), F32),
                        pltpu.VMEM((SUBLANES, wv), F32)],
        compiler_params=_params(("arbitrary", "arbitrary", "arbitrary")),
        name="gdn",
    )(zmain, zmain, zmain, zmain, p["gdn_conv_w"], p["gdn_conv_w"], p["gdn_conv_w"],
      cbuf, cbuf, cbuf, zab, lane_row(p["gdn_a_log"][o]), lane_row(p["gdn_dt_bias"][o]),
      p["gdn_norm_w"][o:o + 1], s0.astype(F32))
    tail = jnp.concatenate([conv_buf.astype(F32), zmain[:, -(kconv - 1):, :conv_ch]], axis=1)[:, -(kconv - 1):]
    return out, tail.astype(conv_buf.dtype), s_fin.astype(s0.dtype)


def _hgrn_kernel(q_ref, f_ref, i_ref, og_ref, lbl_ref, nw_ref, s0_ref, o_ref, s_ref, *, hb, dk, dv, layer):
    @pl.when(pl.program_id(2) == 0)
    def _():
        s_ref[0] = s0_ref[0]

    c = q_ref.shape[1]
    logits = lbl_ref[...]
    mx = jnp.max(logits, axis=0, keepdims=True)
    ex = jnp.exp(logits - mx)
    den = jnp.sum(ex, axis=0, keepdims=True)
    lb_all = jnp.zeros_like(den)
    for r in range(1, layer + 1):
        lb_all = lb_all + ex[r:r + 1] / den
    tril_f = _tril(c).astype(F32)
    nb = max(c // SUB, 1)
    sb = min(SUB, c)
    rowi = _iota2((sb, 1), 0)
    eye = _iota2((dk, dk), 0) == _iota2((dk, dk), 1)
    hs = range(hb)
    ksl = [slice(hl * dk, (hl + 1) * dk) for hl in hs]
    vsl = [slice(hl * dv, (hl + 1) * dv) for hl in hs]
    q = q_ref[0]
    f = lb_all + (1.0 - lb_all) * _sigmoid(f_ref[0])
    k = 1.0 - f
    v = i_ref[0]
    bcum = _mm01(tril_f, jnp.log(f))
    qe = q * jnp.exp(bcum)
    s_old = [s_ref[0, hl] for hl in hs]
    inter = [_mm(qe[:, ksl[hl]], s_old[hl]) for hl in hs]
    blocks = [[] for _ in hs]
    for bi in range(nb):
        r0 = bi * sb
        qi, ki, vi, bb = q[r0:r0 + sb], k[r0:r0 + sb], v[r0:r0 + sb], bcum[r0:r0 + sb]
        pieces = [[] for _ in hs]
        for p0 in range(0, sb, SUBLANES):
            p1 = min(p0 + SUBLANES, sb)
            qp, bp_ = qi[p0:p1], bb[p0:p1]
            acc = [inter[hl][r0 + p0:r0 + p1] for hl in hs]
            for si in range(p1):
                diff = bp_ - bb[si:si + 1]
                if si > p0:
                    msk = rowi[p0:p1] >= si
                    dec = jnp.where(msk, jnp.exp(jnp.where(msk, diff, 0.0)), 0.0)
                else:
                    dec = jnp.exp(diff)
                prod = qp * ki[si:si + 1] * dec
                for hl in hs:
                    colv = jnp.sum(prod[:, ksl[hl]], axis=-1, keepdims=True)
                    acc[hl] = acc[hl] + colv * vi[si:si + 1, vsl[hl]]
            for hl in hs:
                pieces[hl].append(acc[hl])
        acc = [pc[0] if len(pc) == 1 else jnp.concatenate(pc, axis=0) for pc in pieces]
        if bi > 0:
            bref = bcum[r0 - 1:r0]
            qsc = qi * jnp.exp(bb - bref)
            ksc = k[:r0] * jnp.exp(bref - bcum[:r0])
            sc = [_mm3(qsc[:, ksl[hl]], ksc[:, ksl[hl]], ((1,), (1,))) for hl in hs]
            acc = [acc[hl] + _mm(sc[hl], v[:r0, vsl[hl]]) for hl in hs]
        for hl in hs:
            blocks[hl].append(acc[hl])
    b_last = bcum[c - 1:c]
    kd = k * jnp.exp(b_last - bcum)
    e_last = jnp.exp(b_last)
    kv = [_mm_tn(kd[:, ksl[hl]], v[:, vsl[hl]]) for hl in hs]
    states = []
    outs = []
    og = og_ref[0]
    for hl in hs:
        e_col = jnp.sum(jnp.where(eye, e_last[:, ksl[hl]], 0.0), axis=1, keepdims=True)
        states.append(e_col * s_old[hl] + kv[hl])
        o = blocks[hl][0] if nb == 1 else jnp.concatenate(blocks[hl], axis=0)
        outs.append(_head_rmsnorm(o, nw_ref[...]) * _sigmoid(og[:, vsl[hl]]))
    o_ref[0] = (outs[0] if hb == 1 else jnp.concatenate(outs, axis=1)).astype(o_ref.dtype)
    s_ref[0] = jnp.stack(states, axis=0)


def _hgrn2(z, s0, p, o, layer, chunk):
    b, l, zw = z.shape
    dv = p["hg_norm_w"].shape[1]
    depth, kw = p["hg_lb_logits"].shape
    vw = (zw - 2 * kw) // 2
    heads = vw // dv
    dk = kw // heads
    hb = _pick_tile(heads, HGRN_HEADS_PER_STEP, 1)
    nhb = heads // hb
    wk, wv = hb * dk, hb * dv
    col = lambda w_, off: pl.BlockSpec((1, chunk, w_), lambda i, h, t: (i, t, off + h))
    st = pl.BlockSpec((1, hb, dk, dv), lambda i, h, t: (i, h, 0, 0))
    out, s_fin = pl.pallas_call(
        functools.partial(_hgrn_kernel, hb=hb, dk=dk, dv=dv, layer=layer),
        grid=(b, nhb, l // chunk),
        in_specs=[col(wk, 0), col(wk, kw // wk), col(wv, 2 * kw // wv), col(wv, (2 * kw + vw) // wv),
                  pl.BlockSpec((depth, wk), lambda i, h, t: (0, h)),
                  pl.BlockSpec((1, dv), lambda i, h, t: (0, 0)),
                  st],
        out_specs=[pl.BlockSpec((1, chunk, wv), lambda i, h, t: (i, t, h)), st],
        out_shape=[jax.ShapeDtypeStruct((b, l, vw), BF16), jax.ShapeDtypeStruct((b, heads, dk, dv), F32)],
        compiler_params=_params(("arbitrary", "arbitrary", "arbitrary")),
        name="hgrn2",
    )(z, z, z, z, p["hg_lb_logits"].astype(F32), p["hg_norm_w"][o:o + 1], s0.astype(F32))
    return out, s_fin.astype(s0.dtype)


def _trunk(xs, mods, sts, p, prep):
    groups = (0, 1)
    dims = [x.shape[:2] for x in xs]
    d = xs[0].shape[2]
    rows = [b * l for b, l in dims]
    depth = p["w_mod"].shape[0]
    d_ff = p["w_ffn_out"].shape[2]
    chunks = [min(CHUNK, l) for _, l in dims]
    new_st = [[[] for _ in range(7)] for _ in groups]
    x2 = [x.reshape(r, d) for x, r in zip(xs, rows)]

    def mm(x_pair, w, prefix, cols, tn, *, epi="plain", res_pair=None, mod9s=None, idx=None, **kw):
        res2 = gate = gate2 = None
        if epi == "res":
            gate = mod9s[0][:, 3 * idx + 2].reshape(dims[0][0], 1, d)
            gate2 = jnp.repeat(mod9s[1][:, 3 * idx + 2], dims[1][1], axis=0).reshape(1, rows[1], d)
            res2 = res_pair[1]
        return _matmul(x_pair[0], w, prefix, cols, tn, epi=epi, rows_per_gate=dims[0][1],
                       res=None if res_pair is None else res_pair[0], gate=gate,
                       rider=(x_pair[1], res2, gate2), **kw)

    def norm(x2, g, mod9s, idx):
        return [_norm(x2[i].reshape(*dims[i], d), g, mod9s[i], idx).reshape(rows[i], d) for i in groups]

    def ffn(x2, mod9s, lyr, idx, slot):
        h = norm(x2, p["norm_g"][lyr, idx], mod9s, idx)
        tn = _pick_tile(d_ff, 256)
        act = mm(h, p["w_ffn_in"], (lyr, slot), ((0, d_ff // tn), d_ff // tn), tn,
                 epi="swiglu", out_dtype=BF16, tm_pref=1024)
        tn2 = _pick_tile(d, 512)
        return mm(act, p["w_ffn_out"], (lyr, slot), ((0,), d // tn2), tn2, epi="res",
                  res_pair=x2, mod9s=mod9s, idx=idx, coef=0.5, w_buffers=1)

    for lyr in range(depth):
        mod9s = [m[lyr] for m in mods]
        x2 = ffn(x2, mod9s, lyr, 0, 0)
        h = norm(x2, p["norm_g"][lyr, 1], mod9s, 1)
        mix = [None, None]
        if lyr % 2 == 0:
            e = lyr // 2
            even_in = p["w_in_even"].shape[2]
            rw_in = p["rw_mu"].shape[1]
            tn = _pick_tile(even_in, 1280, 256)
            q = prep["even"][e]
            zin = mm(h, q["w_in"], (), ((0,), even_in // tn), tn, w_buffers=1)
            for i in groups:
                rw_s, rw_sh, s5_re, s5_im = sts[i][:4]
                z = zin[i].reshape(*dims[i], even_in)
                oa, sh, s = _rwkv7(z, rw_in, rw_sh[e], rw_s[e], q)
                ob, hr, hi = _s5(z, rw_in, s5_re[e], s5_im[e], p, e)
                for slot, val in zip((0, 1, 2, 3), (s, sh, hr, hi)):
                    new_st[i][slot].append(val)
                mix[i] = (oa.reshape(rows[i], -1), ob.reshape(rows[i], -1))
            w_out, widx = q["w_out"], ()
        else:
            o = lyr // 2
            w_gdn, w_hg, w_ab = prep["odd"][o]
            gdn_main = w_gdn.shape[1]
            tn = _pick_tile(gdn_main, 512)
            zmain = mm(h, w_gdn, (), ((0,), gdn_main // tn), tn, tm_pref=1024)
            zab = mm(h, w_ab, (), ((0,), 1), LANES)
            hg_in = w_hg.shape[1]
            tn = _pick_tile(hg_in, 512)
            zhg = mm(h, w_hg, (), ((0,), hg_in // tn), tn, tm_pref=1024)
            for i in groups:
                gdn_s, gdn_cv, hg_s = sts[i][4:]
                oc, cv, s = _gdn(zmain[i].reshape(*dims[i], gdn_main), zab[i].reshape(*dims[i], LANES),
                                 gdn_cv[o], gdn_s[o], p, o, chunks[i])
                od, sh_ = _hgrn2(zhg[i].reshape(*dims[i], hg_in), hg_s[o], p, o, lyr, chunks[i])
                for slot, val in zip((4, 5, 6), (s, cv, sh_)):
                    new_st[i][slot].append(val)
                mix[i] = (oc.reshape(rows[i], -1), od.reshape(rows[i], -1))
            w_out, widx = p["w_out_odd"], (o,)
        tn = _pick_tile(d, 512)
        x2 = mm(mix, w_out, widx, ((0,), d // tn), tn, epi="res", res_pair=x2, mod9s=mod9s, idx=1,
                coef=1.0, tm_pref=1024)
        x2 = ffn(x2, mod9s, lyr, 2, 1)
    ys = [_norm(x2[i].reshape(*dims[i], d), p["norm_final"], out_dtype=xs[i].dtype) for i in groups]
    return ys, [tuple(jnp.stack(v) for v in new_st[i]) for i in groups]


def kernel(x_prompt, x_sample, c_prompt, c_sample, state_rwkv, state_rwkv_shift, state_s5_re, state_s5_im, state_gdn, cache_gdn_conv, state_hgrn, w_mod, b_mod, norm_g, norm_final, w_ffn_in, w_ffn_out, w_in_even, w_out_even, rw_mu, rw_w0, rw_w_up, rw_a0, rw_a_up, rw_g_up, rw_k_k, rw_k_a, rw_r_k, rw_ln_w, rw_ln_b, s5_lambda_re, s5_lambda_im, s5_log_dt, s5_b_re, s5_b_im, s5_c_re, s5_c_im, s5_d, s5_glu_w, s5_glu_b, w_in_odd, w_out_odd, gdn_conv_w, gdn_a_log, gdn_dt_bias, gdn_norm_w, hg_lb_logits, hg_norm_w):
    p = dict(w_mod=w_mod, b_mod=b_mod, norm_g=norm_g, norm_final=norm_final,
             w_ffn_in=w_ffn_in, w_ffn_out=w_ffn_out, w_in_even=w_in_even, w_out_even=w_out_even,
             rw_mu=rw_mu, rw_w0=rw_w0, rw_w_up=rw_w_up, rw_a0=rw_a0, rw_a_up=rw_a_up, rw_g_up=rw_g_up,
             rw_k_k=rw_k_k, rw_k_a=rw_k_a, rw_r_k=rw_r_k, rw_ln_w=rw_ln_w, rw_ln_b=rw_ln_b,
             s5_lambda_re=s5_lambda_re, s5_lambda_im=s5_lambda_im, s5_log_dt=s5_log_dt,
             s5_b_re=s5_b_re, s5_b_im=s5_b_im, s5_c_re=s5_c_re, s5_c_im=s5_c_im, s5_d=s5_d,
             s5_glu_w=s5_glu_w, s5_glu_b=s5_glu_b, w_in_odd=w_in_odd, w_out_odd=w_out_odd,
             gdn_conv_w=gdn_conv_w, gdn_a_log=gdn_a_log, gdn_dt_bias=gdn_dt_bias, gdn_norm_w=gdn_norm_w,
             hg_lb_logits=hg_lb_logits, hg_norm_w=hg_norm_w)
    bp, bs = x_prompt.shape[0], x_sample.shape[0]
    d = x_prompt.shape[-1]
    depth = w_mod.shape[0]
    n_even, n_odd = (depth + 1) // 2, depth // 2
    dtp = x_prompt.dtype

    bc = bp + bs
    bc_pad = -(-bc // 16) * 16
    c_all = jnp.pad(jnp.concatenate([c_prompt, c_sample], axis=0), ((0, bc_pad - bc), (0, 0)))
    mod = _modulation(c_all, w_mod, b_mod).reshape(depth, bc_pad, 9, d)
    mod_p, mod_s = mod[:, :bp], mod[:, bp:bc]

    heads = gdn_a_log.shape[1]
    conv_ch = gdn_conv_w.shape[2]
    vw = heads * gdn_norm_w.shape[1]
    gdn_main = conv_ch + vw
    gdn_in = gdn_main + 2 * heads
    prep = {"odd": [], "even": [_rw_prep(p, e) for e in range(n_even)]}
    for o in range(n_odd):
        w_gdn = w_in_odd[o][:, :gdn_main]
        w_hg = w_in_odd[o][:, gdn_in:]
        w_ab = jnp.pad(w_in_odd[o][:, gdn_main:gdn_in], ((0, 0), (0, LANES - 2 * heads)))
        prep["odd"].append((w_gdn, w_hg, w_ab))

    rw_h, rw_n = rw_r_k.shape[1], rw_r_k.shape[2]
    rw_in = rw_mu.shape[1]
    s5_g, s5_n = s5_lambda_re.shape[1], s5_lambda_re.shape[2]
    gdn_dk = (conv_ch - vw) // 2 // heads
    gdn_dv = gdn_norm_w.shape[1]
    kconv = gdn_conv_w.shape[1]
    hg_dv = hg_norm_w.shape[1]
    hg_kw = hg_lb_logits.shape[1]
    hg_vw = (w_in_odd.shape[2] - gdn_in - 2 * hg_kw) // 2
    hg_h = hg_vw // hg_dv
    hg_dk = hg_kw // hg_h
    st_prompt = (jnp.zeros((n_even, bp, rw_h, rw_n, rw_n), dtp),
                 jnp.zeros((n_even, bp, rw_in), dtp),
                 jnp.zeros((n_even, bp, s5_g, s5_n), dtp),
                 jnp.zeros((n_even, bp, s5_g, s5_n), dtp),
                 jnp.zeros((n_odd, bp, heads, gdn_dk, gdn_dv), dtp),
                 jnp.zeros((n_odd, bp, kconv - 1, conv_ch), dtp),
                 jnp.zeros((n_odd, bp, hg_h, hg_dk, hg_dv), dtp))
    st_sample = (state_rwkv, state_rwkv_shift, state_s5_re, state_s5_im, state_gdn, cache_gdn_conv, state_hgrn)
    ys, ns = _trunk((x_prompt, x_sample), (mod_p, mod_s), (st_prompt, st_sample), p, prep)
    return tuple(ys) + tuple(ns[0]) + tuple(ns[1])
```

```python
import functools

import jax
import jax.numpy as jnp
from jax import lax
from jax.experimental import pallas as pl
from jax.experimental.pallas import tpu as pltpu

F32 = jnp.float32
BF16 = jnp.bfloat16
HI = lax.Precision.HIGHEST

EPS = 1e-6
RW_GN_EPS = 64e-5
L2_EPS = 1e-12
CHUNK = 64
SUB = 16
HGRN_HEADS_PER_STEP = 16
MM_ROW_CHUNK = 256
BF16_SUBLANES = 16
LANES = 128
SUBLANES = 8
VMEM_LIMIT = 56 * 1024 * 1024


def _params(sem, big=False):
    return pltpu.CompilerParams(dimension_semantics=sem,
                                vmem_limit_bytes=VMEM_LIMIT if big else None)


def _pick_tile(n, pref, mult=LANES):
    t = (min(pref, n) // mult) * mult
    while t >= mult:
        if n % t == 0:
            return t
        t -= mult
    return n


def _mm(a, b, hi=False):
    if not hi:
        a, b = a.astype(BF16), b.astype(BF16)
    return lax.dot_general(a, b, (((1,), (0,)), ((), ())), precision=HI if hi else None,
                           preferred_element_type=F32)


def _mm_nt(a, b, hi=False):
    if not hi:
        a, b = a.astype(BF16), b.astype(BF16)
    return lax.dot_general(a, b, (((1,), (1,)), ((), ())), precision=HI if hi else None,
                           preferred_element_type=F32)


def _mm_tn(a, b, hi=False):
    if not hi:
        a, b = a.astype(BF16), b.astype(BF16)
    return lax.dot_general(a, b, (((0,), (0,)), ((), ())), precision=HI if hi else None,
                           preferred_element_type=F32)


def _split2(a):
    hi = a.astype(BF16)
    return hi, (a - hi.astype(F32)).astype(BF16)


def _split3(a):
    hi = a.astype(BF16)
    r = a - hi.astype(F32)
    mid = r.astype(BF16)
    return hi, mid, (r - mid.astype(F32)).astype(BF16)


def _bdot(a, b, dims):
    return lax.dot_general(a, b, (dims, ((), ())), preferred_element_type=F32)


def _mm3(a, b, dims=((1,), (0,))):
    ah, al = _split2(a)
    bh, bl = _split2(b)
    return _bdot(ah, bh, dims) + (_bdot(ah, bl, dims) + _bdot(al, bh, dims))


def _mm01(mask, x):
    m = mask.astype(BF16)
    xh, xm, xl = _split3(x)
    dims = ((1,), (0,))
    return _bdot(m, xh, dims) + (_bdot(m, xm, dims) + _bdot(m, xl, dims))


def _x01(x, mask):
    m = mask.astype(BF16)
    xh, xm, xl = _split3(x)
    dims = ((1,), (0,))
    return _bdot(xh, m, dims) + (_bdot(xm, m, dims) + _bdot(xl, m, dims))


def _sigmoid(x):
    return 1.0 / (1.0 + jnp.exp(-x))


def _silu(x):
    return x * _sigmoid(x)


def _softplus(x):
    return jnp.maximum(x, 0.0) + jnp.log(1.0 + jnp.exp(-jnp.abs(x)))


def _iota2(shape, axis):
    return lax.broadcasted_iota(jnp.int32, shape, axis)


def _mod_kernel(c_ref, w_ref, b_ref, o_ref):
    s = _silu(c_ref[...]).astype(BF16)
    o_ref[0] = jnp.dot(s, w_ref[0].astype(BF16), preferred_element_type=F32) + b_ref[0]


def _modulation(c, w_mod, b_mod):
    depth, d, n = w_mod.shape
    bc = c.shape[0]
    tn = _pick_tile(n, 512)
    return pl.pallas_call(
        _mod_kernel,
        grid=(depth, n // tn),
        in_specs=[pl.BlockSpec((bc, d), lambda l, j: (0, 0)),
                  pl.BlockSpec((1, d, tn), lambda l, j: (l, 0, j)),
                  pl.BlockSpec((1, 1, tn), lambda l, j: (l, 0, j))],
        out_specs=pl.BlockSpec((1, bc, tn), lambda l, j: (l, 0, j)),
        out_shape=jax.ShapeDtypeStruct((depth, bc, n), F32),
        compiler_params=_params(("arbitrary", "arbitrary"), big=True),
        name="modulation",
    )(c, w_mod, b_mod.reshape(depth, 1, n))


def _norm_kernel(x_ref, g_ref, *rest, idx):
    if idx is None:
        (o_ref,) = rest
    else:
        m_ref, o_ref = rest
    x = x_ref[0]
    y = x * lax.rsqrt(jnp.mean(x * x, axis=-1, keepdims=True) + EPS) * g_ref[...]
    if idx is not None:
        shift = m_ref[0, pl.ds(3 * idx, 1), :]
        scale = m_ref[0, pl.ds(3 * idx + 1, 1), :]
        y = y * (1.0 + scale) + shift
    o_ref[0] = y.astype(o_ref.dtype)


def _norm(x, g, mod9=None, idx=None, out_dtype=BF16):
    b, l, d = x.shape
    tl = _pick_tile(l, 256, SUBLANES)
    in_specs = [pl.BlockSpec((1, tl, d), lambda i, t: (i, t, 0)),
                pl.BlockSpec((1, d), lambda i, t: (0, 0))]
    args = [x, g.reshape(1, d)]
    if idx is not None:
        in_specs.append(pl.BlockSpec((1, mod9.shape[1], d), lambda i, t: (i, 0, 0)))
        args.append(mod9)
    return pl.pallas_call(
        functools.partial(_norm_kernel, idx=idx),
        grid=(b, l // tl),
        in_specs=in_specs,
        out_specs=pl.BlockSpec((1, tl, d), lambda i, t: (i, t, 0)),
        out_shape=jax.ShapeDtypeStruct((b, l, d), out_dtype),
        compiler_params=_params(("arbitrary", "arbitrary")),
        name="norm",
    )(*args)


def _mm_kernel(*refs, n_x, n_w, epi, coef, rider):
    it = iter(refs)
    take = lambda n: [next(it) for _ in range(n)]
    x_refs, w_refs = take(n_x), take(n_w)
    res_ref, gate_ref = take(2) if epi == "res" else (None, None)
    if rider:
        x2_refs = take(n_x)
        res2_ref, gate2_ref = take(2) if epi == "res" else (None, None)
    (o_ref,) = take(1)
    if rider:
        (o2_ref,) = take(1)
    wb_refs = take(n_w)

    def rows(xr, res_r, gate_r, out_r):
        m = out_r.shape[0]
        step = MM_ROW_CHUNK if m % MM_ROW_CHUNK == 0 else m
        for r0 in range(0, m, step):
            rs = pl.ds(r0, step)
            acc = []
            for wb_ref in wb_refs:
                k0, a = 0, None
                for x_ref in xr:
                    kx = x_ref.shape[1]
                    part = jnp.dot(x_ref[rs, :], wb_ref[pl.ds(k0, kx), :], preferred_element_type=F32)
                    a = part if a is None else a + part
                    k0 += kx
                acc.append(a)
            if epi == "swiglu":
                out = _silu(acc[0]) * acc[1]
            elif epi == "res":
                gate = gate_r[0] if gate_r.shape[1] == 1 else gate_r[0, rs, :]
                out = res_r[rs, :] + coef * gate * acc[0]
            else:
                out = acc[0]
            out_r[rs, :] = out.astype(out_r.dtype)

    @pl.when(pl.program_id(1) == 0)
    def _():
        for w_ref, wb_ref in zip(w_refs, wb_refs):
            wb_ref[...] = w_ref[...].astype(BF16)
        if rider:
            rows(x2_refs, res2_ref, gate2_ref, o2_ref)

    rows(x_refs, res_ref, gate_ref, o_ref)


def _matmul(x, w, prefix, col_blocks, tn, *, rows_per_gate, res=None, gate=None, rider=None,
            coef=1.0, epi="plain", out_dtype=F32, tm_pref=512, w_buffers=2):
    xs = x if isinstance(x, (tuple, list)) else (x,)
    m = xs[0].shape[0]
    k = sum(t.shape[1] for t in xs)
    offs, n_cols = col_blocks
    n_w = len(offs)
    n_out = n_cols * tn
    tm = _pick_tile(rows_per_gate, tm_pref, SUBLANES)
    rpt = rows_per_gate // tm
    npre = len(prefix)
    wmode = {} if w_buffers == 2 else {"pipeline_mode": pl.Buffered(w_buffers)}
    in_specs = [pl.BlockSpec((tm, t.shape[1]), lambda j, i: (i, 0)) for t in xs]
    args = list(xs)
    for off in offs:
        in_specs.append(pl.BlockSpec((None,) * npre + (k, tn),
                                     lambda j, i, off=off: tuple(prefix) + (0, j + off), **wmode))
        args.append(w)
    if epi == "res":
        in_specs += [pl.BlockSpec((tm, tn), lambda j, i: (i, j)),
                     pl.BlockSpec((1, 1, tn), lambda j, i: (i // rpt, 0, j))]
        args += [res, gate]
    out_specs = [pl.BlockSpec((tm, tn), lambda j, i: (i, j))]
    out_shape = [jax.ShapeDtypeStruct((m, n_out), out_dtype)]
    if rider is not None:
        x2, res2, gate2 = rider
        x2s = x2 if isinstance(x2, (tuple, list)) else (x2,)
        m2 = x2s[0].shape[0]
        in_specs += [pl.BlockSpec((m2, t.shape[1]), lambda j, i: (0, 0)) for t in x2s]
        args += list(x2s)
        if epi == "res":
            in_specs += [pl.BlockSpec((m2, tn), lambda j, i: (0, j)),
                         pl.BlockSpec((1, m2, tn), lambda j, i: (0, 0, j))]
            args += [res2, gate2]
        out_specs.append(pl.BlockSpec((m2, tn), lambda j, i: (0, j)))
        out_shape.append(jax.ShapeDtypeStruct((m2, n_out), out_dtype))
    outs = pl.pallas_call(
        functools.partial(_mm_kernel, n_x=len(xs), n_w=n_w, epi=epi, coef=coef, rider=rider is not None),
        grid=(n_cols, m // tm),
        in_specs=in_specs,
        out_specs=out_specs,
        out_shape=out_shape,
        scratch_shapes=[pltpu.VMEM((k, tn), BF16) for _ in range(n_w)],
        compiler_params=_params(("arbitrary", "arbitrary"), big=True),
        name="matmul_" + epi,
    )(*args)
    return tuple(outs) if rider is not None else (outs[0], None)


def _group_ones(n):
    r = _iota2((LANES, LANES), 0)
    c = _iota2((LANES, LANES), 1)
    sh = n.bit_length() - 1
    return (jnp.right_shift(r, sh) == jnp.right_shift(c, sh)).astype(F32)


def _group_sum(x, n):
    if n % LANES == 0:
        parts = []
        for h in range(x.shape[1] // n):
            s = jnp.sum(x[:, h * n:(h + 1) * n], axis=-1, keepdims=True)
            parts.append(jnp.broadcast_to(s, (x.shape[0], n)))
        return parts[0] if len(parts) == 1 else jnp.concatenate(parts, axis=1)
    ones = _group_ones(n)
    parts = [_x01(x[:, s * LANES:(s + 1) * LANES], ones) for s in range(x.shape[1] // LANES)]
    return parts[0] if len(parts) == 1 else jnp.concatenate(parts, axis=1)


def _tril(c, strict=False):
    r = _iota2((c, c), 0)
    s = _iota2((c, c), 1)
    return (r > s) if strict else (r >= s)


def _unit_lower_solve(lms, rhss):
    c = lms[0].shape[0]
    r = _iota2((c, c), 0)
    s = _iota2((c, c), 1)
    eye = (r == s).astype(F32)
    if c <= SUB:
        lds, los = lms, None
    else:
        sh = SUB.bit_length() - 1
        same = jnp.right_shift(r, sh) == jnp.right_shift(s, sh)
        lds = [jnp.where(same, lm, 0.0) for lm in lms]
        los = [lm - ld for lm, ld in zip(lms, lds)]
    order = min(SUB, c)
    xs = [-ld for ld in lds]
    ps = [eye + x for x in xs]
    pws = [_mm(x, x) for x in xs]
    k = 2
    while k < order:
        new_ps = [p + _mm(pw, p) for p, pw in zip(ps, pws)]
        if 2 * k < order:
            pws = [_mm(pw, pw) for pw in pws]
        ps = new_ps
        k *= 2
    levels = []
    if los is not None:
        pws = [-_mm(p, lo) for p, lo in zip(ps, los)]
        k = 1
        while True:
            levels.append(pws)
            if 2 * k >= c // SUB:
                break
            pws = [_mm(pw, pw) for pw in pws]
            k *= 2

    def apply(vs):
        sols = [_mm(p, v) for p, v in zip(ps, vs)]
        for lvl in levels:
            sols = [sol + _mm(pw, sol) for sol, pw in zip(sols, lvl)]
        return sols

    sols = apply(rhss)
    resid = [rhs - (sol + _mm3(lm, sol)) for rhs, sol, lm in zip(rhss, sols, lms)]
    return [sol + cor for sol, cor in zip(sols, apply(resid))]


def _head_rmsnorm(o, w):
    return o * lax.rsqrt(jnp.mean(o * o, axis=-1, keepdims=True) + EPS) * w


def _token_shift(z, prev_row, mu):
    prev = jnp.where(_iota2(z.shape, 0) == 0, prev_row, pltpu.roll(z, 1, 0))
    return z + mu * (prev - z)


_RW_R, _RW_D, _RW_K, _RW_V, _RW_KK, _RW_NKKA = range(6)


def _window_transpose4_a(s, roll=None):
    roll = roll or (lambda x, k: pltpu.roll(x, k, 1))
    half = s[0].shape[1] // 2
    lo_half = _iota2(s[0].shape, 1) < half
    return (jnp.where(lo_half, s[0], roll(s[2], half)), jnp.where(lo_half, s[1], roll(s[3], half)),
            jnp.where(lo_half, roll(s[0], half), s[2]), jnp.where(lo_half, roll(s[1], half), s[3]))


def _window_transpose4_b(a, roll=None):
    roll = roll or (lambda x, k: pltpu.roll(x, k, 1))
    w = a[0].shape[1]
    quarter = w // 4
    even_win = jnp.bitwise_and(_iota2(a[0].shape, 1), quarter) == 0
    return (jnp.where(even_win, a[0], roll(a[1], quarter)), jnp.where(even_win, roll(a[0], w - quarter), a[1]),
            jnp.where(even_win, a[2], roll(a[3], quarter)), jnp.where(even_win, roll(a[2], w - quarter), a[3]))


def _window_transpose4(s, roll=None):
    return _window_transpose4_b(_window_transpose4_a(s, roll), roll)


def _rw_scan_kernel(zr_ref, zk_ref, zv_ref, zl_ref, mu_ref, sh_ref, w0_ref, a0_ref, wup_ref, aup_ref,
                    gup_ref, par_ref, s0_ref, out_ref, s_ref, zs_scr, prev_scr, seq_scr, o_scr, bs_scr,
                    *, steps, pitch, n, heads, bpl):
    @pl.when(pl.program_id(1) == 0)
    def _():
        s_ref[...] = s0_ref[...]
        prev_scr[...] = sh_ref[...]

    used = bpl * heads
    width = n * heads
    last = pl.ds(steps - 1, 1)
    for qi, z_ref in enumerate((zr_ref, zk_ref, zv_ref)):
        cols = slice(qi * width, (qi + 1) * width)
        for b in range(bpl):
            zs_scr[qi, b] = _token_shift(z_ref[b], prev_scr[b, :, cols], mu_ref[:, cols])
            prev_scr[b, :, cols] = z_ref[b, last, :]
    lcols = slice(3 * width, 3 * width + zl_ref.shape[2])
    shifted = []
    for b in range(bpl):
        shifted.append(_token_shift(zl_ref[b], prev_scr[b, :, lcols], mu_ref[:, lcols]))
        prev_scr[b, :, lcols] = zl_ref[b, last, :]
    zl = shifted[0] if bpl == 1 else jnp.concatenate(shifted, axis=0)
    w_lora, a_lora = wup_ref.shape[0], aup_ref.shape[0]
    xw, xa, xg = zl[:, :w_lora], zl[:, w_lora:w_lora + a_lora], zl[:, w_lora + a_lora:]
    w = -_softplus(-(w0_ref[...] + _mm(jnp.tanh(xw), wup_ref[...]))) - 0.5
    dag = (jnp.exp(-jnp.exp(w)), _sigmoid(a0_ref[...] + _mm(xa, aup_ref[...])), _mm(_sigmoid(xg), gup_ref[...]))
    for qi, val in enumerate(dag):
        for b in range(bpl):
            zs_scr[3 + qi, b] = val[b * steps:(b + 1) * steps]
    d_ref, a_ref, g_ref = zs_scr.at[3], zs_scr.at[4], zs_scr.at[5]

    def to_lanes(ref, j):
        pieces = [ref[b, :, j * heads:(j + 1) * heads] for b in range(bpl)]
        if used < LANES:
            pieces.append(jnp.zeros((steps, LANES - used), F32))
        return pieces[0] if len(pieces) == 1 else jnp.concatenate(pieces, axis=1)

    per = LANES // heads
    fast = bpl == 4 and per == 4

    def half_slabs(ref, j0):
        if fast:
            cols = slice((j0 // per) * LANES, (j0 // per + 1) * LANES)
            return _window_transpose4_a([ref[b, :, cols] for b in range(bpl)])
        return [to_lanes(ref, j0)]

    r_ref, k_ref, v_ref = zs_scr.at[0], zs_scr.at[1], zs_scr.at[2]
    sources = (r_ref, k_ref, a_ref, d_ref, v_ref)
    kkp, kap, rkp, lnw, lnb = (par_ref[i] for i in range(5))
    acc = [None, None]

    def finish(j0, halves):
        rs, ks, as_, ds_, vs_ = (_window_transpose4_b(h) if fast else h for h in halves)
        for w, (r, k, a) in enumerate(zip(rs, ks, as_)):
            j = j0 + w
            rows = pl.ds(j * pitch, steps)
            kraw = k * kkp[j:j + 1]
            k2 = k * (1.0 + (a - 1.0) * kap[j:j + 1])
            seq_scr[_RW_R, rows, :] = r
            seq_scr[_RW_D, rows, :] = ds_[w]
            seq_scr[_RW_V, rows, :] = vs_[w]
            seq_scr[_RW_K, rows, :] = k2
            seq_scr[_RW_KK, rows, :] = kraw
            seq_scr[_RW_NKKA, rows, :] = a
            for slot, term in enumerate((kraw * kraw, r * k2 * rkp[j:j + 1])):
                acc[slot] = term if acc[slot] is None else acc[slot] + term

    pending = None
    for j0 in range(0, n, per if fast else 1):
        halves = [half_slabs(ref, j0) for ref in sources]
        if pending is not None:
            finish(*pending)
        pending = (j0, halves)
    finish(*pending)
    sumsq, bsum = acc
    inv = lax.rsqrt(sumsq + L2_EPS)
    bs_scr[...] = bsum
    for j in range(n):
        rows = pl.ds(j * pitch, steps)
        kk = seq_scr[_RW_KK, rows, :] * inv
        seq_scr[_RW_KK, rows, :] = kk
        seq_scr[_RW_NKKA, rows, :] = -(kk * seq_scr[_RW_NKKA, rows, :])

    def row(q, j, t):
        return seq_scr[q, pl.ds(j * pitch + t, 1), :]

    sa0 = s_ref[0] * row(_RW_KK, 0, 0)
    for j in range(1, n):
        sa0 = sa0 + s_ref[j] * row(_RW_KK, j, 0)

    def step(t, sa):
        tn = jnp.minimum(t + 1, steps - 1)
        tile = pl.ds(t, n, stride=pitch)
        v_t = seq_scr[_RW_V, tile, :]
        o = None
        sa_next = None
        for j in range(n):
            sj = s_ref[j] * row(_RW_D, j, t) + sa * row(_RW_NKKA, j, t) + v_t * row(_RW_K, j, t)
            s_ref[j] = sj
            term = sj * row(_RW_R, j, t)
            o = term if o is None else o + term
            nxt = sj * row(_RW_KK, j, tn)
            sa_next = nxt if sa_next is None else sa_next + nxt
        mu = jnp.mean(o, axis=0, keepdims=True)
        c = o - mu
        var = jnp.mean(c * c, axis=0, keepdims=True)
        y = c * lax.rsqrt(var + RW_GN_EPS) * lnw + lnb
        o_scr[tile, :] = y + bs_scr[pl.ds(t, 1), :] * v_t
        return sa_next

    lax.fori_loop(0, steps, step, sa0)

    def emit(qd, tiles):
        cols = slice(qd * LANES, (qd + 1) * LANES)
        if fast:
            outs = _window_transpose4_b(tiles)
        else:
            outs = []
            for b in range(bpl):
                pieces = [t[:, b * heads:(b + 1) * heads] for t in tiles]
                outs.append(pieces[0] if per == 1 else jnp.concatenate(pieces, axis=1))
        for b in range(bpl):
            out_ref[b, :, cols] = (outs[b] * g_ref[b, :, cols]).astype(out_ref.dtype)

    pending = None
    for qd in range(n // per):
        tiles = [o_scr[pl.ds(i * pitch, steps), :] for i in range(qd * per, (qd + 1) * per)]
        if fast:
            tiles = _window_transpose4_a(tiles)
        if pending is not None:
            emit(*pending)
        pending = (qd, tiles)
    emit(*pending)


def _rw_scan(zin, rw_in, shift, q, par, s0, bpl):
    b, l, _ = zin.shape
    heads, n = q["heads"], q["n"]
    width = heads * n
    lw = rw_in - 3 * width
    assert (3 * width) % lw == 0 and lw % LANES == 0
    steps = _pick_tile(l, 32, 2 * SUBLANES)
    pitch = steps + SUBLANES
    z_spec = lambda qi: pl.BlockSpec((bpl, steps, width), lambda c, t: (c, t, qi))
    row = lambda w_: pl.BlockSpec((1, w_), lambda c, t: (0, 0))
    full = lambda t_: pl.BlockSpec(t_.shape, lambda c, t: (0, 0))
    st_spec = pl.BlockSpec((n, n, LANES), lambda c, t: (0, 0, c))
    return pl.pallas_call(
        functools.partial(_rw_scan_kernel, steps=steps, pitch=pitch, n=n, heads=heads, bpl=bpl),
        grid=(b // bpl, l // steps),
        in_specs=[z_spec(0), z_spec(1), z_spec(2),
                  pl.BlockSpec((bpl, steps, lw), lambda c, t: (c, t, 3 * width // lw)),
                  row(rw_in), pl.BlockSpec((bpl, 1, rw_in), lambda c, t: (c, 0, 0)),
                  row(width), row(width), full(q["w_up"]), full(q["a_up"]), full(q["g_up"]),
                  pl.BlockSpec((5, n, LANES), lambda c, t: (0, 0, 0)), st_spec],
        out_specs=[pl.BlockSpec((bpl, steps, width), lambda c, t: (c, t, 0)), st_spec],
        out_shape=[jax.ShapeDtypeStruct((b, l, width), BF16), jax.ShapeDtypeStruct(s0.shape, F32)],
        scratch_shapes=[pltpu.VMEM((6, bpl, steps, width), F32), pltpu.VMEM((bpl, 1, rw_in), F32),
                        pltpu.VMEM((6, n * pitch, LANES), F32), pltpu.VMEM((n * pitch, LANES), F32),
                        pltpu.VMEM((steps, LANES), F32)],
        compiler_params=_params(("arbitrary", "arbitrary"), big=True),
        name="rwkv_scan",
    )(zin, zin, zin, zin, q["mu"], shift.reshape(b, 1, rw_in), q["w0"], q["a0"],
      q["w_up"], q["a_up"], q["g_up"], par, s0)


def _index_major(t, heads, n, blocks=1, inverse=False):
    width = heads * n
    a, c = (n, heads) if inverse else (heads, n)
    parts = []
    for i in range(blocks):
        blk = t[..., i * width:(i + 1) * width]
        parts.append(blk.reshape(blk.shape[:-1] + (a, c)).swapaxes(-1, -2).reshape(blk.shape))
    if t.shape[-1] > blocks * width:
        parts.append(t[..., blocks * width:])
    return parts[0] if len(parts) == 1 else jnp.concatenate(parts, axis=-1)


def _rw_prep(p, e):
    heads, n = p["rw_r_k"].shape[1], p["rw_r_k"].shape[2]
    im = functools.partial(_index_major, heads=heads, n=n)
    w_out = p["w_out_even"][e]
    return dict(heads=heads, n=n,
                w_in=im(p["w_in_even"][e], blocks=3), w_out=jnp.swapaxes(im(jnp.swapaxes(w_out, 0, 1)), 0, 1),
                mu=im(p["rw_mu"][e:e + 1], blocks=3), w0=im(p["rw_w0"][e:e + 1]), a0=im(p["rw_a0"][e:e + 1]),
                w_up=im(p["rw_w_up"][e]), a_up=im(p["rw_a_up"][e]), g_up=im(p["rw_g_up"][e]),
                lane=[t.reshape(heads, n).T for t in (p["rw_k_k"][e], p["rw_k_a"][e], p["rw_r_k"][e],
                                                      p["rw_ln_w"][e], p["rw_ln_b"][e])])


def _rwkv7(zin, rw_in, shift, s0, q):
    b, l, _ = zin.shape
    heads, n = q["heads"], q["n"]
    bpl = min(b, LANES // heads)
    nlb, used = b // bpl, bpl * heads
    shift_im = _index_major(shift.astype(F32), heads, n, blocks=3)
    par = jnp.stack([jnp.pad(jnp.tile(t, (1, bpl)), ((0, 0), (0, LANES - used))) for t in q["lane"]])
    s0_l = s0.astype(F32).transpose(3, 2, 0, 1).reshape(n, n, nlb, used)
    s0_l = jnp.pad(s0_l, ((0, 0), (0, 0), (0, 0), (0, LANES - used))).reshape(n, n, nlb * LANES)
    out, s_l = _rw_scan(zin, rw_in, shift_im, q, par, s0_l, bpl)
    s_fin = s_l.reshape(n, n, nlb, LANES)[..., :used].reshape(n, n, b, heads).transpose(2, 3, 1, 0)
    new_shift = _index_major(zin[:, -1, :rw_in], heads, n, blocks=3, inverse=True)
    return out, new_shift.astype(shift.dtype), s_fin.astype(s0.dtype)


def _s5_param_kernel(ldt_ref, lr_ref, li_ref, bre_ref, bim_ref, are_o, aim_o, bbre_o, bbim_o):
    dt = jnp.exp(ldt_ref[...])
    lr, li = lr_ref[...], li_ref[...]
    mag = jnp.exp(lr * dt)
    ab_re, ab_im = mag * jnp.cos(li * dt), mag * jnp.sin(li * dt)
    den = lr * lr + li * li
    pr, pi_ = ab_re - 1.0, ab_im
    coef_re = (pr * lr + pi_ * li) / den
    coef_im = (pi_ * lr - pr * li) / den
    are_o[...] = ab_re
    aim_o[...] = ab_im
    for m in range(bre_ref.shape[0]):
        b_re, b_im = bre_ref[m], bim_ref[m]
        bbre_o[m] = coef_re * b_re - coef_im * b_im
        bbim_o[m] = coef_re * b_im + coef_im * b_re


def _s5_params(p, e):
    g, n, m = p["s5_b_re"].shape[1:]
    outs = pl.pallas_call(
        _s5_param_kernel,
        out_shape=[jax.ShapeDtypeStruct((g, n), F32)] * 2 + [jax.ShapeDtypeStruct((m, g, n), F32)] * 2,
        name="s5_params",
    )(p["s5_log_dt"][e].reshape(g, 1), p["s5_lambda_re"][e], p["s5_lambda_im"][e],
      p["s5_b_re"][e].transpose(2, 0, 1), p["s5_b_im"][e].transpose(2, 0, 1))
    return outs


def _block_diag(t, gs):
    g, a, b = t.shape
    t = t.reshape(g // gs, gs, a, b)
    eye = jnp.eye(gs, dtype=t.dtype)
    return jnp.einsum("sgab,gh->sgahb", t, eye).reshape(g // gs, gs * a, gs * b)


def _s5_kernel(u_ref, bre_ref, bim_ref, cre_ref, cim_ref, d_ref, gw_ref, gb_ref, are_ref, aim_ref,
               h0r_ref, h0i_ref, o_ref, hr_ref, hi_ref, u_scr, xr_scr, xi_scr, *, steps, pitch):
    @pl.when(pl.program_id(1) == 0)
    def _():
        hr_ref[...] = h0r_ref[...]
        hi_ref[...] = h0i_ref[...]

    nb = u_ref.shape[0]
    nk = xr_scr.shape[0]
    lanes = [slice(k * LANES, (k + 1) * LANES) for k in range(nk)]
    for b in range(nb):
        u_scr[pl.ds(b * pitch, steps), :] = u_ref[b]
        u_scr[pl.ds(b * pitch + steps, pitch - steps), :] = jnp.zeros((pitch - steps, LANES), F32)
    u2 = u_scr[...]
    bu_re = _mm(u2, bre_ref[0])
    bu_im = _mm(u2, bim_ref[0])
    for k in range(nk):
        xr_scr[k] = bu_re[:, lanes[k]]
        xi_scr[k] = bu_im[:, lanes[k]]
    a_re = [are_ref[0, :, lanes[k]] for k in range(nk)]
    a_im = [aim_ref[0, :, lanes[k]] for k in range(nk)]

    def step(t, carry):
        rows = pl.ds(t, nb, stride=pitch)
        new = []
        for k in range(nk):
            h_re, h_im = carry[2 * k], carry[2 * k + 1]
            n_re = a_re[k] * h_re - a_im[k] * h_im + xr_scr[k, rows, :]
            n_im = a_re[k] * h_im + a_im[k] * h_re + xi_scr[k, rows, :]
            xr_scr[k, rows, :] = n_re
            xi_scr[k, rows, :] = n_im
            new += [n_re, n_im]
        return tuple(new)

    init = tuple(r[:, lanes[k]] for k in range(nk) for r in (hr_ref, hi_ref))
    fin = lax.fori_loop(0, steps, step, init)
    for k in range(nk):
        hr_ref[:, lanes[k]] = fin[2 * k]
        hi_ref[:, lanes[k]] = fin[2 * k + 1]
    hs_re = jnp.concatenate([xr_scr[k] for k in range(nk)], axis=1) if nk > 1 else xr_scr[0]
    hs_im = jnp.concatenate([xi_scr[k] for k in range(nk)], axis=1) if nk > 1 else xi_scr[0]
    y = _mm(hs_re, cre_ref[0]) - _mm(hs_im, cim_ref[0]) + d_ref[0] * u2
    yg = 0.5 * y * (1.0 + jnp.tanh(0.7978845608028654 * (y + 0.044715 * (y * y * y))))
    out = yg * _sigmoid(_mm(yg, gw_ref[0]) + gb_ref[0])
    for b in range(nb):
        o_ref[b] = out[b * pitch:b * pitch + steps].astype(o_ref.dtype)


def _s5(zin, col0, h_re, h_im, p, e):
    b, l, _ = zin.shape
    g, n, m = p["s5_b_re"].shape[1:]
    width = g * m
    gs = LANES // m
    ns = g // gs
    sn = gs * n
    ab_re, ab_im, bb_re, bb_im = _s5_params(p, e)
    bd_bre = _block_diag(bb_re.transpose(1, 0, 2), gs)
    bd_bim = _block_diag(bb_im.transpose(1, 0, 2), gs)
    bd_cre = _block_diag(p["s5_c_re"][e].transpose(0, 2, 1), gs)
    bd_cim = _block_diag(p["s5_c_im"][e].transpose(0, 2, 1), gs)
    bd_gw = _block_diag(p["s5_glu_w"][e], gs)
    steps = _pick_tile(l, 256, 2 * SUBLANES)
    pitch = steps + SUBLANES
    cb0 = col0 // LANES
    slab = lambda r, c: pl.BlockSpec((1, r, c), lambda s, t: (s, 0, 0))
    st_spec = pl.BlockSpec((b, sn), lambda s, t: (0, s))
    out, hr, hi = pl.pallas_call(
        functools.partial(_s5_kernel, steps=steps, pitch=pitch),
        grid=(ns, l // steps),
        in_specs=[pl.BlockSpec((b, steps, LANES), lambda s, t: (0, t, cb0 + s)),
                  slab(LANES, sn), slab(LANES, sn), slab(sn, LANES), slab(sn, LANES),
                  slab(1, LANES), slab(LANES, LANES), slab(1, LANES), slab(1, sn), slab(1, sn),
                  st_spec, st_spec],
        out_specs=[pl.BlockSpec((b, steps, LANES), lambda s, t: (0, t, s)), st_spec, st_spec],
        out_shape=[jax.ShapeDtypeStruct((b, l, width), BF16),
                   jax.ShapeDtypeStruct((b, g * n), F32), jax.ShapeDtypeStruct((b, g * n), F32)],
        scratch_shapes=[pltpu.VMEM((b * pitch, LANES), F32),
                        pltpu.VMEM((sn // LANES, b * pitch, LANES), F32),
                        pltpu.VMEM((sn // LANES, b * pitch, LANES), F32)],
        compiler_params=_params(("arbitrary", "arbitrary"), big=True),
        name="s5",
    )(zin, bd_bre, bd_bim, bd_cre, bd_cim, p["s5_d"][e].reshape(ns, 1, LANES), bd_gw,
      p["s5_glu_b"][e].reshape(ns, 1, LANES), ab_re.reshape(ns, 1, sn), ab_im.reshape(ns, 1, sn),
      h_re.astype(F32).reshape(b, g * n), h_im.astype(F32).reshape(b, g * n))
    return (out, hr.reshape(b, g, n).astype(h_re.dtype), hi.reshape(b, g, n).astype(h_im.dtype))


def _gdn_kernel(q_ref, k_ref, v_ref, y_ref, wq_ref, wk_ref, wv_ref, bq_ref, bk_ref, bv_ref,
                zab_ref, alog_ref, dtb_ref, nw_ref, s0_ref, o_ref, s_ref, cq_scr, ck_scr, cv_scr,
                *, hb, dk, dv, kconv):
    t = pl.program_id(2)

    @pl.when(t == 0)
    def _():
        s_ref[0] = s0_ref[0]
        cq_scr[...] = bq_ref[0]
        ck_scr[...] = bk_ref[0]
        cv_scr[...] = bv_ref[0]

    c = q_ref.shape[1]

    def conv(x_ref, w_ref, carry_scr):
        x = x_ref[0]
        full = jnp.concatenate([carry_scr[...], x], axis=0)
        carry_scr[...] = x_ref[0, pl.ds(c - SUBLANES, SUBLANES), :]
        acc = None
        for j in range(kconv):
            sh = kconv - 1 - j
            src = full if sh == 0 else pltpu.roll(full, sh, 0)
            term = src[SUBLANES:SUBLANES + c] * w_ref[pl.ds(j, 1), :]
            acc = term if acc is None else acc + term
        return _silu(acc)

    qa = conv(q_ref, wq_ref, cq_scr)
    ka = conv(k_ref, wk_ref, ck_scr)
    va = conv(v_ref, wv_ref, cv_scr)
    incl = _tril(c)
    strict = _tril(c, strict=True)
    hs = range(hb)
    qs, ks, vs = [], [], []
    for hl in hs:
        q = qa[:, hl * dk:(hl + 1) * dk]
        k = ka[:, hl * dk:(hl + 1) * dk]
        qs.append(q * lax.rsqrt(jnp.sum(q * q, axis=-1, keepdims=True) + L2_EPS) * (dk ** -0.5))
        ks.append(k * lax.rsqrt(jnp.sum(k * k, axis=-1, keepdims=True) + L2_EPS))
        vs.append(va[:, hl * dv:(hl + 1) * dv])
    z = zab_ref[0]
    gcum = _mm01(incl.astype(F32), -jnp.exp(alog_ref[...]) * _softplus(z + dtb_ref[...]))
    beta = _sigmoid(z)
    gcum_t = gcum.T
    gc_c = [gcum[:, hl:hl + 1] for hl in hs]
    beta_c = [beta[:, hb + hl:hb + hl + 1] for hl in hs]
    gc_r = [gcum_t[hl:hl + 1, :] for hl in hs]
    g_last = [g[:, c - 1:c] for g in gc_r]
    kbs = [k * b for k, b in zip(ks, beta_c)]
    dmask = [jnp.where(incl, jnp.exp(jnp.where(incl, gc - gr, 0.0)), 0.0) for gc, gr in zip(gc_c, gc_r)]
    kk = [_mm_nt(kb, k) for kb, k in zip(kbs, ks)]
    qk = [_mm_nt(q, k) for q, k in zip(qs, ks)]
    ms = [jnp.where(strict, a * d, 0.0) for a, d in zip(kk, dmask)]
    egc = [jnp.exp(g) for g in gc_c]
    rhss = [jnp.concatenate([v * b, kb * e], axis=1) for v, b, kb, e in zip(vs, beta_c, kbs, egc)]
    sols = _unit_lower_solve(ms, rhss)
    attn = [a * d for a, d in zip(qk, dmask)]
    s_old = [s_ref[0, hl] for hl in hs]
    ws = [_mm(sol[:, dv:], s) for sol, s in zip(sols, s_old)]
    qss = [_mm(q * e, s) for q, e, s in zip(qs, egc, s_old)]
    v_new = [sol[:, :dv] - w for sol, w in zip(sols, ws)]
    av = [_mm(a, vn) for a, vn in zip(attn, v_new)]
    kdec = [k * jnp.exp(gl - gc) for k, gl, gc in zip(ks, g_last, gc_c)]
    kv = [_mm_tn(kd, vn) for kd, vn in zip(kdec, v_new)]
    states = [s * jnp.exp(gl) + x for s, gl, x in zip(s_old, g_last, kv)]
    outs = [_head_rmsnorm(a + b, nw_ref[...]) * _silu(y_ref[0, :, hl * dv:(hl + 1) * dv])
            for hl, a, b in zip(hs, qss, av)]
    o_ref[0] = (outs[0] if hb == 1 else jnp.concatenate(outs, axis=1)).astype(o_ref.dtype)
    s_ref[0] = jnp.stack(states, axis=0)


def _gdn(zmain, zab, conv_buf, s0, p, o, chunk):
    b, l, _ = zmain.shape
    heads = p["gdn_a_log"].shape[1]
    dv = p["gdn_norm_w"].shape[1]
    kconv, conv_ch = p["gdn_conv_w"].shape[1:]
    vw = heads * dv
    kw = (conv_ch - vw) // 2
    dk = kw // heads
    hb = heads
    nhb = heads // hb
    nc = l // chunk
    lane_row = lambda t: jnp.pad(t, (0, LANES - heads)).reshape(1, LANES)
    cbuf = jnp.pad(conv_buf.astype(F32), ((0, 0), (SUBLANES - (kconv - 1), 0), (0, 0)))
    wq, wk = hb * dk, hb * dk
    wv = hb * dv
    qoff, koff, voff, yoff = 0, kw // wk, 2 * kw // wv, (2 * kw + vw) // wv
    col = lambda w_, off: pl.BlockSpec((1, chunk, w_), lambda i, h, t: (i, t, off + h))
    cw = lambda w_, off: pl.BlockSpec((None, kconv, w_), lambda i, h, t: (o, 0, off + h))
    cb = lambda w_, off: pl.BlockSpec((1, SUBLANES, w_), lambda i, h, t: (i, 0, off + h))
    st = pl.BlockSpec((1, hb, dk, dv), lambda i, h, t: (i, h, 0, 0))
    out, s_fin = pl.pallas_call(
        functools.partial(_gdn_kernel, hb=hb, dk=dk, dv=dv, kconv=kconv),
        grid=(b, nhb, nc),
        in_specs=[col(wq, qoff), col(wk, koff), col(wv, voff), col(wv, yoff),
                  cw(wq, qoff), cw(wk, koff), cw(wv, voff),
                  cb(wq, qoff), cb(wk, koff), cb(wv, voff),
                  pl.BlockSpec((1, chunk, LANES), lambda i, h, t: (i, t, 0)),
                  pl.BlockSpec((1, LANES), lambda i, h, t: (0, 0)),
                  pl.BlockSpec((1, LANES), lambda i, h, t: (0, 0)),
                  pl.BlockSpec((1, dv), lambda i, h, t: (0, 0)),
                  st],
        out_specs=[pl.BlockSpec((1, chunk, wv), lambda i, h, t: (i, t, h)), st],
        out_shape=[jax.ShapeDtypeStruct((b, l, vw), BF16), jax.ShapeDtypeStruct((b, heads, dk, dv), F32)],
        scratch_shapes=[pltpu.VMEM((SUBLANES, wq), F32), pltpu.VMEM((SUBLANES, wk), F32),
                        pltpu.VMEM((SUBLANES, wv), F32)],
        compiler_params=_params(("arbitrary", "arbitrary", "arbitrary")),
        name="gdn",
    )(zmain, zmain, zmain, zmain, p["gdn_conv_w"], p["gdn_conv_w"], p["gdn_conv_w"],
      cbuf, cbuf, cbuf, zab, lane_row(p["gdn_a_log"][o]), lane_row(p["gdn_dt_bias"][o]),
      p["gdn_norm_w"][o:o + 1], s0.astype(F32))
    tail = jnp.concatenate([conv_buf.astype(F32), zmain[:, -(kconv - 1):, :conv_ch]], axis=1)[:, -(kconv - 1):]
    return out, tail.astype(conv_buf.dtype), s_fin.astype(s0.dtype)


def _hgrn_kernel(q_ref, f_ref, i_ref, og_ref, lbl_ref, nw_ref, s0_ref, o_ref, s_ref, *, hb, dk, dv, layer):
    @pl.when(pl.program_id(2) == 0)
    def _():
        s_ref[0] = s0_ref[0]

    c = q_ref.shape[1]
    logits = lbl_ref[...]
    mx = jnp.max(logits, axis=0, keepdims=True)
    ex = jnp.exp(logits - mx)
    den = jnp.sum(ex, axis=0, keepdims=True)
    lb_all = jnp.zeros_like(den)
    for r in range(1, layer + 1):
        lb_all = lb_all + ex[r:r + 1] / den
    tril_f = _tril(c).astype(F32)
    nb = max(c // SUB, 1)
    sb = min(SUB, c)
    rowi = _iota2((sb, 1), 0)
    eye = _iota2((dk, dk), 0) == _iota2((dk, dk), 1)
    hs = range(hb)
    ksl = [slice(hl * dk, (hl + 1) * dk) for hl in hs]
    vsl = [slice(hl * dv, (hl + 1) * dv) for hl in hs]
    q = q_ref[0]
    f = lb_all + (1.0 - lb_all) * _sigmoid(f_ref[0])
    k = 1.0 - f
    v = i_ref[0]
    bcum = _mm01(tril_f, jnp.log(f))
    qe = q * jnp.exp(bcum)
    s_old = [s_ref[0, hl] for hl in hs]
    inter = [_mm(qe[:, ksl[hl]], s_old[hl]) for hl in hs]
    blocks = [[] for _ in hs]
    for bi in range(nb):
        r0 = bi * sb
        qi, ki, vi, bb = q[r0:r0 + sb], k[r0:r0 + sb], v[r0:r0 + sb], bcum[r0:r0 + sb]
        pieces = [[] for _ in hs]
        for p0 in range(0, sb, SUBLANES):
            p1 = min(p0 + SUBLANES, sb)
            qp, bp_ = qi[p0:p1], bb[p0:p1]
            acc = [inter[hl][r0 + p0:r0 + p1] for hl in hs]
            for si in range(p1):
                diff = bp_ - bb[si:si + 1]
                if si > p0:
                    msk = rowi[p0:p1] >= si
                    dec = jnp.where(msk, jnp.exp(jnp.where(msk, diff, 0.0)), 0.0)
                else:
                    dec = jnp.exp(diff)
                prod = qp * ki[si:si + 1] * dec
                for hl in hs:
                    colv = jnp.sum(prod[:, ksl[hl]], axis=-1, keepdims=True)
                    acc[hl] = acc[hl] + colv * vi[si:si + 1, vsl[hl]]
            for hl in hs:
                pieces[hl].append(acc[hl])
        acc = [pc[0] if len(pc) == 1 else jnp.concatenate(pc, axis=0) for pc in pieces]
        if bi > 0:
            bref = bcum[r0 - 1:r0]
            qsc = qi * jnp.exp(bb - bref)
            ksc = k[:r0] * jnp.exp(bref - bcum[:r0])
            sc = [_mm3(qsc[:, ksl[hl]], ksc[:, ksl[hl]], ((1,), (1,))) for hl in hs]
            acc = [acc[hl] + _mm(sc[hl], v[:r0, vsl[hl]]) for hl in hs]
        for hl in hs:
            blocks[hl].append(acc[hl])
    b_last = bcum[c - 1:c]
    kd = k * jnp.exp(b_last - bcum)
    e_last = jnp.exp(b_last)
    kv = [_mm_tn(kd[:, ksl[hl]], v[:, vsl[hl]]) for hl in hs]
    states = []
    outs = []
    og = og_ref[0]
    for hl in hs:
        e_col = jnp.sum(jnp.where(eye, e_last[:, ksl[hl]], 0.0), axis=1, keepdims=True)
        states.append(e_col * s_old[hl] + kv[hl])
        o = blocks[hl][0] if nb == 1 else jnp.concatenate(blocks[hl], axis=0)
        outs.append(_head_rmsnorm(o, nw_ref[...]) * _sigmoid(og[:, vsl[hl]]))
    o_ref[0] = (outs[0] if hb == 1 else jnp.concatenate(outs, axis=1)).astype(o_ref.dtype)
    s_ref[0] = jnp.stack(states, axis=0)


def _hgrn2(z, s0, p, o, layer, chunk):
    b, l, zw = z.shape
    dv = p["hg_norm_w"].shape[1]
    depth, kw = p["hg_lb_logits"].shape
    vw = (zw - 2 * kw) // 2
    heads = vw // dv
    dk = kw // heads
    hb = _pick_tile(heads, HGRN_HEADS_PER_STEP, 1)
    nhb = heads // hb
    wk, wv = hb * dk, hb * dv
    col = lambda w_, off: pl.BlockSpec((1, chunk, w_), lambda i, h, t: (i, t, off + h))
    st = pl.BlockSpec((1, hb, dk, dv), lambda i, h, t: (i, h, 0, 0))
    out, s_fin = pl.pallas_call(
        functools.partial(_hgrn_kernel, hb=hb, dk=dk, dv=dv, layer=layer),
        grid=(b, nhb, l // chunk),
        in_specs=[col(wk, 0), col(wk, kw // wk), col(wv, 2 * kw // wv), col(wv, (2 * kw + vw) // wv),
                  pl.BlockSpec((depth, wk), lambda i, h, t: (0, h)),
                  pl.BlockSpec((1, dv), lambda i, h, t: (0, 0)),
                  st],
        out_specs=[pl.BlockSpec((1, chunk, wv), lambda i, h, t: (i, t, h)), st],
        out_shape=[jax.ShapeDtypeStruct((b, l, vw), BF16), jax.ShapeDtypeStruct((b, heads, dk, dv), F32)],
        compiler_params=_params(("arbitrary", "arbitrary", "arbitrary")),
        name="hgrn2",
    )(z, z, z, z, p["hg_lb_logits"].astype(F32), p["hg_norm_w"][o:o + 1], s0.astype(F32))
    return out, s_fin.astype(s0.dtype)


def _trunk(xs, mods, sts, p, prep):
    groups = (0, 1)
    dims = [x.shape[:2] for x in xs]
    d = xs[0].shape[2]
    rows = [b * l for b, l in dims]
    depth = p["w_mod"].shape[0]
    d_ff = p["w_ffn_out"].shape[2]
    chunks = [min(CHUNK, l) for _, l in dims]
    new_st = [[[] for _ in range(7)] for _ in groups]
    x2 = [x.reshape(r, d) for x, r in zip(xs, rows)]

    def mm(x_pair, w, prefix, cols, tn, *, epi="plain", res_pair=None, mod9s=None, idx=None, **kw):
        res2 = gate = gate2 = None
        if epi == "res":
            gate = mod9s[0][:, 3 * idx + 2].reshape(dims[0][0], 1, d)
            gate2 = jnp.repeat(mod9s[1][:, 3 * idx + 2], dims[1][1], axis=0).reshape(1, rows[1], d)
            res2 = res_pair[1]
        return _matmul(x_pair[0], w, prefix, cols, tn, epi=epi, rows_per_gate=dims[0][1],
                       res=None if res_pair is None else res_pair[0], gate=gate,
                       rider=(x_pair[1], res2, gate2), **kw)

    def norm(x2, g, mod9s, idx):
        return [_norm(x2[i].reshape(*dims[i], d), g, mod9s[i], idx).reshape(rows[i], d) for i in groups]

    def ffn(x2, mod9s, lyr, idx, slot):
        h = norm(x2, p["norm_g"][lyr, idx], mod9s, idx)
        tn = _pick_tile(d_ff, 256)
        act = mm(h, p["w_ffn_in"], (lyr, slot), ((0, d_ff // tn), d_ff // tn), tn,
                 epi="swiglu", out_dtype=BF16, tm_pref=1024)
        tn2 = _pick_tile(d, 512)
        return mm(act, p["w_ffn_out"], (lyr, slot), ((0,), d // tn2), tn2, epi="res",
                  res_pair=x2, mod9s=mod9s, idx=idx, coef=0.5, w_buffers=1)

    for lyr in range(depth):
        mod9s = [m[lyr] for m in mods]
        x2 = ffn(x2, mod9s, lyr, 0, 0)
        h = norm(x2, p["norm_g"][lyr, 1], mod9s, 1)
        mix = [None, None]
        if lyr % 2 == 0:
            e = lyr // 2
            even_in = p["w_in_even"].shape[2]
            rw_in = p["rw_mu"].shape[1]
            tn = _pick_tile(even_in, 1280, 256)
            q = prep["even"][e]
            zin = mm(h, q["w_in"], (), ((0,), even_in // tn), tn, w_buffers=1)
            for i in groups:
                rw_s, rw_sh, s5_re, s5_im = sts[i][:4]
                z = zin[i].reshape(*dims[i], even_in)
                oa, sh, s = _rwkv7(z, rw_in, rw_sh[e], rw_s[e], q)
                ob, hr, hi = _s5(z, rw_in, s5_re[e], s5_im[e], p, e)
                for slot, val in zip((0, 1, 2, 3), (s, sh, hr, hi)):
                    new_st[i][slot].append(val)
                mix[i] = (oa.reshape(rows[i], -1), ob.reshape(rows[i], -1))
            w_out, widx = q["w_out"], ()
        else:
            o = lyr // 2
            w_gdn, w_hg, w_ab = prep["odd"][o]
            gdn_main = w_gdn.shape[1]
            tn = _pick_tile(gdn_main, 512)
            zmain = mm(h, w_gdn, (), ((0,), gdn_main // tn), tn, tm_pref=1024)
            zab = mm(h, w_ab, (), ((0,), 1), LANES)
            hg_in = w_hg.shape[1]
            tn = _pick_tile(hg_in, 512)
            zhg = mm(h, w_hg, (), ((0,), hg_in // tn), tn, tm_pref=1024)
            for i in groups:
                gdn_s, gdn_cv, hg_s = sts[i][4:]
                oc, cv, s = _gdn(zmain[i].reshape(*dims[i], gdn_main), zab[i].reshape(*dims[i], LANES),
                                 gdn_cv[o], gdn_s[o], p, o, chunks[i])
                od, sh_ = _hgrn2(zhg[i].reshape(*dims[i], hg_in), hg_s[o], p, o, lyr, chunks[i])
                for slot, val in zip((4, 5, 6), (s, cv, sh_)):
                    new_st[i][slot].append(val)
                mix[i] = (oc.reshape(rows[i], -1), od.reshape(rows[i], -1))
            w_out, widx = p["w_out_odd"], (o,)
        tn = _pick_tile(d, 512)
        x2 = mm(mix, w_out, widx, ((0,), d // tn), tn, epi="res", res_pair=x2, mod9s=mod9s, idx=1,
                coef=1.0, tm_pref=1024)
        x2 = ffn(x2, mod9s, lyr, 2, 1)
    ys = [_norm(x2[i].reshape(*dims[i], d), p["norm_final"], out_dtype=xs[i].dtype) for i in groups]
    return ys, [tuple(jnp.stack(v) for v in new_st[i]) for i in groups]


def kernel(x_prompt, x_sample, c_prompt, c_sample, state_rwkv, state_rwkv_shift, state_s5_re, state_s5_im, state_gdn, cache_gdn_conv, state_hgrn, w_mod, b_mod, norm_g, norm_final, w_ffn_in, w_ffn_out, w_in_even, w_out_even, rw_mu, rw_w0, rw_w_up, rw_a0, rw_a_up, rw_g_up, rw_k_k, rw_k_a, rw_r_k, rw_ln_w, rw_ln_b, s5_lambda_re, s5_lambda_im, s5_log_dt, s5_b_re, s5_b_im, s5_c_re, s5_c_im, s5_d, s5_glu_w, s5_glu_b, w_in_odd, w_out_odd, gdn_conv_w, gdn_a_log, gdn_dt_bias, gdn_norm_w, hg_lb_logits, hg_norm_w):
    p = dict(w_mod=w_mod, b_mod=b_mod, norm_g=norm_g, norm_final=norm_final,
             w_ffn_in=w_ffn_in, w_ffn_out=w_ffn_out, w_in_even=w_in_even, w_out_even=w_out_even,
             rw_mu=rw_mu, rw_w0=rw_w0, rw_w_up=rw_w_up, rw_a0=rw_a0, rw_a_up=rw_a_up, rw_g_up=rw_g_up,
             rw_k_k=rw_k_k, rw_k_a=rw_k_a, rw_r_k=rw_r_k, rw_ln_w=rw_ln_w, rw_ln_b=rw_ln_b,
             s5_lambda_re=s5_lambda_re, s5_lambda_im=s5_lambda_im, s5_log_dt=s5_log_dt,
             s5_b_re=s5_b_re, s5_b_im=s5_b_im, s5_c_re=s5_c_re, s5_c_im=s5_c_im, s5_d=s5_d,
             s5_glu_w=s5_glu_w, s5_glu_b=s5_glu_b, w_in_odd=w_in_odd, w_out_odd=w_out_odd,
             gdn_conv_w=gdn_conv_w, gdn_a_log=gdn_a_log, gdn_dt_bias=gdn_dt_bias, gdn_norm_w=gdn_norm_w,
             hg_lb_logits=hg_lb_logits, hg_norm_w=hg_norm_w)
    bp, bs = x_prompt.shape[0], x_sample.shape[0]
    d = x_prompt.shape[-1]
    depth = w_mod.shape[0]
    n_even, n_odd = (depth + 1) // 2, depth // 2
    dtp = x_prompt.dtype

    bc = bp + bs
    bc_pad = -(-bc // BF16_SUBLANES) * BF16_SUBLANES
    c_all = jnp.pad(jnp.concatenate([c_prompt, c_sample], axis=0), ((0, bc_pad - bc), (0, 0)))
    mod = _modulation(c_all, w_mod, b_mod).reshape(depth, bc_pad, 9, d)
    mod_p, mod_s = mod[:, :bp], mod[:, bp:bc]

    heads = gdn_a_log.shape[1]
    conv_ch = gdn_conv_w.shape[2]
    vw = heads * gdn_norm_w.shape[1]
    gdn_main = conv_ch + vw
    gdn_in = gdn_main + 2 * heads
    prep = {"odd": [], "even": [_rw_prep(p, e) for e in range(n_even)]}
    for o in range(n_odd):
        w_gdn = w_in_odd[o][:, :gdn_main]
        w_hg = w_in_odd[o][:, gdn_in:]
        w_ab = jnp.pad(w_in_odd[o][:, gdn_main:gdn_in], ((0, 0), (0, LANES - 2 * heads)))
        prep["odd"].append((w_gdn, w_hg, w_ab))

    rw_h, rw_n = rw_r_k.shape[1], rw_r_k.shape[2]
    rw_in = rw_mu.shape[1]
    s5_g, s5_n = s5_lambda_re.shape[1], s5_lambda_re.shape[2]
    gdn_dk = (conv_ch - vw) // 2 // heads
    gdn_dv = gdn_norm_w.shape[1]
    kconv = gdn_conv_w.shape[1]
    hg_dv = hg_norm_w.shape[1]
    hg_kw = hg_lb_logits.shape[1]
    hg_vw = (w_in_odd.shape[2] - gdn_in - 2 * hg_kw) // 2
    hg_h = hg_vw // hg_dv
    hg_dk = hg_kw // hg_h
    st_prompt = (jnp.zeros((n_even, bp, rw_h, rw_n, rw_n), dtp),
                 jnp.zeros((n_even, bp, rw_in), dtp),
                 jnp.zeros((n_even, bp, s5_g, s5_n), dtp),
                 jnp.zeros((n_even, bp, s5_g, s5_n), dtp),
                 jnp.zeros((n_odd, bp, heads, gdn_dk, gdn_dv), dtp),
                 jnp.zeros((n_odd, bp, kconv - 1, conv_ch), dtp),
                 jnp.zeros((n_odd, bp, hg_h, hg_dk, hg_dv), dtp))
    st_sample = (state_rwkv, state_rwkv_shift, state_s5_re, state_s5_im, state_gdn, cache_gdn_conv, state_hgrn)
    ys, ns = _trunk((x_prompt, x_sample), (mod_p, mod_s), (st_prompt, st_sample), p, prep)
    return tuple(ys) + tuple(ns[0]) + tuple(ns[1])
```

```python
import functools

import jax
import jax.numpy as jnp
from jax import lax
from jax.experimental import pallas as pl
from jax.experimental.pallas import tpu as pltpu

F32 = jnp.float32
BF16 = jnp.bfloat16
HI = lax.Precision.HIGHEST

EPS = 1e-6
RW_GN_EPS = 64e-5
L2_EPS = 1e-12
CHUNK = 64
SUB = 16
HGRN_HEADS_PER_STEP = 16
MM_ROW_CHUNK = 256
BF16_SUBLANES = 16
LANES = 128
SUBLANES = 8
VMEM_LIMIT = 56 * 1024 * 1024


def _params(sem, big=False):
    return pltpu.CompilerParams(dimension_semantics=sem,
                                vmem_limit_bytes=VMEM_LIMIT if big else None)


def _pick_tile(n, pref, mult=LANES):
    t = (min(pref, n) // mult) * mult
    while t >= mult:
        if n % t == 0:
            return t
        t -= mult
    return n


def _mm(a, b, hi=False):
    if not hi:
        a, b = a.astype(BF16), b.astype(BF16)
    return lax.dot_general(a, b, (((1,), (0,)), ((), ())), precision=HI if hi else None,
                           preferred_element_type=F32)


def _mm_nt(a, b, hi=False):
    if not hi:
        a, b = a.astype(BF16), b.astype(BF16)
    return lax.dot_general(a, b, (((1,), (1,)), ((), ())), precision=HI if hi else None,
                           preferred_element_type=F32)


def _mm_tn(a, b, hi=False):
    if not hi:
        a, b = a.astype(BF16), b.astype(BF16)
    return lax.dot_general(a, b, (((0,), (0,)), ((), ())), precision=HI if hi else None,
                           preferred_element_type=F32)


def _split2(a):
    hi = a.astype(BF16)
    return hi, (a - hi.astype(F32)).astype(BF16)


def _split3(a):
    hi = a.astype(BF16)
    r = a - hi.astype(F32)
    mid = r.astype(BF16)
    return hi, mid, (r - mid.astype(F32)).astype(BF16)


def _bdot(a, b, dims):
    return lax.dot_general(a, b, (dims, ((), ())), preferred_element_type=F32)


def _mm3(a, b, dims=((1,), (0,))):
    ah, al = _split2(a)
    bh, bl = _split2(b)
    return _bdot(ah, bh, dims) + (_bdot(ah, bl, dims) + _bdot(al, bh, dims))


def _mm01(mask, x):
    m = mask.astype(BF16)
    xh, xm, xl = _split3(x)
    dims = ((1,), (0,))
    return _bdot(m, xh, dims) + (_bdot(m, xm, dims) + _bdot(m, xl, dims))


def _x01(x, mask):
    m = mask.astype(BF16)
    xh, xm, xl = _split3(x)
    dims = ((1,), (0,))
    return _bdot(xh, m, dims) + (_bdot(xm, m, dims) + _bdot(xl, m, dims))


def _sigmoid(x):
    return 1.0 / (1.0 + jnp.exp(-x))


def _silu(x):
    return x * _sigmoid(x)


def _softplus(x):
    return jnp.maximum(x, 0.0) + jnp.log(1.0 + jnp.exp(-jnp.abs(x)))


def _iota2(shape, axis):
    return lax.broadcasted_iota(jnp.int32, shape, axis)


def _mod_kernel(c_ref, w_ref, b_ref, o_ref):
    s = _silu(c_ref[...]).astype(BF16)
    o_ref[0] = jnp.dot(s, w_ref[0].astype(BF16), preferred_element_type=F32) + b_ref[0]


def _modulation(c, w_mod, b_mod):
    depth, d, n = w_mod.shape
    bc = c.shape[0]
    tn = _pick_tile(n, 512)
    return pl.pallas_call(
        _mod_kernel,
        grid=(depth, n // tn),
        in_specs=[pl.BlockSpec((bc, d), lambda l, j: (0, 0)),
                  pl.BlockSpec((1, d, tn), lambda l, j: (l, 0, j)),
                  pl.BlockSpec((1, 1, tn), lambda l, j: (l, 0, j))],
        out_specs=pl.BlockSpec((1, bc, tn), lambda l, j: (l, 0, j)),
        out_shape=jax.ShapeDtypeStruct((depth, bc, n), F32),
        compiler_params=_params(("arbitrary", "arbitrary"), big=True),
        name="modulation",
    )(c, w_mod, b_mod.reshape(depth, 1, n))


def _norm_kernel(x_ref, g_ref, *rest, idx):
    if idx is None:
        (o_ref,) = rest
    else:
        m_ref, o_ref = rest
    x = x_ref[0]
    y = x * lax.rsqrt(jnp.mean(x * x, axis=-1, keepdims=True) + EPS) * g_ref[...]
    if idx is not None:
        shift = m_ref[0, pl.ds(3 * idx, 1), :]
        scale = m_ref[0, pl.ds(3 * idx + 1, 1), :]
        y = y * (1.0 + scale) + shift
    o_ref[0] = y.astype(o_ref.dtype)


def _norm(x, g, mod9=None, idx=None, out_dtype=BF16):
    b, l, d = x.shape
    tl = _pick_tile(l, 256, SUBLANES)
    in_specs = [pl.BlockSpec((1, tl, d), lambda i, t: (i, t, 0)),
                pl.BlockSpec((1, d), lambda i, t: (0, 0))]
    args = [x, g.reshape(1, d)]
    if idx is not None:
        in_specs.append(pl.BlockSpec((1, mod9.shape[1], d), lambda i, t: (i, 0, 0)))
        args.append(mod9)
    return pl.pallas_call(
        functools.partial(_norm_kernel, idx=idx),
        grid=(b, l // tl),
        in_specs=in_specs,
        out_specs=pl.BlockSpec((1, tl, d), lambda i, t: (i, t, 0)),
        out_shape=jax.ShapeDtypeStruct((b, l, d), out_dtype),
        compiler_params=_params(("arbitrary", "arbitrary")),
        name="norm",
    )(*args)


def _mm_kernel(*refs, n_x, n_w, epi, coef, rider):
    it = iter(refs)
    take = lambda n: [next(it) for _ in range(n)]
    x_refs, w_refs = take(n_x), take(n_w)
    res_ref, gate_ref = take(2) if epi == "res" else (None, None)
    if rider:
        x2_refs = take(n_x)
        res2_ref, gate2_ref = take(2) if epi == "res" else (None, None)
    (o_ref,) = take(1)
    if rider:
        (o2_ref,) = take(1)
    wb_refs = take(n_w)

    def rows(xr, res_r, gate_r, out_r):
        m = out_r.shape[0]
        step = MM_ROW_CHUNK if m % MM_ROW_CHUNK == 0 else m
        for r0 in range(0, m, step):
            rs = pl.ds(r0, step)
            acc = []
            for wb_ref in wb_refs:
                k0, a = 0, None
                for x_ref in xr:
                    kx = x_ref.shape[1]
                    part = jnp.dot(x_ref[rs, :], wb_ref[pl.ds(k0, kx), :], preferred_element_type=F32)
                    a = part if a is None else a + part
                    k0 += kx
                acc.append(a)
            if epi == "swiglu":
                out = _silu(acc[0]) * acc[1]
            elif epi == "res":
                gate = gate_r[0] if gate_r.shape[1] == 1 else gate_r[0, rs, :]
                out = res_r[rs, :] + coef * gate * acc[0]
            else:
                out = acc[0]
            out_r[rs, :] = out.astype(out_r.dtype)

    @pl.when(pl.program_id(1) == 0)
    def _():
        for w_ref, wb_ref in zip(w_refs, wb_refs):
            wb_ref[...] = w_ref[...].astype(BF16)
        if rider:
            rows(x2_refs, res2_ref, gate2_ref, o2_ref)

    rows(x_refs, res_ref, gate_ref, o_ref)


def _matmul(x, w, prefix, col_blocks, tn, *, rows_per_gate, res=None, gate=None, rider=None,
            coef=1.0, epi="plain", out_dtype=F32, tm_pref=512, w_buffers=2):
    xs = x if isinstance(x, (tuple, list)) else (x,)
    m = xs[0].shape[0]
    k = sum(t.shape[1] for t in xs)
    offs, n_cols = col_blocks
    n_w = len(offs)
    n_out = n_cols * tn
    tm = _pick_tile(rows_per_gate, tm_pref, SUBLANES)
    rpt = rows_per_gate // tm
    npre = len(prefix)
    wmode = {} if w_buffers == 2 else {"pipeline_mode": pl.Buffered(w_buffers)}
    in_specs = [pl.BlockSpec((tm, t.shape[1]), lambda j, i: (i, 0)) for t in xs]
    args = list(xs)
    for off in offs:
        in_specs.append(pl.BlockSpec((None,) * npre + (k, tn),
                                     lambda j, i, off=off: tuple(prefix) + (0, j + off), **wmode))
        args.append(w)
    if epi == "res":
        in_specs += [pl.BlockSpec((tm, tn), lambda j, i: (i, j)),
                     pl.BlockSpec((1, 1, tn), lambda j, i: (i // rpt, 0, j))]
        args += [res, gate]
    out_specs = [pl.BlockSpec((tm, tn), lambda j, i: (i, j))]
    out_shape = [jax.ShapeDtypeStruct((m, n_out), out_dtype)]
    if rider is not None:
        x2, res2, gate2 = rider
        x2s = x2 if isinstance(x2, (tuple, list)) else (x2,)
        m2 = x2s[0].shape[0]
        in_specs += [pl.BlockSpec((m2, t.shape[1]), lambda j, i: (0, 0)) for t in x2s]
        args += list(x2s)
        if epi == "res":
            in_specs += [pl.BlockSpec((m2, tn), lambda j, i: (0, j)),
                         pl.BlockSpec((1, m2, tn), lambda j, i: (0, 0, j))]
            args += [res2, gate2]
        out_specs.append(pl.BlockSpec((m2, tn), lambda j, i: (0, j)))
        out_shape.append(jax.ShapeDtypeStruct((m2, n_out), out_dtype))
    outs = pl.pallas_call(
        functools.partial(_mm_kernel, n_x=len(xs), n_w=n_w, epi=epi, coef=coef, rider=rider is not None),
        grid=(n_cols, m // tm),
        in_specs=in_specs,
        out_specs=out_specs,
        out_shape=out_shape,
        scratch_shapes=[pltpu.VMEM((k, tn), BF16) for _ in range(n_w)],
        compiler_params=_params(("arbitrary", "arbitrary"), big=True),
        name="matmul_" + epi,
    )(*args)
    return tuple(outs) if rider is not None else (outs[0], None)


def _group_ones(n):
    r = _iota2((LANES, LANES), 0)
    c = _iota2((LANES, LANES), 1)
    sh = n.bit_length() - 1
    return (jnp.right_shift(r, sh) == jnp.right_shift(c, sh)).astype(F32)


def _group_sum(x, n):
    if n % LANES == 0:
        parts = []
        for h in range(x.shape[1] // n):
            s = jnp.sum(x[:, h * n:(h + 1) * n], axis=-1, keepdims=True)
            parts.append(jnp.broadcast_to(s, (x.shape[0], n)))
        return parts[0] if len(parts) == 1 else jnp.concatenate(parts, axis=1)
    ones = _group_ones(n)
    parts = [_x01(x[:, s * LANES:(s + 1) * LANES], ones) for s in range(x.shape[1] // LANES)]
    return parts[0] if len(parts) == 1 else jnp.concatenate(parts, axis=1)


def _tril(c, strict=False):
    r = _iota2((c, c), 0)
    s = _iota2((c, c), 1)
    return (r > s) if strict else (r >= s)


def _unit_lower_solve(lms, rhss):
    c = lms[0].shape[0]
    r = _iota2((c, c), 0)
    s = _iota2((c, c), 1)
    eye = (r == s).astype(F32)
    if c <= SUB:
        lds, los = lms, None
    else:
        sh = SUB.bit_length() - 1
        same = jnp.right_shift(r, sh) == jnp.right_shift(s, sh)
        lds = [jnp.where(same, lm, 0.0) for lm in lms]
        los = [lm - ld for lm, ld in zip(lms, lds)]
    order = min(SUB, c)
    xs = [-ld for ld in lds]
    ps = [eye + x for x in xs]
    pws = [_mm(x, x) for x in xs]
    k = 2
    while k < order:
        new_ps = [p + _mm(pw, p) for p, pw in zip(ps, pws)]
        if 2 * k < order:
            pws = [_mm(pw, pw) for pw in pws]
        ps = new_ps
        k *= 2
    levels = []
    if los is not None:
        pws = [-_mm(p, lo) for p, lo in zip(ps, los)]
        k = 1
        while True:
            levels.append(pws)
            if 2 * k >= c // SUB:
                break
            pws = [_mm(pw, pw) for pw in pws]
            k *= 2

    def apply(vs):
        sols = [_mm(p, v) for p, v in zip(ps, vs)]
        for lvl in levels:
            sols = [sol + _mm(pw, sol) for sol, pw in zip(sols, lvl)]
        return sols

    sols = apply(rhss)
    resid = [rhs - (sol + _mm3(lm, sol)) for rhs, sol, lm in zip(rhss, sols, lms)]
    return [sol + cor for sol, cor in zip(sols, apply(resid))]


def _head_rmsnorm(o, w):
    return o * lax.rsqrt(jnp.mean(o * o, axis=-1, keepdims=True) + EPS) * w


def _token_shift(z, prev_row, mu):
    prev = jnp.where(_iota2(z.shape, 0) == 0, prev_row, pltpu.roll(z, 1, 0))
    return z + mu * (prev - z)


_RW_R, _RW_D, _RW_K, _RW_V, _RW_KK, _RW_NKKA = range(6)


def _window_transpose4_a(s, roll=None):
    roll = roll or (lambda x, k: pltpu.roll(x, k, 1))
    half = s[0].shape[1] // 2
    lo_half = _iota2(s[0].shape, 1) < half
    return (jnp.where(lo_half, s[0], roll(s[2], half)), jnp.where(lo_half, s[1], roll(s[3], half)),
            jnp.where(lo_half, roll(s[0], half), s[2]), jnp.where(lo_half, roll(s[1], half), s[3]))


def _window_transpose4_b(a, roll=None):
    roll = roll or (lambda x, k: pltpu.roll(x, k, 1))
    w = a[0].shape[1]
    quarter = w // 4
    even_win = jnp.bitwise_and(_iota2(a[0].shape, 1), quarter) == 0
    return (jnp.where(even_win, a[0], roll(a[1], quarter)), jnp.where(even_win, roll(a[0], w - quarter), a[1]),
            jnp.where(even_win, a[2], roll(a[3], quarter)), jnp.where(even_win, roll(a[2], w - quarter), a[3]))


def _window_transpose4(s, roll=None):
    return _window_transpose4_b(_window_transpose4_a(s, roll), roll)


def _rw_scan_kernel(zr_ref, zk_ref, zv_ref, zl_ref, mu_ref, sh_ref, w0_ref, a0_ref, wup_ref, aup_ref,
                    gup_ref, par_ref, s0_ref, out_ref, s_ref, zs_scr, prev_scr, seq_scr, o_scr, bs_scr,
                    *, steps, pitch, n, heads, bpl):
    @pl.when(pl.program_id(1) == 0)
    def _():
        s_ref[...] = s0_ref[...]
        prev_scr[...] = sh_ref[...]

    used = bpl * heads
    width = n * heads
    last = pl.ds(steps - 1, 1)
    for qi, z_ref in enumerate((zr_ref, zk_ref, zv_ref)):
        cols = slice(qi * width, (qi + 1) * width)
        for b in range(bpl):
            zs_scr[qi, b] = _token_shift(z_ref[b], prev_scr[b, :, cols], mu_ref[:, cols])
            prev_scr[b, :, cols] = z_ref[b, last, :]
    lcols = slice(3 * width, 3 * width + zl_ref.shape[2])
    shifted = []
    for b in range(bpl):
        shifted.append(_token_shift(zl_ref[b], prev_scr[b, :, lcols], mu_ref[:, lcols]))
        prev_scr[b, :, lcols] = zl_ref[b, last, :]
    zl = shifted[0] if bpl == 1 else jnp.concatenate(shifted, axis=0)
    w_lora, a_lora = wup_ref.shape[0], aup_ref.shape[0]
    xw, xa, xg = zl[:, :w_lora], zl[:, w_lora:w_lora + a_lora], zl[:, w_lora + a_lora:]
    w = -_softplus(-(w0_ref[...] + _mm(jnp.tanh(xw), wup_ref[...]))) - 0.5
    dag = (jnp.exp(-jnp.exp(w)), _sigmoid(a0_ref[...] + _mm(xa, aup_ref[...])), _mm(_sigmoid(xg), gup_ref[...]))
    for qi, val in enumerate(dag):
        for b in range(bpl):
            zs_scr[3 + qi, b] = val[b * steps:(b + 1) * steps]
    d_ref, a_ref, g_ref = zs_scr.at[3], zs_scr.at[4], zs_scr.at[5]

    def to_lanes(ref, j):
        pieces = [ref[b, :, j * heads:(j + 1) * heads] for b in range(bpl)]
        if used < LANES:
            pieces.append(jnp.zeros((steps, LANES - used), F32))
        return pieces[0] if len(pieces) == 1 else jnp.concatenate(pieces, axis=1)

    per = LANES // heads
    fast = bpl == 4 and per == 4

    def half_slabs(ref, j0):
        if fast:
            cols = slice((j0 // per) * LANES, (j0 // per + 1) * LANES)
            return _window_transpose4_a([ref[b, :, cols] for b in range(bpl)])
        return [to_lanes(ref, j0)]

    r_ref, k_ref, v_ref = zs_scr.at[0], zs_scr.at[1], zs_scr.at[2]
    sources = (r_ref, k_ref, a_ref, d_ref, v_ref)
    kkp, kap, rkp, lnw, lnb = (par_ref[i] for i in range(5))
    acc = [None, None]

    def finish(j0, halves):
        rs, ks, as_, ds_, vs_ = (_window_transpose4_b(h) if fast else h for h in halves)
        for w, (r, k, a) in enumerate(zip(rs, ks, as_)):
            j = j0 + w
            rows = pl.ds(j * pitch, steps)
            kraw = k * kkp[j:j + 1]
            k2 = k * (1.0 + (a - 1.0) * kap[j:j + 1])
            seq_scr[_RW_R, rows, :] = r
            seq_scr[_RW_D, rows, :] = ds_[w]
            seq_scr[_RW_V, rows, :] = vs_[w]
            seq_scr[_RW_K, rows, :] = k2
            seq_scr[_RW_KK, rows, :] = kraw
            seq_scr[_RW_NKKA, rows, :] = a
            for slot, term in enumerate((kraw * kraw, r * k2 * rkp[j:j + 1])):
                acc[slot] = term if acc[slot] is None else acc[slot] + term

    pending = None
    for j0 in range(0, n, per if fast else 1):
        halves = [half_slabs(ref, j0) for ref in sources]
        if pending is not None:
            finish(*pending)
        pending = (j0, halves)
    finish(*pending)
    sumsq, bsum = acc
    inv = lax.rsqrt(sumsq + L2_EPS)
    bs_scr[...] = bsum
    for j in range(n):
        rows = pl.ds(j * pitch, steps)
        kk = seq_scr[_RW_KK, rows, :] * inv
        seq_scr[_RW_KK, rows, :] = kk
        seq_scr[_RW_NKKA, rows, :] = -(kk * seq_scr[_RW_NKKA, rows, :])

    def row(q, j, t):
        return seq_scr[q, pl.ds(j * pitch + t, 1), :]

    sa0 = s_ref[0] * row(_RW_KK, 0, 0)
    for j in range(1, n):
        sa0 = sa0 + s_ref[j] * row(_RW_KK, j, 0)

    def step(t, sa):
        tn = jnp.minimum(t + 1, steps - 1)
        tile = pl.ds(t, n, stride=pitch)
        v_t = seq_scr[_RW_V, tile, :]
        o = None
        sa_next = None
        for j in range(n):
            sj = s_ref[j] * row(_RW_D, j, t) + sa * row(_RW_NKKA, j, t) + v_t * row(_RW_K, j, t)
            s_ref[j] = sj
            term = sj * row(_RW_R, j, t)
            o = term if o is None else o + term
            nxt = sj * row(_RW_KK, j, tn)
            sa_next = nxt if sa_next is None else sa_next + nxt
        mu = jnp.mean(o, axis=0, keepdims=True)
        c = o - mu
        var = jnp.mean(c * c, axis=0, keepdims=True)
        y = c * lax.rsqrt(var + RW_GN_EPS) * lnw + lnb
        o_scr[tile, :] = y + bs_scr[pl.ds(t, 1), :] * v_t
        return sa_next

    lax.fori_loop(0, steps, step, sa0)

    def emit(qd, tiles):
        cols = slice(qd * LANES, (qd + 1) * LANES)
        if fast:
            outs = _window_transpose4_b(tiles)
        else:
            outs = []
            for b in range(bpl):
                pieces = [t[:, b * heads:(b + 1) * heads] for t in tiles]
                outs.append(pieces[0] if per == 1 else jnp.concatenate(pieces, axis=1))
        for b in range(bpl):
            out_ref[b, :, cols] = (outs[b] * g_ref[b, :, cols]).astype(out_ref.dtype)

    pending = None
    for qd in range(n // per):
        tiles = [o_scr[pl.ds(i * pitch, steps), :] for i in range(qd * per, (qd + 1) * per)]
        if fast:
            tiles = _window_transpose4_a(tiles)
        if pending is not None:
            emit(*pending)
        pending = (qd, tiles)
    emit(*pending)


def _rw_scan(zin, rw_in, shift, q, par, s0, bpl):
    b, l, _ = zin.shape
    heads, n = q["heads"], q["n"]
    width = heads * n
    lw = rw_in - 3 * width
    assert (3 * width) % lw == 0 and lw % LANES == 0
    steps = _pick_tile(l, 32, 2 * SUBLANES)
    pitch = steps + SUBLANES
    z_spec = lambda qi: pl.BlockSpec((bpl, steps, width), lambda c, t: (c, t, qi))
    row = lambda w_: pl.BlockSpec((1, w_), lambda c, t: (0, 0))
    full = lambda t_: pl.BlockSpec(t_.shape, lambda c, t: (0, 0))
    st_spec = pl.BlockSpec((n, n, LANES), lambda c, t: (0, 0, c))
    return pl.pallas_call(
        functools.partial(_rw_scan_kernel, steps=steps, pitch=pitch, n=n, heads=heads, bpl=bpl),
        grid=(b // bpl, l // steps),
        in_specs=[z_spec(0), z_spec(1), z_spec(2),
                  pl.BlockSpec((bpl, steps, lw), lambda c, t: (c, t, 3 * width // lw)),
                  row(rw_in), pl.BlockSpec((bpl, 1, rw_in), lambda c, t: (c, 0, 0)),
                  row(width), row(width), full(q["w_up"]), full(q["a_up"]), full(q["g_up"]),
                  pl.BlockSpec((5, n, LANES), lambda c, t: (0, 0, 0)), st_spec],
        out_specs=[pl.BlockSpec((bpl, steps, width), lambda c, t: (c, t, 0)), st_spec],
        out_shape=[jax.ShapeDtypeStruct((b, l, width), BF16), jax.ShapeDtypeStruct(s0.shape, F32)],
        scratch_shapes=[pltpu.VMEM((6, bpl, steps, width), F32), pltpu.VMEM((bpl, 1, rw_in), F32),
                        pltpu.VMEM((6, n * pitch, LANES), F32), pltpu.VMEM((n * pitch, LANES), F32),
                        pltpu.VMEM((steps, LANES), F32)],
        compiler_params=_params(("arbitrary", "arbitrary"), big=True),
        name="rwkv_scan",
    )(zin, zin, zin, zin, q["mu"], shift.reshape(b, 1, rw_in), q["w0"], q["a0"],
      q["w_up"], q["a_up"], q["g_up"], par, s0)


def _index_major(t, heads, n, blocks=1, inverse=False):
    width = heads * n
    a, c = (n, heads) if inverse else (heads, n)
    parts = []
    for i in range(blocks):
        blk = t[..., i * width:(i + 1) * width]
        parts.append(blk.reshape(blk.shape[:-1] + (a, c)).swapaxes(-1, -2).reshape(blk.shape))
    if t.shape[-1] > blocks * width:
        parts.append(t[..., blocks * width:])
    return parts[0] if len(parts) == 1 else jnp.concatenate(parts, axis=-1)


def _rw_prep(p, e):
    heads, n = p["rw_r_k"].shape[1], p["rw_r_k"].shape[2]
    im = functools.partial(_index_major, heads=heads, n=n)
    w_out = p["w_out_even"][e]
    return dict(heads=heads, n=n,
                w_in=im(p["w_in_even"][e], blocks=3), w_out=jnp.swapaxes(im(jnp.swapaxes(w_out, 0, 1)), 0, 1),
                mu=im(p["rw_mu"][e:e + 1], blocks=3), w0=im(p["rw_w0"][e:e + 1]), a0=im(p["rw_a0"][e:e + 1]),
                w_up=im(p["rw_w_up"][e]), a_up=im(p["rw_a_up"][e]), g_up=im(p["rw_g_up"][e]),
                lane=[t.reshape(heads, n).T for t in (p["rw_k_k"][e], p["rw_k_a"][e], p["rw_r_k"][e],
                                                      p["rw_ln_w"][e], p["rw_ln_b"][e])])


def _rwkv7(zin, rw_in, shift, s0, q):
    b, l, _ = zin.shape
    heads, n = q["heads"], q["n"]
    bpl = min(b, LANES // heads)
    nlb, used = b // bpl, bpl * heads
    shift_im = _index_major(shift.astype(F32), heads, n, blocks=3)
    par = jnp.stack([jnp.pad(jnp.tile(t, (1, bpl)), ((0, 0), (0, LANES - used))) for t in q["lane"]])
    s0_l = s0.astype(F32).transpose(3, 2, 0, 1).reshape(n, n, nlb, used)
    s0_l = jnp.pad(s0_l, ((0, 0), (0, 0), (0, 0), (0, LANES - used))).reshape(n, n, nlb * LANES)
    out, s_l = _rw_scan(zin, rw_in, shift_im, q, par, s0_l, bpl)
    s_fin = s_l.reshape(n, n, nlb, LANES)[..., :used].reshape(n, n, b, heads).transpose(2, 3, 1, 0)
    new_shift = _index_major(zin[:, -1, :rw_in], heads, n, blocks=3, inverse=True)
    return out, new_shift.astype(shift.dtype), s_fin.astype(s0.dtype)


def _s5_param_kernel(ldt_ref, lr_ref, li_ref, bre_ref, bim_ref, are_o, aim_o, bbre_o, bbim_o):
    dt = jnp.exp(ldt_ref[...])
    lr, li = lr_ref[...], li_ref[...]
    mag = jnp.exp(lr * dt)
    ab_re, ab_im = mag * jnp.cos(li * dt), mag * jnp.sin(li * dt)
    den = lr * lr + li * li
    pr, pi_ = ab_re - 1.0, ab_im
    coef_re = (pr * lr + pi_ * li) / den
    coef_im = (pi_ * lr - pr * li) / den
    are_o[...] = ab_re
    aim_o[...] = ab_im
    for m in range(bre_ref.shape[0]):
        b_re, b_im = bre_ref[m], bim_ref[m]
        bbre_o[m] = coef_re * b_re - coef_im * b_im
        bbim_o[m] = coef_re * b_im + coef_im * b_re


def _s5_params(p, e):
    g, n, m = p["s5_b_re"].shape[1:]
    outs = pl.pallas_call(
        _s5_param_kernel,
        out_shape=[jax.ShapeDtypeStruct((g, n), F32)] * 2 + [jax.ShapeDtypeStruct((m, g, n), F32)] * 2,
        name="s5_params",
    )(p["s5_log_dt"][e].reshape(g, 1), p["s5_lambda_re"][e], p["s5_lambda_im"][e],
      p["s5_b_re"][e].transpose(2, 0, 1), p["s5_b_im"][e].transpose(2, 0, 1))
    return outs


def _block_diag(t, gs):
    g, a, b = t.shape
    t = t.reshape(g // gs, gs, a, b)
    eye = jnp.eye(gs, dtype=t.dtype)
    return jnp.einsum("sgab,gh->sgahb", t, eye).reshape(g // gs, gs * a, gs * b)


def _s5_kernel(u_ref, bre_ref, bim_ref, cre_ref, cim_ref, d_ref, gw_ref, gb_ref, are_ref, aim_ref,
               h0r_ref, h0i_ref, o_ref, hr_ref, hi_ref, u_scr, xr_scr, xi_scr, *, steps, pitch):
    @pl.when(pl.program_id(1) == 0)
    def _():
        hr_ref[...] = h0r_ref[...]
        hi_ref[...] = h0i_ref[...]

    nb = u_ref.shape[0]
    nk = xr_scr.shape[0]
    lanes = [slice(k * LANES, (k + 1) * LANES) for k in range(nk)]
    for b in range(nb):
        u_scr[pl.ds(b * pitch, steps), :] = u_ref[b]
        u_scr[pl.ds(b * pitch + steps, pitch - steps), :] = jnp.zeros((pitch - steps, LANES), F32)
    u2 = u_scr[...]
    bu_re = _mm(u2, bre_ref[0])
    bu_im = _mm(u2, bim_ref[0])
    for k in range(nk):
        xr_scr[k] = bu_re[:, lanes[k]]
        xi_scr[k] = bu_im[:, lanes[k]]
    a_re = [are_ref[0, :, lanes[k]] for k in range(nk)]
    a_im = [aim_ref[0, :, lanes[k]] for k in range(nk)]

    def step(t, carry):
        rows = pl.ds(t, nb, stride=pitch)
        new = []
        for k in range(nk):
            h_re, h_im = carry[2 * k], carry[2 * k + 1]
            n_re = a_re[k] * h_re - a_im[k] * h_im + xr_scr[k, rows, :]
            n_im = a_re[k] * h_im + a_im[k] * h_re + xi_scr[k, rows, :]
            xr_scr[k, rows, :] = n_re
            xi_scr[k, rows, :] = n_im
            new += [n_re, n_im]
        return tuple(new)

    init = tuple(r[:, lanes[k]] for k in range(nk) for r in (hr_ref, hi_ref))
    fin = lax.fori_loop(0, steps, step, init)
    for k in range(nk):
        hr_ref[:, lanes[k]] = fin[2 * k]
        hi_ref[:, lanes[k]] = fin[2 * k + 1]
    hs_re = jnp.concatenate([xr_scr[k] for k in range(nk)], axis=1) if nk > 1 else xr_scr[0]
    hs_im = jnp.concatenate([xi_scr[k] for k in range(nk)], axis=1) if nk > 1 else xi_scr[0]
    y = _mm(hs_re, cre_ref[0]) - _mm(hs_im, cim_ref[0]) + d_ref[0] * u2
    yg = 0.5 * y * (1.0 + jnp.tanh(0.7978845608028654 * (y + 0.044715 * (y * y * y))))
    out = yg * _sigmoid(_mm(yg, gw_ref[0]) + gb_ref[0])
    for b in range(nb):
        o_ref[b] = out[b * pitch:b * pitch + steps].astype(o_ref.dtype)


def _s5(zin, col0, h_re, h_im, p, e):
    b, l, _ = zin.shape
    g, n, m = p["s5_b_re"].shape[1:]
    width = g * m
    gs = LANES // m
    ns = g // gs
    sn = gs * n
    ab_re, ab_im, bb_re, bb_im = _s5_params(p, e)
    bd_bre = _block_diag(bb_re.transpose(1, 0, 2), gs)
    bd_bim = _block_diag(bb_im.transpose(1, 0, 2), gs)
    bd_cre = _block_diag(p["s5_c_re"][e].transpose(0, 2, 1), gs)
    bd_cim = _block_diag(p["s5_c_im"][e].transpose(0, 2, 1), gs)
    bd_gw = _block_diag(p["s5_glu_w"][e], gs)
    steps = _pick_tile(l, 256, 2 * SUBLANES)
    pitch = steps + SUBLANES
    cb0 = col0 // LANES
    slab = lambda r, c: pl.BlockSpec((1, r, c), lambda s, t: (s, 0, 0))
    st_spec = pl.BlockSpec((b, sn), lambda s, t: (0, s))
    out, hr, hi = pl.pallas_call(
        functools.partial(_s5_kernel, steps=steps, pitch=pitch),
        grid=(ns, l // steps),
        in_specs=[pl.BlockSpec((b, steps, LANES), lambda s, t: (0, t, cb0 + s)),
                  slab(LANES, sn), slab(LANES, sn), slab(sn, LANES), slab(sn, LANES),
                  slab(1, LANES), slab(LANES, LANES), slab(1, LANES), slab(1, sn), slab(1, sn),
                  st_spec, st_spec],
        out_specs=[pl.BlockSpec((b, steps, LANES), lambda s, t: (0, t, s)), st_spec, st_spec],
        out_shape=[jax.ShapeDtypeStruct((b, l, width), BF16),
                   jax.ShapeDtypeStruct((b, g * n), F32), jax.ShapeDtypeStruct((b, g * n), F32)],
        scratch_shapes=[pltpu.VMEM((b * pitch, LANES), F32),
                        pltpu.VMEM((sn // LANES, b * pitch, LANES), F32),
                        pltpu.VMEM((sn // LANES, b * pitch, LANES), F32)],
        compiler_params=_params(("arbitrary", "arbitrary"), big=True),
        name="s5",
    )(zin, bd_bre, bd_bim, bd_cre, bd_cim, p["s5_d"][e].reshape(ns, 1, LANES), bd_gw,
      p["s5_glu_b"][e].reshape(ns, 1, LANES), ab_re.reshape(ns, 1, sn), ab_im.reshape(ns, 1, sn),
      h_re.astype(F32).reshape(b, g * n), h_im.astype(F32).reshape(b, g * n))
    return (out, hr.reshape(b, g, n).astype(h_re.dtype), hi.reshape(b, g, n).astype(h_im.dtype))


def _gdn_kernel(q_ref, k_ref, v_ref, y_ref, wq_ref, wk_ref, wv_ref, bq_ref, bk_ref, bv_ref,
                zab_ref, alog_ref, dtb_ref, nw_ref, s0_ref, o_ref, s_ref, cq_scr, ck_scr, cv_scr,
                *, hb, dk, dv, kconv):
    t = pl.program_id(2)

    @pl.when(t == 0)
    def _():
        s_ref[0] = s0_ref[0]
        cq_scr[...] = bq_ref[0]
        ck_scr[...] = bk_ref[0]
        cv_scr[...] = bv_ref[0]

    c = q_ref.shape[1]

    def conv(x_ref, w_ref, carry_scr):
        x = x_ref[0]
        full = jnp.concatenate([carry_scr[...], x], axis=0)
        carry_scr[...] = x_ref[0, pl.ds(c - SUBLANES, SUBLANES), :]
        acc = None
        for j in range(kconv):
            sh = kconv - 1 - j
            src = full if sh == 0 else pltpu.roll(full, sh, 0)
            term = src[SUBLANES:SUBLANES + c] * w_ref[pl.ds(j, 1), :]
            acc = term if acc is None else acc + term
        return _silu(acc)

    qa = conv(q_ref, wq_ref, cq_scr)
    ka = conv(k_ref, wk_ref, ck_scr)
    va = conv(v_ref, wv_ref, cv_scr)
    incl = _tril(c)
    strict = _tril(c, strict=True)
    hs = range(hb)
    qs, ks, vs = [], [], []
    for hl in hs:
        q = qa[:, hl * dk:(hl + 1) * dk]
        k = ka[:, hl * dk:(hl + 1) * dk]
        qs.append(q * lax.rsqrt(jnp.sum(q * q, axis=-1, keepdims=True) + L2_EPS) * (dk ** -0.5))
        ks.append(k * lax.rsqrt(jnp.sum(k * k, axis=-1, keepdims=True) + L2_EPS))
        vs.append(va[:, hl * dv:(hl + 1) * dv])
    z = zab_ref[0]
    gcum = _mm01(incl.astype(F32), -jnp.exp(alog_ref[...]) * _softplus(z + dtb_ref[...]))
    beta = _sigmoid(z)
    gcum_t = gcum.T
    gc_c = [gcum[:, hl:hl + 1] for hl in hs]
    beta_c = [beta[:, hb + hl:hb + hl + 1] for hl in hs]
    gc_r = [gcum_t[hl:hl + 1, :] for hl in hs]
    g_last = [g[:, c - 1:c] for g in gc_r]
    kbs = [k * b for k, b in zip(ks, beta_c)]
    dmask = [jnp.where(incl, jnp.exp(jnp.where(incl, gc - gr, 0.0)), 0.0) for gc, gr in zip(gc_c, gc_r)]
    kk = [_mm_nt(kb, k) for kb, k in zip(kbs, ks)]
    qk = [_mm_nt(q, k) for q, k in zip(qs, ks)]
    ms = [jnp.where(strict, a * d, 0.0) for a, d in zip(kk, dmask)]
    egc = [jnp.exp(g) for g in gc_c]
    rhss = [jnp.concatenate([v * b, kb * e], axis=1) for v, b, kb, e in zip(vs, beta_c, kbs, egc)]
    sols = _unit_lower_solve(ms, rhss)
    attn = [a * d for a, d in zip(qk, dmask)]
    s_old = [s_ref[0, hl] for hl in hs]
    ws = [_mm(sol[:, dv:], s) for sol, s in zip(sols, s_old)]
    qss = [_mm(q * e, s) for q, e, s in zip(qs, egc, s_old)]
    v_new = [sol[:, :dv] - w for sol, w in zip(sols, ws)]
    av = [_mm(a, vn) for a, vn in zip(attn, v_new)]
    kdec = [k * jnp.exp(gl - gc) for k, gl, gc in zip(ks, g_last, gc_c)]
    kv = [_mm_tn(kd, vn) for kd, vn in zip(kdec, v_new)]
    states = [s * jnp.exp(gl) + x for s, gl, x in zip(s_old, g_last, kv)]
    outs = [_head_rmsnorm(a + b, nw_ref[...]) * _silu(y_ref[0, :, hl * dv:(hl + 1) * dv])
            for hl, a, b in zip(hs, qss, av)]
    o_ref[0] = (outs[0] if hb == 1 else jnp.concatenate(outs, axis=1)).astype(o_ref.dtype)
    s_ref[0] = jnp.stack(states, axis=0)


def _gdn(zmain, zab, conv_buf, s0, p, o, chunk):
    b, l, _ = zmain.shape
    heads = p["gdn_a_log"].shape[1]
    dv = p["gdn_norm_w"].shape[1]
    kconv, conv_ch = p["gdn_conv_w"].shape[1:]
    vw = heads * dv
    kw = (conv_ch - vw) // 2
    dk = kw // heads
    hb = heads
    nhb = heads // hb
    nc = l // chunk
    lane_row = lambda t: jnp.pad(t, (0, LANES - heads)).reshape(1, LANES)
    cbuf = jnp.pad(conv_buf.astype(F32), ((0, 0), (SUBLANES - (kconv - 1), 0), (0, 0)))
    wq, wk = hb * dk, hb * dk
    wv = hb * dv
    qoff, koff, voff, yoff = 0, kw // wk, 2 * kw // wv, (2 * kw + vw) // wv
    col = lambda w_, off: pl.BlockSpec((1, chunk, w_), lambda i, h, t: (i, t, off + h))
    cw = lambda w_, off: pl.BlockSpec((None, kconv, w_), lambda i, h, t: (o, 0, off + h))
    cb = lambda w_, off: pl.BlockSpec((1, SUBLANES, w_), lambda i, h, t: (i, 0, off + h))
    st = pl.BlockSpec((1, hb, dk, dv), lambda i, h, t: (i, h, 0, 0))
    out, s_fin = pl.pallas_call(
        functools.partial(_gdn_kernel, hb=hb, dk=dk, dv=dv, kconv=kconv),
        grid=(b, nhb, nc),
        in_specs=[col(wq, qoff), col(wk, koff), col(wv, voff), col(wv, yoff),
                  cw(wq, qoff), cw(wk, koff), cw(wv, voff),
                  cb(wq, qoff), cb(wk, koff), cb(wv, voff),
                  pl.BlockSpec((1, chunk, LANES), lambda i, h, t: (i, t, 0)),
                  pl.BlockSpec((1, LANES), lambda i, h, t: (0, 0)),
                  pl.BlockSpec((1, LANES), lambda i, h, t: (0, 0)),
                  pl.BlockSpec((1, dv), lambda i, h, t: (0, 0)),
                  st],
        out_specs=[pl.BlockSpec((1, chunk, wv), lambda i, h, t: (i, t, h)), st],
        out_shape=[jax.ShapeDtypeStruct((b, l, vw), BF16), jax.ShapeDtypeStruct((b, heads, dk, dv), F32)],
        scratch_shapes=[pltpu.VMEM((SUBLANES, wq), F32), pltpu.VMEM((SUBLANES, wk), F32),
                        pltpu.VMEM((SUBLANES, wv), F32)],
        compiler_params=_params(("arbitrary", "arbitrary", "arbitrary")),
        name="gdn",
    )(zmain, zmain, zmain, zmain, p["gdn_conv_w"], p["gdn_conv_w"], p["gdn_conv_w"],
      cbuf, cbuf, cbuf, zab, lane_row(p["gdn_a_log"][o]), lane_row(p["gdn_dt_bias"][o]),
      p["gdn_norm_w"][o:o + 1], s0.astype(F32))
    tail = jnp.concatenate([conv_buf.astype(F32), zmain[:, -(kconv - 1):, :conv_ch]], axis=1)[:, -(kconv - 1):]
    return out, tail.astype(conv_buf.dtype), s_fin.astype(s0.dtype)


def _hgrn_kernel(q_ref, f_ref, i_ref, og_ref, lbl_ref, nw_ref, s0_ref, o_ref, s_ref, *, hb, dk, dv, layer):
    @pl.when(pl.program_id(2) == 0)
    def _():
        s_ref[0] = s0_ref[0]

    c = q_ref.shape[1]
    logits = lbl_ref[...]
    mx = jnp.max(logits, axis=0, keepdims=True)
    ex = jnp.exp(logits - mx)
    den = jnp.sum(ex, axis=0, keepdims=True)
    lb_all = jnp.zeros_like(den)
    for r in range(1, layer + 1):
        lb_all = lb_all + ex[r:r + 1] / den
    tril_f = _tril(c).astype(F32)
    nb = max(c // SUB, 1)
    sb = min(SUB, c)
    rowi = _iota2((sb, 1), 0)
    eye = _iota2((dk, dk), 0) == _iota2((dk, dk), 1)
    hs = range(hb)
    ksl = [slice(hl * dk, (hl + 1) * dk) for hl in hs]
    vsl = [slice(hl * dv, (hl + 1) * dv) for hl in hs]
    q = q_ref[0]
    f = lb_all + (1.0 - lb_all) * _sigmoid(f_ref[0])
    k = 1.0 - f
    v = i_ref[0]
    bcum = _mm01(tril_f, jnp.log(f))
    qe = q * jnp.exp(bcum)
    s_old = [s_ref[0, hl] for hl in hs]
    inter = [_mm(qe[:, ksl[hl]], s_old[hl]) for hl in hs]
    blocks = [[] for _ in hs]
    for bi in range(nb):
        r0 = bi * sb
        qi, ki, vi, bb = q[r0:r0 + sb], k[r0:r0 + sb], v[r0:r0 + sb], bcum[r0:r0 + sb]
        pieces = [[] for _ in hs]
        for p0 in range(0, sb, SUBLANES):
            p1 = min(p0 + SUBLANES, sb)
            qp, bp_ = qi[p0:p1], bb[p0:p1]
            acc = [inter[hl][r0 + p0:r0 + p1] for hl in hs]
            for si in range(p1):
                diff = bp_ - bb[si:si + 1]
                if si > p0:
                    msk = rowi[p0:p1] >= si
                    dec = jnp.where(msk, jnp.exp(jnp.where(msk, diff, 0.0)), 0.0)
                else:
                    dec = jnp.exp(diff)
                prod = qp * ki[si:si + 1] * dec
                for hl in hs:
                    colv = jnp.sum(prod[:, ksl[hl]], axis=-1, keepdims=True)
                    acc[hl] = acc[hl] + colv * vi[si:si + 1, vsl[hl]]
            for hl in hs:
                pieces[hl].append(acc[hl])
        acc = [pc[0] if len(pc) == 1 else jnp.concatenate(pc, axis=0) for pc in pieces]
        if bi > 0:
            bref = bcum[r0 - 1:r0]
            qsc = qi * jnp.exp(bb - bref)
            ksc = k[:r0] * jnp.exp(bref - bcum[:r0])
            sc = [_mm3(qsc[:, ksl[hl]], ksc[:, ksl[hl]], ((1,), (1,))) for hl in hs]
            acc = [acc[hl] + _mm(sc[hl], v[:r0, vsl[hl]]) for hl in hs]
        for hl in hs:
            blocks[hl].append(acc[hl])
    b_last = bcum[c - 1:c]
    kd = k * jnp.exp(b_last - bcum)
    e_last = jnp.exp(b_last)
    kv = [_mm_tn(kd[:, ksl[hl]], v[:, vsl[hl]]) for hl in hs]
    states = []
    outs = []
    og = og_ref[0]
    for hl in hs:
        e_col = jnp.sum(jnp.where(eye, e_last[:, ksl[hl]], 0.0), axis=1, keepdims=True)
        states.append(e_col * s_old[hl] + kv[hl])
        o = blocks[hl][0] if nb == 1 else jnp.concatenate(blocks[hl], axis=0)
        outs.append(_head_rmsnorm(o, nw_ref[...]) * _sigmoid(og[:, vsl[hl]]))
    o_ref[0] = (outs[0] if hb == 1 else jnp.concatenate(outs, axis=1)).astype(o_ref.dtype)
    s_ref[0] = jnp.stack(states, axis=0)


def _hgrn2(z, s0, p, o, layer, chunk):
    b, l, zw = z.shape
    dv = p["hg_norm_w"].shape[1]
    depth, kw = p["hg_lb_logits"].shape
    vw = (zw - 2 * kw) // 2
    heads = vw // dv
    dk = kw // heads
    hb = _pick_tile(heads, HGRN_HEADS_PER_STEP, 1)
    nhb = heads // hb
    wk, wv = hb * dk, hb * dv
    col = lambda w_, off: pl.BlockSpec((1, chunk, w_), lambda i, h, t: (i, t, off + h))
    st = pl.BlockSpec((1, hb, dk, dv), lambda i, h, t: (i, h, 0, 0))
    out, s_fin = pl.pallas_call(
        functools.partial(_hgrn_kernel, hb=hb, dk=dk, dv=dv, layer=layer),
        grid=(b, nhb, l // chunk),
        in_specs=[col(wk, 0), col(wk, kw // wk), col(wv, 2 * kw // wv), col(wv, (2 * kw + vw) // wv),
                  pl.BlockSpec((depth, wk), lambda i, h, t: (0, h)),
                  pl.BlockSpec((1, dv), lambda i, h, t: (0, 0)),
                  st],
        out_specs=[pl.BlockSpec((1, chunk, wv), lambda i, h, t: (i, t, h)), st],
        out_shape=[jax.ShapeDtypeStruct((b, l, vw), BF16), jax.ShapeDtypeStruct((b, heads, dk, dv), F32)],
        compiler_params=_params(("arbitrary", "arbitrary", "arbitrary")),
        name="hgrn2",
    )(z, z, z, z, p["hg_lb_logits"].astype(F32), p["hg_norm_w"][o:o + 1], s0.astype(F32))
    return out, s_fin.astype(s0.dtype)


def _trunk(xs, mods, sts, p, prep):
    groups = (0, 1)
    dims = [x.shape[:2] for x in xs]
    d = xs[0].shape[2]
    rows = [b * l for b, l in dims]
    depth = p["w_mod"].shape[0]
    d_ff = p["w_ffn_out"].shape[2]
    chunks = [min(CHUNK, l) for _, l in dims]
    new_st = [[[] for _ in range(7)] for _ in groups]
    x2 = [x.reshape(r, d) for x, r in zip(xs, rows)]

    def mm(x_pair, w, prefix, cols, tn, *, epi="plain", res_pair=None, mod9s=None, idx=None, **kw):
        res2 = gate = gate2 = None
        if epi == "res":
            gate = mod9s[0][:, 3 * idx + 2].reshape(dims[0][0], 1, d)
            gate2 = jnp.repeat(mod9s[1][:, 3 * idx + 2], dims[1][1], axis=0).reshape(1, rows[1], d)
            res2 = res_pair[1]
        return _matmul(x_pair[0], w, prefix, cols, tn, epi=epi, rows_per_gate=dims[0][1],
                       res=None if res_pair is None else res_pair[0], gate=gate,
                       rider=(x_pair[1], res2, gate2), **kw)

    def norm(x2, g, mod9s, idx):
        return [_norm(x2[i].reshape(*dims[i], d), g, mod9s[i], idx).reshape(rows[i], d) for i in groups]

    def ffn(x2, mod9s, lyr, idx, slot):
        h = norm(x2, p["norm_g"][lyr, idx], mod9s, idx)
        tn = _pick_tile(d_ff, 512)
        act = mm(h, p["w_ffn_in"], (lyr, slot), ((0, d_ff // tn), d_ff // tn), tn,
                 epi="swiglu", out_dtype=BF16, tm_pref=1024, w_buffers=1)
        tn2 = _pick_tile(d, 512)
        return mm(act, p["w_ffn_out"], (lyr, slot), ((0,), d // tn2), tn2, epi="res",
                  res_pair=x2, mod9s=mod9s, idx=idx, coef=0.5, w_buffers=1)

    for lyr in range(depth):
        mod9s = [m[lyr] for m in mods]
        x2 = ffn(x2, mod9s, lyr, 0, 0)
        h = norm(x2, p["norm_g"][lyr, 1], mod9s, 1)
        mix = [None, None]
        if lyr % 2 == 0:
            e = lyr // 2
            even_in = p["w_in_even"].shape[2]
            rw_in = p["rw_mu"].shape[1]
            tn = _pick_tile(even_in, 1280, 256)
            q = prep["even"][e]
            zin = mm(h, q["w_in"], (), ((0,), even_in // tn), tn, w_buffers=1)
            for i in groups:
                rw_s, rw_sh, s5_re, s5_im = sts[i][:4]
                z = zin[i].reshape(*dims[i], even_in)
                oa, sh, s = _rwkv7(z, rw_in, rw_sh[e], rw_s[e], q)
                ob, hr, hi = _s5(z, rw_in, s5_re[e], s5_im[e], p, e)
                for slot, val in zip((0, 1, 2, 3), (s, sh, hr, hi)):
                    new_st[i][slot].append(val)
                mix[i] = (oa.reshape(rows[i], -1), ob.reshape(rows[i], -1))
            w_out, widx = q["w_out"], ()
        else:
            o = lyr // 2
            w_gdn, w_hg, w_ab = prep["odd"][o]
            gdn_main = w_gdn.shape[1]
            tn = _pick_tile(gdn_main, 512)
            zmain = mm(h, w_gdn, (), ((0,), gdn_main // tn), tn, tm_pref=1024)
            zab = mm(h, w_ab, (), ((0,), 1), LANES)
            hg_in = w_hg.shape[1]
            tn = _pick_tile(hg_in, 512)
            zhg = mm(h, w_hg, (), ((0,), hg_in // tn), tn, tm_pref=1024)
            for i in groups:
                gdn_s, gdn_cv, hg_s = sts[i][4:]
                oc, cv, s = _gdn(zmain[i].reshape(*dims[i], gdn_main), zab[i].reshape(*dims[i], LANES),
                                 gdn_cv[o], gdn_s[o], p, o, chunks[i])
                od, sh_ = _hgrn2(zhg[i].reshape(*dims[i], hg_in), hg_s[o], p, o, lyr, chunks[i])
                for slot, val in zip((4, 5, 6), (s, cv, sh_)):
                    new_st[i][slot].append(val)
                mix[i] = (oc.reshape(rows[i], -1), od.reshape(rows[i], -1))
            w_out, widx = p["w_out_odd"], (o,)
        tn = _pick_tile(d, 512)
        x2 = mm(mix, w_out, widx, ((0,), d // tn), tn, epi="res", res_pair=x2, mod9s=mod9s, idx=1,
                coef=1.0, tm_pref=1024)
        x2 = ffn(x2, mod9s, lyr, 2, 1)
    ys = [_norm(x2[i].reshape(*dims[i], d), p["norm_final"], out_dtype=xs[i].dtype) for i in groups]
    return ys, [tuple(jnp.stack(v) for v in new_st[i]) for i in groups]


def kernel(x_prompt, x_sample, c_prompt, c_sample, state_rwkv, state_rwkv_shift, state_s5_re, state_s5_im, state_gdn, cache_gdn_conv, state_hgrn, w_mod, b_mod, norm_g, norm_final, w_ffn_in, w_ffn_out, w_in_even, w_out_even, rw_mu, rw_w0, rw_w_up, rw_a0, rw_a_up, rw_g_up, rw_k_k, rw_k_a, rw_r_k, rw_ln_w, rw_ln_b, s5_lambda_re, s5_lambda_im, s5_log_dt, s5_b_re, s5_b_im, s5_c_re, s5_c_im, s5_d, s5_glu_w, s5_glu_b, w_in_odd, w_out_odd, gdn_conv_w, gdn_a_log, gdn_dt_bias, gdn_norm_w, hg_lb_logits, hg_norm_w):
    p = dict(w_mod=w_mod, b_mod=b_mod, norm_g=norm_g, norm_final=norm_final,
             w_ffn_in=w_ffn_in, w_ffn_out=w_ffn_out, w_in_even=w_in_even, w_out_even=w_out_even,
             rw_mu=rw_mu, rw_w0=rw_w0, rw_w_up=rw_w_up, rw_a0=rw_a0, rw_a_up=rw_a_up, rw_g_up=rw_g_up,
             rw_k_k=rw_k_k, rw_k_a=rw_k_a, rw_r_k=rw_r_k, rw_ln_w=rw_ln_w, rw_ln_b=rw_ln_b,
             s5_lambda_re=s5_lambda_re, s5_lambda_im=s5_lambda_im, s5_log_dt=s5_log_dt,
             s5_b_re=s5_b_re, s5_b_im=s5_b_im, s5_c_re=s5_c_re, s5_c_im=s5_c_im, s5_d=s5_d,
             s5_glu_w=s5_glu_w, s5_glu_b=s5_glu_b, w_in_odd=w_in_odd, w_out_odd=w_out_odd,
             gdn_conv_w=gdn_conv_w, gdn_a_log=gdn_a_log, gdn_dt_bias=gdn_dt_bias, gdn_norm_w=gdn_norm_w,
             hg_lb_logits=hg_lb_logits, hg_norm_w=hg_norm_w)
    bp, bs = x_prompt.shape[0], x_sample.shape[0]
    d = x_prompt.shape[-1]
    depth = w_mod.shape[0]
    n_even, n_odd = (depth + 1) // 2, depth // 2
    dtp = x_prompt.dtype

    bc = bp + bs
    bc_pad = -(-bc // BF16_SUBLANES) * BF16_SUBLANES
    c_all = jnp.pad(jnp.concatenate([c_prompt, c_sample], axis=0), ((0, bc_pad - bc), (0, 0)))
    mod = _modulation(c_all, w_mod, b_mod).reshape(depth, bc_pad, 9, d)
    mod_p, mod_s = mod[:, :bp], mod[:, bp:bc]

    heads = gdn_a_log.shape[1]
    conv_ch = gdn_conv_w.shape[2]
    vw = heads * gdn_norm_w.shape[1]
    gdn_main = conv_ch + vw
    gdn_in = gdn_main + 2 * heads
    prep = {"odd": [], "even": [_rw_prep(p, e) for e in range(n_even)]}
    for o in range(n_odd):
        w_gdn = w_in_odd[o][:, :gdn_main]
        w_hg = w_in_odd[o][:, gdn_in:]
        w_ab = jnp.pad(w_in_odd[o][:, gdn_main:gdn_in], ((0, 0), (0, LANES - 2 * heads)))
        prep["odd"].append((w_gdn, w_hg, w_ab))

    rw_h, rw_n = rw_r_k.shape[1], rw_r_k.shape[2]
    rw_in = rw_mu.shape[1]
    s5_g, s5_n = s5_lambda_re.shape[1], s5_lambda_re.shape[2]
    gdn_dk = (conv_ch - vw) // 2 // heads
    gdn_dv = gdn_norm_w.shape[1]
    kconv = gdn_conv_w.shape[1]
    hg_dv = hg_norm_w.shape[1]
    hg_kw = hg_lb_logits.shape[1]
    hg_vw = (w_in_odd.shape[2] - gdn_in - 2 * hg_kw) // 2
    hg_h = hg_vw // hg_dv
    hg_dk = hg_kw // hg_h
    st_prompt = (jnp.zeros((n_even, bp, rw_h, rw_n, rw_n), dtp),
                 jnp.zeros((n_even, bp, rw_in), dtp),
                 jnp.zeros((n_even, bp, s5_g, s5_n), dtp),
                 jnp.zeros((n_even, bp, s5_g, s5_n), dtp),
                 jnp.zeros((n_odd, bp, heads, gdn_dk, gdn_dv), dtp),
                 jnp.zeros((n_odd, bp, kconv - 1, conv_ch), dtp),
                 jnp.zeros((n_odd, bp, hg_h, hg_dk, hg_dv), dtp))
    st_sample = (state_rwkv, state_rwkv_shift, state_s5_re, state_s5_im, state_gdn, cache_gdn_conv, state_hgrn)
    ys, ns = _trunk((x_prompt, x_sample), (mod_p, mod_s), (st_prompt, st_sample), p, prep)
    return tuple(ys) + tuple(ns[0]) + tuple(ns[1])
```

```python
import functools

import jax
import jax.numpy as jnp
from jax import lax
from jax.experimental import pallas as pl
from jax.experimental.pallas import tpu as pltpu

F32 = jnp.float32
BF16 = jnp.bfloat16
HI = lax.Precision.HIGHEST

EPS = 1e-6
RW_GN_EPS = 64e-5
L2_EPS = 1e-12
CHUNK = 64
SUB = 16
HGRN_HEADS_PER_STEP = 16
MM_ROW_CHUNK = 256
BF16_SUBLANES = 16
LANES = 128
SUBLANES = 8
VMEM_LIMIT = 56 * 1024 * 1024


def _params(sem, big=False):
    return pltpu.CompilerParams(dimension_semantics=sem,
                                vmem_limit_bytes=VMEM_LIMIT if big else None)


def _pick_tile(n, pref, mult=LANES):
    t = (min(pref, n) // mult) * mult
    while t >= mult:
        if n % t == 0:
            return t
        t -= mult
    return n


def _mm(a, b, hi=False):
    if not hi:
        a, b = a.astype(BF16), b.astype(BF16)
    return lax.dot_general(a, b, (((1,), (0,)), ((), ())), precision=HI if hi else None,
                           preferred_element_type=F32)


def _mm_nt(a, b, hi=False):
    if not hi:
        a, b = a.astype(BF16), b.astype(BF16)
    return lax.dot_general(a, b, (((1,), (1,)), ((), ())), precision=HI if hi else None,
                           preferred_element_type=F32)


def _mm_tn(a, b, hi=False):
    if not hi:
        a, b = a.astype(BF16), b.astype(BF16)
    return lax.dot_general(a, b, (((0,), (0,)), ((), ())), precision=HI if hi else None,
                           preferred_element_type=F32)


def _split2(a):
    hi = a.astype(BF16)
    return hi, (a - hi.astype(F32)).astype(BF16)


def _split3(a):
    hi = a.astype(BF16)
    r = a - hi.astype(F32)
    mid = r.astype(BF16)
    return hi, mid, (r - mid.astype(F32)).astype(BF16)


def _bdot(a, b, dims):
    return lax.dot_general(a, b, (dims, ((), ())), preferred_element_type=F32)


def _mm3(a, b, dims=((1,), (0,))):
    ah, al = _split2(a)
    bh, bl = _split2(b)
    return _bdot(ah, bh, dims) + (_bdot(ah, bl, dims) + _bdot(al, bh, dims))


def _mm01(mask, x):
    m = mask.astype(BF16)
    xh, xm, xl = _split3(x)
    dims = ((1,), (0,))
    return _bdot(m, xh, dims) + (_bdot(m, xm, dims) + _bdot(m, xl, dims))


def _x01(x, mask):
    m = mask.astype(BF16)
    xh, xm, xl = _split3(x)
    dims = ((1,), (0,))
    return _bdot(xh, m, dims) + (_bdot(xm, m, dims) + _bdot(xl, m, dims))


def _sigmoid(x):
    return 1.0 / (1.0 + jnp.exp(-x))


def _silu(x):
    return x * _sigmoid(x)


def _softplus(x):
    return jnp.maximum(x, 0.0) + jnp.log(1.0 + jnp.exp(-jnp.abs(x)))


def _iota2(shape, axis):
    return lax.broadcasted_iota(jnp.int32, shape, axis)


def _mod_kernel(c_ref, w_ref, b_ref, o_ref):
    s = _silu(c_ref[...]).astype(BF16)
    o_ref[0] = jnp.dot(s, w_ref[0].astype(BF16), preferred_element_type=F32) + b_ref[0]


def _modulation(c, w_mod, b_mod):
    depth, d, n = w_mod.shape
    bc = c.shape[0]
    tn = _pick_tile(n, 512)
    return pl.pallas_call(
        _mod_kernel,
        grid=(depth, n // tn),
        in_specs=[pl.BlockSpec((bc, d), lambda l, j: (0, 0)),
                  pl.BlockSpec((1, d, tn), lambda l, j: (l, 0, j)),
                  pl.BlockSpec((1, 1, tn), lambda l, j: (l, 0, j))],
        out_specs=pl.BlockSpec((1, bc, tn), lambda l, j: (l, 0, j)),
        out_shape=jax.ShapeDtypeStruct((depth, bc, n), F32),
        compiler_params=_params(("arbitrary", "arbitrary"), big=True),
        name="modulation",
    )(c, w_mod, b_mod.reshape(depth, 1, n))


def _norm_kernel(x_ref, g_ref, *rest, idx):
    if idx is None:
        (o_ref,) = rest
    else:
        m_ref, o_ref = rest
    x = x_ref[0]
    y = x * lax.rsqrt(jnp.mean(x * x, axis=-1, keepdims=True) + EPS) * g_ref[...]
    if idx is not None:
        shift = m_ref[0, pl.ds(3 * idx, 1), :]
        scale = m_ref[0, pl.ds(3 * idx + 1, 1), :]
        y = y * (1.0 + scale) + shift
    o_ref[0] = y.astype(o_ref.dtype)


def _norm(x, g, mod9=None, idx=None, out_dtype=BF16):
    b, l, d = x.shape
    tl = _pick_tile(l, 256, SUBLANES)
    in_specs = [pl.BlockSpec((1, tl, d), lambda i, t: (i, t, 0)),
                pl.BlockSpec((1, d), lambda i, t: (0, 0))]
    args = [x, g.reshape(1, d)]
    if idx is not None:
        in_specs.append(pl.BlockSpec((1, mod9.shape[1], d), lambda i, t: (i, 0, 0)))
        args.append(mod9)
    return pl.pallas_call(
        functools.partial(_norm_kernel, idx=idx),
        grid=(b, l // tl),
        in_specs=in_specs,
        out_specs=pl.BlockSpec((1, tl, d), lambda i, t: (i, t, 0)),
        out_shape=jax.ShapeDtypeStruct((b, l, d), out_dtype),
        compiler_params=_params(("arbitrary", "arbitrary")),
        name="norm",
    )(*args)


def _mm_kernel(*refs, n_x, n_w, epi, coef, rider):
    it = iter(refs)
    take = lambda n: [next(it) for _ in range(n)]
    x_refs, w_refs = take(n_x), take(n_w)
    res_ref, gate_ref = take(2) if epi == "res" else (None, None)
    if rider:
        x2_refs = take(n_x)
        res2_ref, gate2_ref = take(2) if epi == "res" else (None, None)
    (o_ref,) = take(1)
    if rider:
        (o2_ref,) = take(1)
    wb_refs = take(n_w)

    def rows(xr, res_r, gate_r, out_r):
        m = out_r.shape[0]
        step = MM_ROW_CHUNK if m % MM_ROW_CHUNK == 0 else m
        for r0 in range(0, m, step):
            rs = pl.ds(r0, step)
            acc = []
            for wb_ref in wb_refs:
                k0, a = 0, None
                for x_ref in xr:
                    kx = x_ref.shape[1]
                    part = jnp.dot(x_ref[rs, :], wb_ref[pl.ds(k0, kx), :], preferred_element_type=F32)
                    a = part if a is None else a + part
                    k0 += kx
                acc.append(a)
            if epi == "swiglu":
                out = _silu(acc[0]) * acc[1]
            elif epi == "res":
                gate = gate_r[0] if gate_r.shape[1] == 1 else gate_r[0, rs, :]
                out = res_r[rs, :] + coef * gate * acc[0]
            else:
                out = acc[0]
            out_r[rs, :] = out.astype(out_r.dtype)

    @pl.when(pl.program_id(1) == 0)
    def _():
        for w_ref, wb_ref in zip(w_refs, wb_refs):
            wb_ref[...] = w_ref[...].astype(BF16)
        if rider:
            rows(x2_refs, res2_ref, gate2_ref, o2_ref)

    rows(x_refs, res_ref, gate_ref, o_ref)


def _matmul(x, w, prefix, col_blocks, tn, *, rows_per_gate, res=None, gate=None, rider=None,
            coef=1.0, epi="plain", out_dtype=F32, tm_pref=512, w_buffers=2):
    xs = x if isinstance(x, (tuple, list)) else (x,)
    m = xs[0].shape[0]
    k = sum(t.shape[1] for t in xs)
    offs, n_cols = col_blocks
    n_w = len(offs)
    n_out = n_cols * tn
    tm = _pick_tile(rows_per_gate, tm_pref, SUBLANES)
    rpt = rows_per_gate // tm
    npre = len(prefix)
    wmode = {} if w_buffers == 2 else {"pipeline_mode": pl.Buffered(w_buffers)}
    in_specs = [pl.BlockSpec((tm, t.shape[1]), lambda j, i: (i, 0)) for t in xs]
    args = list(xs)
    for off in offs:
        in_specs.append(pl.BlockSpec((None,) * npre + (k, tn),
                                     lambda j, i, off=off: tuple(prefix) + (0, j + off), **wmode))
        args.append(w)
    if epi == "res":
        in_specs += [pl.BlockSpec((tm, tn), lambda j, i: (i, j)),
                     pl.BlockSpec((1, 1, tn), lambda j, i: (i // rpt, 0, j))]
        args += [res, gate]
    out_specs = [pl.BlockSpec((tm, tn), lambda j, i: (i, j))]
    out_shape = [jax.ShapeDtypeStruct((m, n_out), out_dtype)]
    if rider is not None:
        x2, res2, gate2 = rider
        x2s = x2 if isinstance(x2, (tuple, list)) else (x2,)
        m2 = x2s[0].shape[0]
        in_specs += [pl.BlockSpec((m2, t.shape[1]), lambda j, i: (0, 0)) for t in x2s]
        args += list(x2s)
        if epi == "res":
            in_specs += [pl.BlockSpec((m2, tn), lambda j, i: (0, j)),
                         pl.BlockSpec((1, m2, tn), lambda j, i: (0, 0, j))]
            args += [res2, gate2]
        out_specs.append(pl.BlockSpec((m2, tn), lambda j, i: (0, j)))
        out_shape.append(jax.ShapeDtypeStruct((m2, n_out), out_dtype))
    outs = pl.pallas_call(
        functools.partial(_mm_kernel, n_x=len(xs), n_w=n_w, epi=epi, coef=coef, rider=rider is not None),
        grid=(n_cols, m // tm),
        in_specs=in_specs,
        out_specs=out_specs,
        out_shape=out_shape,
        scratch_shapes=[pltpu.VMEM((k, tn), BF16) for _ in range(n_w)],
        compiler_params=_params(("arbitrary", "arbitrary"), big=True),
        name="matmul_" + epi,
    )(*args)
    return tuple(outs) if rider is not None else (outs[0], None)


def _group_ones(n):
    r = _iota2((LANES, LANES), 0)
    c = _iota2((LANES, LANES), 1)
    sh = n.bit_length() - 1
    return (jnp.right_shift(r, sh) == jnp.right_shift(c, sh)).astype(F32)


def _group_sum(x, n):
    if n % LANES == 0:
        parts = []
        for h in range(x.shape[1] // n):
            s = jnp.sum(x[:, h * n:(h + 1) * n], axis=-1, keepdims=True)
            parts.append(jnp.broadcast_to(s, (x.shape[0], n)))
        return parts[0] if len(parts) == 1 else jnp.concatenate(parts, axis=1)
    ones = _group_ones(n)
    parts = [_x01(x[:, s * LANES:(s + 1) * LANES], ones) for s in range(x.shape[1] // LANES)]
    return parts[0] if len(parts) == 1 else jnp.concatenate(parts, axis=1)


def _tril(c, strict=False):
    r = _iota2((c, c), 0)
    s = _iota2((c, c), 1)
    return (r > s) if strict else (r >= s)


def _unit_lower_solve(lms, rhss):
    c = lms[0].shape[0]
    r = _iota2((c, c), 0)
    s = _iota2((c, c), 1)
    eye = (r == s).astype(F32)
    if c <= SUB:
        lds, los = lms, None
    else:
        sh = SUB.bit_length() - 1
        same = jnp.right_shift(r, sh) == jnp.right_shift(s, sh)
        lds = [jnp.where(same, lm, 0.0) for lm in lms]
        los = [lm - ld for lm, ld in zip(lms, lds)]
    order = min(SUB, c)
    xs = [-ld for ld in lds]
    ps = [eye + x for x in xs]
    pws = [_mm(x, x) for x in xs]
    k = 2
    while k < order:
        new_ps = [p + _mm(pw, p) for p, pw in zip(ps, pws)]
        if 2 * k < order:
            pws = [_mm(pw, pw) for pw in pws]
        ps = new_ps
        k *= 2
    levels = []
    if los is not None:
        pws = [-_mm(p, lo) for p, lo in zip(ps, los)]
        k = 1
        while True:
            levels.append(pws)
            if 2 * k >= c // SUB:
                break
            pws = [_mm(pw, pw) for pw in pws]
            k *= 2

    def apply(vs):
        sols = [_mm(p, v) for p, v in zip(ps, vs)]
        for lvl in levels:
            sols = [sol + _mm(pw, sol) for sol, pw in zip(sols, lvl)]
        return sols

    sols = apply(rhss)
    resid = [rhs - (sol + _mm3(lm, sol)) for rhs, sol, lm in zip(rhss, sols, lms)]
    return [sol + cor for sol, cor in zip(sols, apply(resid))]


def _head_rmsnorm(o, w):
    return o * lax.rsqrt(jnp.mean(o * o, axis=-1, keepdims=True) + EPS) * w


def _token_shift(z, prev_row, mu):
    prev = jnp.where(_iota2(z.shape, 0) == 0, prev_row, pltpu.roll(z, 1, 0))
    return z + mu * (prev - z)


_RW_R, _RW_D, _RW_K, _RW_V, _RW_KK, _RW_NKKA = range(6)


def _window_transpose4_a(s, roll=None):
    roll = roll or (lambda x, k: pltpu.roll(x, k, 1))
    half = s[0].shape[1] // 2
    lo_half = _iota2(s[0].shape, 1) < half
    return (jnp.where(lo_half, s[0], roll(s[2], half)), jnp.where(lo_half, s[1], roll(s[3], half)),
            jnp.where(lo_half, roll(s[0], half), s[2]), jnp.where(lo_half, roll(s[1], half), s[3]))


def _window_transpose4_b(a, roll=None):
    roll = roll or (lambda x, k: pltpu.roll(x, k, 1))
    w = a[0].shape[1]
    quarter = w // 4
    even_win = jnp.bitwise_and(_iota2(a[0].shape, 1), quarter) == 0
    return (jnp.where(even_win, a[0], roll(a[1], quarter)), jnp.where(even_win, roll(a[0], w - quarter), a[1]),
            jnp.where(even_win, a[2], roll(a[3], quarter)), jnp.where(even_win, roll(a[2], w - quarter), a[3]))


def _window_transpose4(s, roll=None):
    return _window_transpose4_b(_window_transpose4_a(s, roll), roll)


def _rw_scan_kernel(zr_ref, zk_ref, zv_ref, zl_ref, mu_ref, sh_ref, w0_ref, a0_ref, wup_ref, aup_ref,
                    gup_ref, par_ref, s0_ref, out_ref, s_ref, zs_scr, prev_scr, seq_scr, o_scr, bs_scr,
                    *, steps, pitch, n, heads, bpl):
    @pl.when(pl.program_id(1) == 0)
    def _():
        s_ref[...] = s0_ref[...]
        prev_scr[...] = sh_ref[...]

    used = bpl * heads
    width = n * heads
    last = pl.ds(steps - 1, 1)
    for qi, z_ref in enumerate((zr_ref, zk_ref, zv_ref)):
        cols = slice(qi * width, (qi + 1) * width)
        for b in range(bpl):
            zs_scr[qi, b] = _token_shift(z_ref[b], prev_scr[b, :, cols], mu_ref[:, cols])
            prev_scr[b, :, cols] = z_ref[b, last, :]
    lcols = slice(3 * width, 3 * width + zl_ref.shape[2])
    shifted = []
    for b in range(bpl):
        shifted.append(_token_shift(zl_ref[b], prev_scr[b, :, lcols], mu_ref[:, lcols]))
        prev_scr[b, :, lcols] = zl_ref[b, last, :]
    zl = shifted[0] if bpl == 1 else jnp.concatenate(shifted, axis=0)
    w_lora, a_lora = wup_ref.shape[0], aup_ref.shape[0]
    xw, xa, xg = zl[:, :w_lora], zl[:, w_lora:w_lora + a_lora], zl[:, w_lora + a_lora:]
    w = -_softplus(-(w0_ref[...] + _mm(jnp.tanh(xw), wup_ref[...]))) - 0.5
    dag = (jnp.exp(-jnp.exp(w)), _sigmoid(a0_ref[...] + _mm(xa, aup_ref[...])), _mm(_sigmoid(xg), gup_ref[...]))
    for qi, val in enumerate(dag):
        for b in range(bpl):
            zs_scr[3 + qi, b] = val[b * steps:(b + 1) * steps]
    d_ref, a_ref, g_ref = zs_scr.at[3], zs_scr.at[4], zs_scr.at[5]

    def to_lanes(ref, j):
        pieces = [ref[b, :, j * heads:(j + 1) * heads] for b in range(bpl)]
        if used < LANES:
            pieces.append(jnp.zeros((steps, LANES - used), F32))
        return pieces[0] if len(pieces) == 1 else jnp.concatenate(pieces, axis=1)

    per = LANES // heads
    fast = bpl == 4 and per == 4

    def half_slabs(ref, j0):
        if fast:
            cols = slice((j0 // per) * LANES, (j0 // per + 1) * LANES)
            return _window_transpose4_a([ref[b, :, cols] for b in range(bpl)])
        return [to_lanes(ref, j0)]

    r_ref, k_ref, v_ref = zs_scr.at[0], zs_scr.at[1], zs_scr.at[2]
    sources = (r_ref, k_ref, a_ref, d_ref, v_ref)
    kkp, kap, rkp, lnw, lnb = (par_ref[i] for i in range(5))
    acc = [None, None]

    def finish(j0, halves):
        rs, ks, as_, ds_, vs_ = (_window_transpose4_b(h) if fast else h for h in halves)
        for w, (r, k, a) in enumerate(zip(rs, ks, as_)):
            j = j0 + w
            rows = pl.ds(j * pitch, steps)
            kraw = k * kkp[j:j + 1]
            k2 = k * (1.0 + (a - 1.0) * kap[j:j + 1])
            seq_scr[_RW_R, rows, :] = r
            seq_scr[_RW_D, rows, :] = ds_[w]
            seq_scr[_RW_V, rows, :] = vs_[w]
            seq_scr[_RW_K, rows, :] = k2
            seq_scr[_RW_KK, rows, :] = kraw
            seq_scr[_RW_NKKA, rows, :] = a
            for slot, term in enumerate((kraw * kraw, r * k2 * rkp[j:j + 1])):
                acc[slot] = term if acc[slot] is None else acc[slot] + term

    pending = None
    for j0 in range(0, n, per if fast else 1):
        halves = [half_slabs(ref, j0) for ref in sources]
        if pending is not None:
            finish(*pending)
        pending = (j0, halves)
    finish(*pending)
    sumsq, bsum = acc
    inv = lax.rsqrt(sumsq + L2_EPS)
    bs_scr[...] = bsum
    for j in range(n):
        rows = pl.ds(j * pitch, steps)
        kk = seq_scr[_RW_KK, rows, :] * inv
        seq_scr[_RW_KK, rows, :] = kk
        seq_scr[_RW_NKKA, rows, :] = -(kk * seq_scr[_RW_NKKA, rows, :])

    def row(q, j, t):
        return seq_scr[q, pl.ds(j * pitch + t, 1), :]

    sa0 = s_ref[0] * row(_RW_KK, 0, 0)
    for j in range(1, n):
        sa0 = sa0 + s_ref[j] * row(_RW_KK, j, 0)

    def step(t, sa):
        tn = jnp.minimum(t + 1, steps - 1)
        tile = pl.ds(t, n, stride=pitch)
        v_t = seq_scr[_RW_V, tile, :]
        o = None
        sa_next = None
        for j in range(n):
            sj = s_ref[j] * row(_RW_D, j, t) + sa * row(_RW_NKKA, j, t) + v_t * row(_RW_K, j, t)
            s_ref[j] = sj
            term = sj * row(_RW_R, j, t)
            o = term if o is None else o + term
            nxt = sj * row(_RW_KK, j, tn)
            sa_next = nxt if sa_next is None else sa_next + nxt
        mu = jnp.mean(o, axis=0, keepdims=True)
        c = o - mu
        var = jnp.mean(c * c, axis=0, keepdims=True)
        y = c * lax.rsqrt(var + RW_GN_EPS) * lnw + lnb
        o_scr[tile, :] = y + bs_scr[pl.ds(t, 1), :] * v_t
        return sa_next

    lax.fori_loop(0, steps, step, sa0)

    def emit(qd, tiles):
        cols = slice(qd * LANES, (qd + 1) * LANES)
        if fast:
            outs = _window_transpose4_b(tiles)
        else:
            outs = []
            for b in range(bpl):
                pieces = [t[:, b * heads:(b + 1) * heads] for t in tiles]
                outs.append(pieces[0] if per == 1 else jnp.concatenate(pieces, axis=1))
        for b in range(bpl):
            out_ref[b, :, cols] = (outs[b] * g_ref[b, :, cols]).astype(out_ref.dtype)

    pending = None
    for qd in range(n // per):
        tiles = [o_scr[pl.ds(i * pitch, steps), :] for i in range(qd * per, (qd + 1) * per)]
        if fast:
            tiles = _window_transpose4_a(tiles)
        if pending is not None:
            emit(*pending)
        pending = (qd, tiles)
    emit(*pending)


def _rw_scan(zin, rw_in, shift, q, par, s0, bpl):
    b, l, _ = zin.shape
    heads, n = q["heads"], q["n"]
    width = heads * n
    lw = rw_in - 3 * width
    assert (3 * width) % lw == 0 and lw % LANES == 0
    steps = _pick_tile(l, 32, 2 * SUBLANES)
    pitch = steps + SUBLANES
    z_spec = lambda qi: pl.BlockSpec((bpl, steps, width), lambda c, t: (c, t, qi))
    row = lambda w_: pl.BlockSpec((1, w_), lambda c, t: (0, 0))
    full = lambda t_: pl.BlockSpec(t_.shape, lambda c, t: (0, 0))
    st_spec = pl.BlockSpec((n, n, LANES), lambda c, t: (0, 0, c))
    return pl.pallas_call(
        functools.partial(_rw_scan_kernel, steps=steps, pitch=pitch, n=n, heads=heads, bpl=bpl),
        grid=(b // bpl, l // steps),
        in_specs=[z_spec(0), z_spec(1), z_spec(2),
                  pl.BlockSpec((bpl, steps, lw), lambda c, t: (c, t, 3 * width // lw)),
                  row(rw_in), pl.BlockSpec((bpl, 1, rw_in), lambda c, t: (c, 0, 0)),
                  row(width), row(width), full(q["w_up"]), full(q["a_up"]), full(q["g_up"]),
                  pl.BlockSpec((5, n, LANES), lambda c, t: (0, 0, 0)), st_spec],
        out_specs=[pl.BlockSpec((bpl, steps, width), lambda c, t: (c, t, 0)), st_spec],
        out_shape=[jax.ShapeDtypeStruct((b, l, width), BF16), jax.ShapeDtypeStruct(s0.shape, F32)],
        scratch_shapes=[pltpu.VMEM((6, bpl, steps, width), F32), pltpu.VMEM((bpl, 1, rw_in), F32),
                        pltpu.VMEM((6, n * pitch, LANES), F32), pltpu.VMEM((n * pitch, LANES), F32),
                        pltpu.VMEM((steps, LANES), F32)],
        compiler_params=_params(("arbitrary", "arbitrary"), big=True),
        name="rwkv_scan",
    )(zin, zin, zin, zin, q["mu"], shift.reshape(b, 1, rw_in), q["w0"], q["a0"],
      q["w_up"], q["a_up"], q["g_up"], par, s0)


def _index_major(t, heads, n, blocks=1, inverse=False):
    width = heads * n
    a, c = (n, heads) if inverse else (heads, n)
    parts = []
    for i in range(blocks):
        blk = t[..., i * width:(i + 1) * width]
        parts.append(blk.reshape(blk.shape[:-1] + (a, c)).swapaxes(-1, -2).reshape(blk.shape))
    if t.shape[-1] > blocks * width:
        parts.append(t[..., blocks * width:])
    return parts[0] if len(parts) == 1 else jnp.concatenate(parts, axis=-1)


def _rw_prep(p, e):
    heads, n = p["rw_r_k"].shape[1], p["rw_r_k"].shape[2]
    im = functools.partial(_index_major, heads=heads, n=n)
    w_out = p["w_out_even"][e]
    return dict(heads=heads, n=n,
                w_in=im(p["w_in_even"][e], blocks=3), w_out=jnp.swapaxes(im(jnp.swapaxes(w_out, 0, 1)), 0, 1),
                mu=im(p["rw_mu"][e:e + 1], blocks=3), w0=im(p["rw_w0"][e:e + 1]), a0=im(p["rw_a0"][e:e + 1]),
                w_up=im(p["rw_w_up"][e]), a_up=im(p["rw_a_up"][e]), g_up=im(p["rw_g_up"][e]),
                lane=[t.reshape(heads, n).T for t in (p["rw_k_k"][e], p["rw_k_a"][e], p["rw_r_k"][e],
                                                      p["rw_ln_w"][e], p["rw_ln_b"][e])])


def _rwkv7(zin, rw_in, shift, s0, q):
    b, l, _ = zin.shape
    heads, n = q["heads"], q["n"]
    bpl = min(b, LANES // heads)
    nlb, used = b // bpl, bpl * heads
    shift_im = _index_major(shift.astype(F32), heads, n, blocks=3)
    par = jnp.stack([jnp.pad(jnp.tile(t, (1, bpl)), ((0, 0), (0, LANES - used))) for t in q["lane"]])
    s0_l = s0.astype(F32).transpose(3, 2, 0, 1).reshape(n, n, nlb, used)
    s0_l = jnp.pad(s0_l, ((0, 0), (0, 0), (0, 0), (0, LANES - used))).reshape(n, n, nlb * LANES)
    out, s_l = _rw_scan(zin, rw_in, shift_im, q, par, s0_l, bpl)
    s_fin = s_l.reshape(n, n, nlb, LANES)[..., :used].reshape(n, n, b, heads).transpose(2, 3, 1, 0)
    new_shift = _index_major(zin[:, -1, :rw_in], heads, n, blocks=3, inverse=True)
    return out, new_shift.astype(shift.dtype), s_fin.astype(s0.dtype)


def _s5_param_kernel(ldt_ref, lr_ref, li_ref, bre_ref, bim_ref, are_o, aim_o, bbre_o, bbim_o):
    dt = jnp.exp(ldt_ref[...])
    lr, li = lr_ref[...], li_ref[...]
    mag = jnp.exp(lr * dt)
    ab_re, ab_im = mag * jnp.cos(li * dt), mag * jnp.sin(li * dt)
    den = lr * lr + li * li
    pr, pi_ = ab_re - 1.0, ab_im
    coef_re = (pr * lr + pi_ * li) / den
    coef_im = (pi_ * lr - pr * li) / den
    are_o[...] = ab_re
    aim_o[...] = ab_im
    for m in range(bre_ref.shape[0]):
        b_re, b_im = bre_ref[m], bim_ref[m]
        bbre_o[m] = coef_re * b_re - coef_im * b_im
        bbim_o[m] = coef_re * b_im + coef_im * b_re


def _s5_params(p, e):
    g, n, m = p["s5_b_re"].shape[1:]
    outs = pl.pallas_call(
        _s5_param_kernel,
        out_shape=[jax.ShapeDtypeStruct((g, n), F32)] * 2 + [jax.ShapeDtypeStruct((m, g, n), F32)] * 2,
        name="s5_params",
    )(p["s5_log_dt"][e].reshape(g, 1), p["s5_lambda_re"][e], p["s5_lambda_im"][e],
      p["s5_b_re"][e].transpose(2, 0, 1), p["s5_b_im"][e].transpose(2, 0, 1))
    return outs


def _block_diag(t, gs):
    g, a, b = t.shape
    t = t.reshape(g // gs, gs, a, b)
    eye = jnp.eye(gs, dtype=t.dtype)
    return jnp.einsum("sgab,gh->sgahb", t, eye).reshape(g // gs, gs * a, gs * b)


def _s5_kernel(u_ref, bre_ref, bim_ref, cre_ref, cim_ref, d_ref, gw_ref, gb_ref, are_ref, aim_ref,
               h0r_ref, h0i_ref, o_ref, hr_ref, hi_ref, u_scr, xr_scr, xi_scr, *, steps, pitch):
    @pl.when(pl.program_id(1) == 0)
    def _():
        hr_ref[...] = h0r_ref[...]
        hi_ref[...] = h0i_ref[...]

    nb = u_ref.shape[0]
    nk = xr_scr.shape[0]
    lanes = [slice(k * LANES, (k + 1) * LANES) for k in range(nk)]
    for b in range(nb):
        u_scr[pl.ds(b * pitch, steps), :] = u_ref[b]
        u_scr[pl.ds(b * pitch + steps, pitch - steps), :] = jnp.zeros((pitch - steps, LANES), F32)
    u2 = u_scr[...]
    bu_re = _mm(u2, bre_ref[0])
    bu_im = _mm(u2, bim_ref[0])
    for k in range(nk):
        xr_scr[k] = bu_re[:, lanes[k]]
        xi_scr[k] = bu_im[:, lanes[k]]
    a_re = [are_ref[0, :, lanes[k]] for k in range(nk)]
    a_im = [aim_ref[0, :, lanes[k]] for k in range(nk)]

    def step(t, carry):
        rows = pl.ds(t, nb, stride=pitch)
        new = []
        for k in range(nk):
            h_re, h_im = carry[2 * k], carry[2 * k + 1]
            n_re = a_re[k] * h_re - a_im[k] * h_im + xr_scr[k, rows, :]
            n_im = a_re[k] * h_im + a_im[k] * h_re + xi_scr[k, rows, :]
            xr_scr[k, rows, :] = n_re
            xi_scr[k, rows, :] = n_im
            new += [n_re, n_im]
        return tuple(new)

    init = tuple(r[:, lanes[k]] for k in range(nk) for r in (hr_ref, hi_ref))
    fin = lax.fori_loop(0, steps, step, init)
    for k in range(nk):
        hr_ref[:, lanes[k]] = fin[2 * k]
        hi_ref[:, lanes[k]] = fin[2 * k + 1]
    hs_re = jnp.concatenate([xr_scr[k] for k in range(nk)], axis=1) if nk > 1 else xr_scr[0]
    hs_im = jnp.concatenate([xi_scr[k] for k in range(nk)], axis=1) if nk > 1 else xi_scr[0]
    y = _mm(hs_re, cre_ref[0]) - _mm(hs_im, cim_ref[0]) + d_ref[0] * u2
    yg = 0.5 * y * (1.0 + jnp.tanh(0.7978845608028654 * (y + 0.044715 * (y * y * y))))
    out = yg * _sigmoid(_mm(yg, gw_ref[0]) + gb_ref[0])
    for b in range(nb):
        o_ref[b] = out[b * pitch:b * pitch + steps].astype(o_ref.dtype)


def _s5(zin, col0, h_re, h_im, p, e):
    b, l, _ = zin.shape
    g, n, m = p["s5_b_re"].shape[1:]
    width = g * m
    gs = LANES // m
    ns = g // gs
    sn = gs * n
    ab_re, ab_im, bb_re, bb_im = _s5_params(p, e)
    bd_bre = _block_diag(bb_re.transpose(1, 0, 2), gs)
    bd_bim = _block_diag(bb_im.transpose(1, 0, 2), gs)
    bd_cre = _block_diag(p["s5_c_re"][e].transpose(0, 2, 1), gs)
    bd_cim = _block_diag(p["s5_c_im"][e].transpose(0, 2, 1), gs)
    bd_gw = _block_diag(p["s5_glu_w"][e], gs)
    steps = _pick_tile(l, 256, 2 * SUBLANES)
    pitch = steps + SUBLANES
    cb0 = col0 // LANES
    slab = lambda r, c: pl.BlockSpec((1, r, c), lambda s, t: (s, 0, 0))
    st_spec = pl.BlockSpec((b, sn), lambda s, t: (0, s))
    out, hr, hi = pl.pallas_call(
        functools.partial(_s5_kernel, steps=steps, pitch=pitch),
        grid=(ns, l // steps),
        in_specs=[pl.BlockSpec((b, steps, LANES), lambda s, t: (0, t, cb0 + s)),
                  slab(LANES, sn), slab(LANES, sn), slab(sn, LANES), slab(sn, LANES),
                  slab(1, LANES), slab(LANES, LANES), slab(1, LANES), slab(1, sn), slab(1, sn),
                  st_spec, st_spec],
        out_specs=[pl.BlockSpec((b, steps, LANES), lambda s, t: (0, t, s)), st_spec, st_spec],
        out_shape=[jax.ShapeDtypeStruct((b, l, width), BF16),
                   jax.ShapeDtypeStruct((b, g * n), F32), jax.ShapeDtypeStruct((b, g * n), F32)],
        scratch_shapes=[pltpu.VMEM((b * pitch, LANES), F32),
                        pltpu.VMEM((sn // LANES, b * pitch, LANES), F32),
                        pltpu.VMEM((sn // LANES, b * pitch, LANES), F32)],
        compiler_params=_params(("arbitrary", "arbitrary"), big=True),
        name="s5",
    )(zin, bd_bre, bd_bim, bd_cre, bd_cim, p["s5_d"][e].reshape(ns, 1, LANES), bd_gw,
      p["s5_glu_b"][e].reshape(ns, 1, LANES), ab_re.reshape(ns, 1, sn), ab_im.reshape(ns, 1, sn),
      h_re.astype(F32).reshape(b, g * n), h_im.astype(F32).reshape(b, g * n))
    return (out, hr.reshape(b, g, n).astype(h_re.dtype), hi.reshape(b, g, n).astype(h_im.dtype))


def _gdn_kernel(q_ref, k_ref, v_ref, y_ref, wq_ref, wk_ref, wv_ref, bq_ref, bk_ref, bv_ref,
                zab_ref, alog_ref, dtb_ref, nw_ref, s0_ref, o_ref, s_ref, cq_scr, ck_scr, cv_scr,
                *, hb, dk, dv, kconv):
    t = pl.program_id(2)

    @pl.when(t == 0)
    def _():
        s_ref[0] = s0_ref[0]
        cq_scr[...] = bq_ref[0]
        ck_scr[...] = bk_ref[0]
        cv_scr[...] = bv_ref[0]

    c = q_ref.shape[1]

    def conv(x_ref, w_ref, carry_scr):
        x = x_ref[0]
        full = jnp.concatenate([carry_scr[...], x], axis=0)
        carry_scr[...] = x_ref[0, pl.ds(c - SUBLANES, SUBLANES), :]
        acc = None
        for j in range(kconv):
            sh = kconv - 1 - j
            src = full if sh == 0 else pltpu.roll(full, sh, 0)
            term = src[SUBLANES:SUBLANES + c] * w_ref[pl.ds(j, 1), :]
            acc = term if acc is None else acc + term
        return _silu(acc)

    qa = conv(q_ref, wq_ref, cq_scr)
    ka = conv(k_ref, wk_ref, ck_scr)
    va = conv(v_ref, wv_ref, cv_scr)
    incl = _tril(c)
    strict = _tril(c, strict=True)
    hs = range(hb)
    qs, ks, vs = [], [], []
    for hl in hs:
        q = qa[:, hl * dk:(hl + 1) * dk]
        k = ka[:, hl * dk:(hl + 1) * dk]
        qs.append(q * lax.rsqrt(jnp.sum(q * q, axis=-1, keepdims=True) + L2_EPS) * (dk ** -0.5))
        ks.append(k * lax.rsqrt(jnp.sum(k * k, axis=-1, keepdims=True) + L2_EPS))
        vs.append(va[:, hl * dv:(hl + 1) * dv])
    z = zab_ref[0]
    gcum = _mm01(incl.astype(F32), -jnp.exp(alog_ref[...]) * _softplus(z + dtb_ref[...]))
    beta = _sigmoid(z)
    gcum_t = gcum.T
    gc_c = [gcum[:, hl:hl + 1] for hl in hs]
    beta_c = [beta[:, hb + hl:hb + hl + 1] for hl in hs]
    gc_r = [gcum_t[hl:hl + 1, :] for hl in hs]
    g_last = [g[:, c - 1:c] for g in gc_r]
    kbs = [k * b for k, b in zip(ks, beta_c)]
    dmask = [jnp.where(incl, jnp.exp(jnp.where(incl, gc - gr, 0.0)), 0.0) for gc, gr in zip(gc_c, gc_r)]
    kk = [_mm_nt(kb, k) for kb, k in zip(kbs, ks)]
    qk = [_mm_nt(q, k) for q, k in zip(qs, ks)]
    ms = [jnp.where(strict, a * d, 0.0) for a, d in zip(kk, dmask)]
    egc = [jnp.exp(g) for g in gc_c]
    rhss = [jnp.concatenate([v * b, kb * e], axis=1) for v, b, kb, e in zip(vs, beta_c, kbs, egc)]
    sols = _unit_lower_solve(ms, rhss)
    attn = [a * d for a, d in zip(qk, dmask)]
    s_old = [s_ref[0, hl] for hl in hs]
    ws = [_mm(sol[:, dv:], s) for sol, s in zip(sols, s_old)]
    qss = [_mm(q * e, s) for q, e, s in zip(qs, egc, s_old)]
    v_new = [sol[:, :dv] - w for sol, w in zip(sols, ws)]
    av = [_mm(a, vn) for a, vn in zip(attn, v_new)]
    kdec = [k * jnp.exp(gl - gc) for k, gl, gc in zip(ks, g_last, gc_c)]
    kv = [_mm_tn(kd, vn) for kd, vn in zip(kdec, v_new)]
    states = [s * jnp.exp(gl) + x for s, gl, x in zip(s_old, g_last, kv)]
    outs = [_head_rmsnorm(a + b, nw_ref[...]) * _silu(y_ref[0, :, hl * dv:(hl + 1) * dv])
            for hl, a, b in zip(hs, qss, av)]
    o_ref[0] = (outs[0] if hb == 1 else jnp.concatenate(outs, axis=1)).astype(o_ref.dtype)
    s_ref[0] = jnp.stack(states, axis=0)


def _gdn(zmain, zab, conv_buf, s0, p, o, chunk):
    b, l, _ = zmain.shape
    heads = p["gdn_a_log"].shape[1]
    dv = p["gdn_norm_w"].shape[1]
    kconv, conv_ch = p["gdn_conv_w"].shape[1:]
    vw = heads * dv
    kw = (conv_ch - vw) // 2
    dk = kw // heads
    hb = heads
    nhb = heads // hb
    nc = l // chunk
    lane_row = lambda t: jnp.pad(t, (0, LANES - heads)).reshape(1, LANES)
    cbuf = jnp.pad(conv_buf.astype(F32), ((0, 0), (SUBLANES - (kconv - 1), 0), (0, 0)))
    wq, wk = hb * dk, hb * dk
    wv = hb * dv
    qoff, koff, voff, yoff = 0, kw // wk, 2 * kw // wv, (2 * kw + vw) // wv
    col = lambda w_, off: pl.BlockSpec((1, chunk, w_), lambda i, h, t: (i, t, off + h))
    cw = lambda w_, off: pl.BlockSpec((None, kconv, w_), lambda i, h, t: (o, 0, off + h))
    cb = lambda w_, off: pl.BlockSpec((1, SUBLANES, w_), lambda i, h, t: (i, 0, off + h))
    st = pl.BlockSpec((1, hb, dk, dv), lambda i, h, t: (i, h, 0, 0))
    out, s_fin = pl.pallas_call(
        functools.partial(_gdn_kernel, hb=hb, dk=dk, dv=dv, kconv=kconv),
        grid=(b, nhb, nc),
        in_specs=[col(wq, qoff), col(wk, koff), col(wv, voff), col(wv, yoff),
                  cw(wq, qoff), cw(wk, koff), cw(wv, voff),
                  cb(wq, qoff), cb(wk, koff), cb(wv, voff),
                  pl.BlockSpec((1, chunk, LANES), lambda i, h, t: (i, t, 0)),
                  pl.BlockSpec((1, LANES), lambda i, h, t: (0, 0)),
                  pl.BlockSpec((1, LANES), lambda i, h, t: (0, 0)),
                  pl.BlockSpec((1, dv), lambda i, h, t: (0, 0)),
                  st],
        out_specs=[pl.BlockSpec((1, chunk, wv), lambda i, h, t: (i, t, h)), st],
        out_shape=[jax.ShapeDtypeStruct((b, l, vw), BF16), jax.ShapeDtypeStruct((b, heads, dk, dv), F32)],
        scratch_shapes=[pltpu.VMEM((SUBLANES, wq), F32), pltpu.VMEM((SUBLANES, wk), F32),
                        pltpu.VMEM((SUBLANES, wv), F32)],
        compiler_params=_params(("arbitrary", "arbitrary", "arbitrary")),
        name="gdn",
    )(zmain, zmain, zmain, zmain, p["gdn_conv_w"], p["gdn_conv_w"], p["gdn_conv_w"],
      cbuf, cbuf, cbuf, zab, lane_row(p["gdn_a_log"][o]), lane_row(p["gdn_dt_bias"][o]),
      p["gdn_norm_w"][o:o + 1], s0.astype(F32))
    tail = jnp.concatenate([conv_buf.astype(F32), zmain[:, -(kconv - 1):, :conv_ch]], axis=1)[:, -(kconv - 1):]
    return out, tail.astype(conv_buf.dtype), s_fin.astype(s0.dtype)


def _hgrn_kernel(q_ref, f_ref, i_ref, og_ref, lbl_ref, nw_ref, s0_ref, o_ref, s_ref, *, hb, dk, dv, layer):
    @pl.when(pl.program_id(2) == 0)
    def _():
        s_ref[0] = s0_ref[0]

    c = q_ref.shape[1]
    logits = lbl_ref[...]
    mx = jnp.max(logits, axis=0, keepdims=True)
    ex = jnp.exp(logits - mx)
    den = jnp.sum(ex, axis=0, keepdims=True)
    lb_all = jnp.zeros_like(den)
    for r in range(1, layer + 1):
        lb_all = lb_all + ex[r:r + 1] / den
    tril_f = _tril(c).astype(F32)
    nb = max(c // SUB, 1)
    sb = min(SUB, c)
    rowi = _iota2((sb, 1), 0)
    eye = _iota2((dk, dk), 0) == _iota2((dk, dk), 1)
    hs = range(hb)
    ksl = [slice(hl * dk, (hl + 1) * dk) for hl in hs]
    vsl = [slice(hl * dv, (hl + 1) * dv) for hl in hs]
    q = q_ref[0]
    f = lb_all + (1.0 - lb_all) * _sigmoid(f_ref[0])
    k = 1.0 - f
    v = i_ref[0]
    bcum = _mm01(tril_f, jnp.log(f))
    qe = q * jnp.exp(bcum)
    s_old = [s_ref[0, hl] for hl in hs]
    inter = [_mm(qe[:, ksl[hl]], s_old[hl]) for hl in hs]
    blocks = [[] for _ in hs]
    for bi in range(nb):
        r0 = bi * sb
        qi, ki, vi, bb = q[r0:r0 + sb], k[r0:r0 + sb], v[r0:r0 + sb], bcum[r0:r0 + sb]
        pieces = [[] for _ in hs]
        for p0 in range(0, sb, SUBLANES):
            p1 = min(p0 + SUBLANES, sb)
            qp, bp_ = qi[p0:p1], bb[p0:p1]
            acc = [inter[hl][r0 + p0:r0 + p1] for hl in hs]
            for si in range(p1):
                diff = bp_ - bb[si:si + 1]
                if si > p0:
                    msk = rowi[p0:p1] >= si
                    dec = jnp.where(msk, jnp.exp(jnp.where(msk, diff, 0.0)), 0.0)
                else:
                    dec = jnp.exp(diff)
                prod = qp * ki[si:si + 1] * dec
                for hl in hs:
                    colv = jnp.sum(prod[:, ksl[hl]], axis=-1, keepdims=True)
                    acc[hl] = acc[hl] + colv * vi[si:si + 1, vsl[hl]]
            for hl in hs:
                pieces[hl].append(acc[hl])
        acc = [pc[0] if len(pc) == 1 else jnp.concatenate(pc, axis=0) for pc in pieces]
        if bi > 0:
            bref = bcum[r0 - 1:r0]
            qsc = qi * jnp.exp(bb - bref)
            ksc = k[:r0] * jnp.exp(bref - bcum[:r0])
            sc = [_mm3(qsc[:, ksl[hl]], ksc[:, ksl[hl]], ((1,), (1,))) for hl in hs]
            acc = [acc[hl] + _mm(sc[hl], v[:r0, vsl[hl]]) for hl in hs]
        for hl in hs:
            blocks[hl].append(acc[hl])
    b_last = bcum[c - 1:c]
    kd = k * jnp.exp(b_last - bcum)
    e_last = jnp.exp(b_last)
    kv = [_mm_tn(kd[:, ksl[hl]], v[:, vsl[hl]]) for hl in hs]
    states = []
    outs = []
    og = og_ref[0]
    for hl in hs:
        e_col = jnp.sum(jnp.where(eye, e_last[:, ksl[hl]], 0.0), axis=1, keepdims=True)
        states.append(e_col * s_old[hl] + kv[hl])
        o = blocks[hl][0] if nb == 1 else jnp.concatenate(blocks[hl], axis=0)
        outs.append(_head_rmsnorm(o, nw_ref[...]) * _sigmoid(og[:, vsl[hl]]))
    o_ref[0] = (outs[0] if hb == 1 else jnp.concatenate(outs, axis=1)).astype(o_ref.dtype)
    s_ref[0] = jnp.stack(states, axis=0)


def _hgrn2(z, s0, p, o, layer, chunk):
    b, l, zw = z.shape
    dv = p["hg_norm_w"].shape[1]
    depth, kw = p["hg_lb_logits"].shape
    vw = (zw - 2 * kw) // 2
    heads = vw // dv
    dk = kw // heads
    hb = _pick_tile(heads, HGRN_HEADS_PER_STEP, 1)
    nhb = heads // hb
    wk, wv = hb * dk, hb * dv
    col = lambda w_, off: pl.BlockSpec((1, chunk, w_), lambda i, h, t: (i, t, off + h))
    st = pl.BlockSpec((1, hb, dk, dv), lambda i, h, t: (i, h, 0, 0))
    out, s_fin = pl.pallas_call(
        functools.partial(_hgrn_kernel, hb=hb, dk=dk, dv=dv, layer=layer),
        grid=(b, nhb, l // chunk),
        in_specs=[col(wk, 0), col(wk, kw // wk), col(wv, 2 * kw // wv), col(wv, (2 * kw + vw) // wv),
                  pl.BlockSpec((depth, wk), lambda i, h, t: (0, h)),
                  pl.BlockSpec((1, dv), lambda i, h, t: (0, 0)),
                  st],
        out_specs=[pl.BlockSpec((1, chunk, wv), lambda i, h, t: (i, t, h)), st],
        out_shape=[jax.ShapeDtypeStruct((b, l, vw), BF16), jax.ShapeDtypeStruct((b, heads, dk, dv), F32)],
        compiler_params=_params(("arbitrary", "arbitrary", "arbitrary")),
        name="hgrn2",
    )(z, z, z, z, p["hg_lb_logits"].astype(F32), p["hg_norm_w"][o:o + 1], s0.astype(F32))
    return out, s_fin.astype(s0.dtype)


def _trunk(xs, mods, sts, p, prep):
    groups = (0, 1)
    dims = [x.shape[:2] for x in xs]
    d = xs[0].shape[2]
    rows = [b * l for b, l in dims]
    depth = p["w_mod"].shape[0]
    d_ff = p["w_ffn_out"].shape[2]
    chunks = [min(CHUNK, l) for _, l in dims]
    new_st = [[[] for _ in range(7)] for _ in groups]
    x2 = [x.reshape(r, d) for x, r in zip(xs, rows)]

    def mm(x_pair, w, prefix, cols, tn, *, epi="plain", res_pair=None, mod9s=None, idx=None, **kw):
        res2 = gate = gate2 = None
        if epi == "res":
            gate = mod9s[0][:, 3 * idx + 2].reshape(dims[0][0], 1, d)
            gate2 = jnp.repeat(mod9s[1][:, 3 * idx + 2], dims[1][1], axis=0).reshape(1, rows[1], d)
            res2 = res_pair[1]
        return _matmul(x_pair[0], w, prefix, cols, tn, epi=epi, rows_per_gate=dims[0][1],
                       res=None if res_pair is None else res_pair[0], gate=gate,
                       rider=(x_pair[1], res2, gate2), **kw)

    def norm(x2, g, mod9s, idx):
        return [_norm(x2[i].reshape(*dims[i], d), g, mod9s[i], idx).reshape(rows[i], d) for i in groups]

    def ffn(x2, mod9s, lyr, idx, slot):
        h = norm(x2, p["norm_g"][lyr, idx], mod9s, idx)
        tn = _pick_tile(d_ff, 512)
        act = mm(h, p["w_ffn_in"], (lyr, slot), ((0, d_ff // tn), d_ff // tn), tn,
                 epi="swiglu", out_dtype=BF16, tm_pref=1024, w_buffers=1)
        tn2 = _pick_tile(d, 512)
        return mm(act, p["w_ffn_out"], (lyr, slot), ((0,), d // tn2), tn2, epi="res",
                  res_pair=x2, mod9s=mod9s, idx=idx, coef=0.5, w_buffers=1)

    for lyr in range(depth):
        mod9s = [m[lyr] for m in mods]
        x2 = ffn(x2, mod9s, lyr, 0, 0)
        h = norm(x2, p["norm_g"][lyr, 1], mod9s, 1)
        mix = [None, None]
        if lyr % 2 == 0:
            e = lyr // 2
            even_in = p["w_in_even"].shape[2]
            rw_in = p["rw_mu"].shape[1]
            tn = _pick_tile(even_in, 1280, 256)
            q = prep["even"][e]
            zin = mm(h, q["w_in"], (), ((0,), even_in // tn), tn, w_buffers=1)
            for i in groups:
                rw_s, rw_sh, s5_re, s5_im = sts[i][:4]
                z = zin[i].reshape(*dims[i], even_in)
                oa, sh, s = _rwkv7(z, rw_in, rw_sh[e], rw_s[e], q)
                ob, hr, hi = _s5(z, rw_in, s5_re[e], s5_im[e], p, e)
                for slot, val in zip((0, 1, 2, 3), (s, sh, hr, hi)):
                    new_st[i][slot].append(val)
                mix[i] = (oa.reshape(rows[i], -1), ob.reshape(rows[i], -1))
            w_out, widx = q["w_out"], ()
        else:
            o = lyr // 2
            w_gdn, w_hg, w_ab = prep["odd"][o]
            gdn_main = w_gdn.shape[1]
            tn = _pick_tile(gdn_main, 1024)
            zmain = mm(h, w_gdn, (), ((0,), gdn_main // tn), tn, tm_pref=1024, w_buffers=1)
            zab = mm(h, w_ab, (), ((0,), 1), LANES)
            hg_in = w_hg.shape[1]
            tn = _pick_tile(hg_in, 1024)
            zhg = mm(h, w_hg, (), ((0,), hg_in // tn), tn, tm_pref=1024, w_buffers=1)
            for i in groups:
                gdn_s, gdn_cv, hg_s = sts[i][4:]
                oc, cv, s = _gdn(zmain[i].reshape(*dims[i], gdn_main), zab[i].reshape(*dims[i], LANES),
                                 gdn_cv[o], gdn_s[o], p, o, chunks[i])
                od, sh_ = _hgrn2(zhg[i].reshape(*dims[i], hg_in), hg_s[o], p, o, lyr, chunks[i])
                for slot, val in zip((4, 5, 6), (s, cv, sh_)):
                    new_st[i][slot].append(val)
                mix[i] = (oc.reshape(rows[i], -1), od.reshape(rows[i], -1))
            w_out, widx = p["w_out_odd"], (o,)
        tn = _pick_tile(d, 512)
        x2 = mm(mix, w_out, widx, ((0,), d // tn), tn, epi="res", res_pair=x2, mod9s=mod9s, idx=1,
                coef=1.0, tm_pref=1024)
        x2 = ffn(x2, mod9s, lyr, 2, 1)
    ys = [_norm(x2[i].reshape(*dims[i], d), p["norm_final"], out_dtype=xs[i].dtype) for i in groups]
    return ys, [tuple(jnp.stack(v) for v in new_st[i]) for i in groups]


def kernel(x_prompt, x_sample, c_prompt, c_sample, state_rwkv, state_rwkv_shift, state_s5_re, state_s5_im, state_gdn, cache_gdn_conv, state_hgrn, w_mod, b_mod, norm_g, norm_final, w_ffn_in, w_ffn_out, w_in_even, w_out_even, rw_mu, rw_w0, rw_w_up, rw_a0, rw_a_up, rw_g_up, rw_k_k, rw_k_a, rw_r_k, rw_ln_w, rw_ln_b, s5_lambda_re, s5_lambda_im, s5_log_dt, s5_b_re, s5_b_im, s5_c_re, s5_c_im, s5_d, s5_glu_w, s5_glu_b, w_in_odd, w_out_odd, gdn_conv_w, gdn_a_log, gdn_dt_bias, gdn_norm_w, hg_lb_logits, hg_norm_w):
    p = dict(w_mod=w_mod, b_mod=b_mod, norm_g=norm_g, norm_final=norm_final,
             w_ffn_in=w_ffn_in, w_ffn_out=w_ffn_out, w_in_even=w_in_even, w_out_even=w_out_even,
             rw_mu=rw_mu, rw_w0=rw_w0, rw_w_up=rw_w_up, rw_a0=rw_a0, rw_a_up=rw_a_up, rw_g_up=rw_g_up,
             rw_k_k=rw_k_k, rw_k_a=rw_k_a, rw_r_k=rw_r_k, rw_ln_w=rw_ln_w, rw_ln_b=rw_ln_b,
             s5_lambda_re=s5_lambda_re, s5_lambda_im=s5_lambda_im, s5_log_dt=s5_log_dt,
             s5_b_re=s5_b_re, s5_b_im=s5_b_im, s5_c_re=s5_c_re, s5_c_im=s5_c_im, s5_d=s5_d,
             s5_glu_w=s5_glu_w, s5_glu_b=s5_glu_b, w_in_odd=w_in_odd, w_out_odd=w_out_odd,
             gdn_conv_w=gdn_conv_w, gdn_a_log=gdn_a_log, gdn_dt_bias=gdn_dt_bias, gdn_norm_w=gdn_norm_w,
             hg_lb_logits=hg_lb_logits, hg_norm_w=hg_norm_w)
    bp, bs = x_prompt.shape[0], x_sample.shape[0]
    d = x_prompt.shape[-1]
    depth = w_mod.shape[0]
    n_even, n_odd = (depth + 1) // 2, depth // 2
    dtp = x_prompt.dtype

    bc = bp + bs
    bc_pad = -(-bc // BF16_SUBLANES) * BF16_SUBLANES
    c_all = jnp.pad(jnp.concatenate([c_prompt, c_sample], axis=0), ((0, bc_pad - bc), (0, 0)))
    mod = _modulation(c_all, w_mod, b_mod).reshape(depth, bc_pad, 9, d)
    mod_p, mod_s = mod[:, :bp], mod[:, bp:bc]

    heads = gdn_a_log.shape[1]
    conv_ch = gdn_conv_w.shape[2]
    vw = heads * gdn_norm_w.shape[1]
    gdn_main = conv_ch + vw
    gdn_in = gdn_main + 2 * heads
    prep = {"odd": [], "even": [_rw_prep(p, e) for e in range(n_even)]}
    for o in range(n_odd):
        w_gdn = w_in_odd[o][:, :gdn_main]
        w_hg = w_in_odd[o][:, gdn_in:]
        w_ab = jnp.pad(w_in_odd[o][:, gdn_main:gdn_in], ((0, 0), (0, LANES - 2 * heads)))
        prep["odd"].append((w_gdn, w_hg, w_ab))

    rw_h, rw_n = rw_r_k.shape[1], rw_r_k.shape[2]
    rw_in = rw_mu.shape[1]
    s5_g, s5_n = s5_lambda_re.shape[1], s5_lambda_re.shape[2]
    gdn_dk = (conv_ch - vw) // 2 // heads
    gdn_dv = gdn_norm_w.shape[1]
    kconv = gdn_conv_w.shape[1]
    hg_dv = hg_norm_w.shape[1]
    hg_kw = hg_lb_logits.shape[1]
    hg_vw = (w_in_odd.shape[2] - gdn_in - 2 * hg_kw) // 2
    hg_h = hg_vw // hg_dv
    hg_dk = hg_kw // hg_h
    st_prompt = (jnp.zeros((n_even, bp, rw_h, rw_n, rw_n), dtp),
                 jnp.zeros((n_even, bp, rw_in), dtp),
                 jnp.zeros((n_even, bp, s5_g, s5_n), dtp),
                 jnp.zeros((n_even, bp, s5_g, s5_n), dtp),
                 jnp.zeros((n_odd, bp, heads, gdn_dk, gdn_dv), dtp),
                 jnp.zeros((n_odd, bp, kconv - 1, conv_ch), dtp),
                 jnp.zeros((n_odd, bp, hg_h, hg_dk, hg_dv), dtp))
    st_sample = (state_rwkv, state_rwkv_shift, state_s5_re, state_s5_im, state_gdn, cache_gdn_conv, state_hgrn)
    ys, ns = _trunk((x_prompt, x_sample), (mod_p, mod_s), (st_prompt, st_sample), p, prep)
    return tuple(ys) + tuple(ns[0]) + tuple(ns[1])
```
